```python
import math
import jax, jax.numpy as jnp
from jax import lax
import numpy as np

D_MODEL = 1024
BATCH = 8
SEQ = 4096
DEPTH = 1

CHUNK = 64
N_MEM = 256
ATT_HEADS = 8
ATT_KV_HEADS = 2
ATT_HEAD_DIM = 64
IDX_HEADS = 4
IDX_DIM = 64
TOPK_MAX = 256
Q_BLOCK = 128
ROPE_THETA = 10000.0
S5_WIDTH = 512
S5_GROUP = 16
S5_GROUPS = S5_WIDTH // S5_GROUP
S5_STATE = 64
MEM_HEADS = 4
MEM_HEAD_DIM = 128
FFN_DIM = 2816
CONV_WIDTH = 3
N_BRANCH = 3
ALPHA = (2.0 * DEPTH) ** 0.25
BETA = (8.0 * DEPTH) ** -0.25
LN_EPS = 1e-5
NEG = -1e30

IN_WIDTHS = (ATT_HEADS * ATT_HEAD_DIM, ATT_KV_HEADS * ATT_HEAD_DIM, ATT_KV_HEADS * ATT_HEAD_DIM,
             IDX_HEADS * IDX_DIM, IDX_DIM, IDX_HEADS, S5_WIDTH, MEM_HEADS * MEM_HEAD_DIM)
IN_WIDTH = sum(IN_WIDTHS)

kernel_name = "hybrid_dsa_s5_memattn_convffn_deepnorm"


def layer_norm(x, g, b):
    xf = x.astype(jnp.float32)
    mu = jnp.mean(xf, axis=-1, keepdims=True)
    var = jnp.mean(jnp.square(xf - mu), axis=-1, keepdims=True)
    return ((xf - mu) * lax.rsqrt(var + LN_EPS) * g.astype(jnp.float32) + b.astype(jnp.float32)).astype(x.dtype)


def rope(x, positions):
    half = x.shape[-1] // 2
    inv_freq = ROPE_THETA ** (-jnp.arange(half, dtype=jnp.float32) / half)
    ang = positions.astype(jnp.float32)[..., None] * inv_freq
    cos = jnp.cos(ang)[:, :, None, :]
    sin = jnp.sin(ang)[:, :, None, :]
    xf = x.astype(jnp.float32)
    x1, x2 = xf[..., :half], xf[..., half:]
    return jnp.concatenate([x1 * cos - x2 * sin, x2 * cos + x1 * sin], axis=-1).astype(x.dtype)


def dsa_attention(q, k, v, q_idx, k_idx, w_idx):
    B, S = q.shape[0], q.shape[1]
    topk = min(TOPK_MAX, S // 4)
    n_blocks = S // Q_BLOCK
    rep = ATT_HEADS // ATT_KV_HEADS
    scale = ATT_HEAD_DIM ** -0.5

    def to_blocks(a):
        return a.reshape(B, n_blocks, Q_BLOCK, *a.shape[2:]).swapaxes(0, 1)

    q_b = to_blocks(q.reshape(B, S, ATT_KV_HEADS, rep, ATT_HEAD_DIM))
    qi_b = to_blocks(q_idx)
    w_b = to_blocks(w_idx)
    key_chunk = jnp.arange(S) // CHUNK
    b_ix = jnp.arange(B)[:, None, None]
    k_idx_f = k_idx.astype(jnp.float32)

    def one_block(args):
        blk, q_blk, qi_blk, w_blk = args
        t = blk * Q_BLOCK + jnp.arange(Q_BLOCK)
        q_chunk = t // CHUNK
        logits = jnp.einsum('bqhd,bsd->bqhs', qi_blk.astype(jnp.float32), k_idx_f) * (IDX_DIM ** -0.5)
        score = jnp.einsum('bqh,bqhs->bqs', w_blk.astype(jnp.float32) * (IDX_HEADS ** -0.5), jax.nn.relu(logits))
        admissible = key_chunk[None, :] <= q_chunk[:, None]
        score = jnp.where(admissible[None], score, NEG)
        _, idx = lax.top_k(score, topk)
        valid = key_chunk[idx] <= q_chunk[None, :, None]
        k_sel = k[b_ix, idx]
        v_sel = v[b_ix, idx]
        s = jnp.einsum('bqgrd,bqkgd->bqgrk', q_blk, k_sel).astype(jnp.float32) * scale
        s = jnp.where(valid[:, :, None, None, :], s, NEG)
        p = jax.nn.softmax(s, axis=-1).astype(v.dtype)
        return jnp.einsum('bqgrk,bqkgd->bqgrd', p, v_sel)

    out = lax.map(one_block, (jnp.arange(n_blocks), q_b, qi_b, w_b))
    return out.swapaxes(0, 1).reshape(B, S, ATT_HEADS * ATT_HEAD_DIM)


def s5_branch(u, lam_re, lam_im, log_dt, b_re, b_im, c_re, c_im, d_skip, w_glu, b_glu):
    B, S, _ = u.shape
    uf = u.astype(jnp.float32).reshape(B, S, S5_GROUPS, S5_GROUP)
    lam = lax.complex(jnp.minimum(lam_re.astype(jnp.float32), -1e-4), lam_im.astype(jnp.float32))
    dt = jnp.exp(log_dt.astype(jnp.float32))[:, None]
    lam_bar = jnp.exp(lam * dt)
    b_mat = lax.complex(b_re.astype(jnp.float32), b_im.astype(jnp.float32))
    c_mat = lax.complex(c_re.astype(jnp.float32), c_im.astype(jnp.float32))
    b_bar = ((lam_bar - 1.0) / lam)[..., None] * b_mat
    bu = jnp.einsum('gph,bsgh->bsgp', b_bar, uf.astype(jnp.complex64))
    a = jnp.broadcast_to(lam_bar, (1, S, S5_GROUPS, S5_STATE))

    def combine(e1, e2):
        a1, x1 = e1
        a2, x2 = e2
        return a2 * a1, a2 * x1 + x2

    _, h = lax.associative_scan(combine, (a, bu), axis=1)
    y = jnp.einsum('ghp,bsgp->bsgh', c_mat, h).real + d_skip.astype(jnp.float32).reshape(S5_GROUPS, S5_GROUP) * uf
    y = jax.nn.gelu(y.reshape(B, S, S5_WIDTH))
    y = y * jax.nn.sigmoid(y @ w_glu.astype(jnp.float32) + b_glu.astype(jnp.float32))
    return y.astype(u.dtype)


def memory_attention(q_mem, mem, w_mem_kv):
    B, S, _ = q_mem.shape
    kv = (mem @ w_mem_kv).reshape(B, mem.shape[1], 2, MEM_HEADS, MEM_HEAD_DIM)
    k_m, v_m = kv[:, :, 0], kv[:, :, 1]
    q = q_mem.reshape(B, S, MEM_HEADS, MEM_HEAD_DIM)
    s = jnp.einsum('bqhd,bmhd->bhqm', q, k_m).astype(jnp.float32) * (MEM_HEAD_DIM ** -0.5)
    p = jax.nn.softmax(s, axis=-1).astype(v_m.dtype)
    return jnp.einsum('bhqm,bmhd->bqhd', p, v_m).reshape(B, S, MEM_HEADS * MEM_HEAD_DIM)


def hybrid_mixer(x, mem, positions, w_in, w_gate, b_gate, lam_re, lam_im, log_dt, b_re, b_im,
                 c_re, c_im, d_skip, w_glu, b_glu, w_mem_kv, w_proj_a, w_proj_b, w_proj_c, w_out):
    B, S, D = x.shape
    offsets = np.cumsum(IN_WIDTHS)[:-1].tolist()
    q_att, k_att, v_att, q_idx, k_idx, w_idx, u_s5, q_mem = jnp.split(x @ w_in, offsets, axis=-1)
    q_att = rope(q_att.reshape(B, S, ATT_HEADS, ATT_HEAD_DIM), positions)
    k_att = rope(k_att.reshape(B, S, ATT_KV_HEADS, ATT_HEAD_DIM), positions)
    v_att = v_att.reshape(B, S, ATT_KV_HEADS, ATT_HEAD_DIM)
    q_idx = rope(q_idx.reshape(B, S, IDX_HEADS, IDX_DIM), positions)
    k_idx = rope(k_idx[:, :, None, :], positions)[:, :, 0]
    y_a = dsa_attention(q_att, k_att, v_att, q_idx, k_idx, w_idx) @ w_proj_a
    y_b = s5_branch(u_s5, lam_re, lam_im, log_dt, b_re, b_im, c_re, c_im, d_skip, w_glu, b_glu) @ w_proj_b
    y_c = memory_attention(q_mem, mem, w_mem_kv) @ w_proj_c
    gates = jax.nn.sigmoid((x @ w_gate + b_gate).astype(jnp.float32)).reshape(B, S, N_BRANCH, D).astype(x.dtype)
    merged = gates[:, :, 0] * y_a + gates[:, :, 1] * y_b + gates[:, :, 2] * y_c
    return merged @ w_out


def conv_ffn(x, w_up, conv_w, conv_b, w_down):
    S = x.shape[1]
    h = x @ w_up
    hp = jnp.pad(h, ((0, 0), (CONV_WIDTH - 1, 0), (0, 0)))
    h = sum(conv_w[j] * hp[:, j:j + S] for j in range(CONV_WIDTH)) + conv_b
    g, up = jnp.split(h, 2, axis=-1)
    return (jax.nn.silu(g) * up) @ w_down


def setup_inputs(seed: int = 0) -> dict:
    key = jax.random.key(seed)
    ks = jax.random.split(key, 32)
    f32 = jnp.float32
    D, L = D_MODEL, DEPTH

    def nrm(k, shape, scale):
        return jax.random.normal(k, shape, f32) * scale

    x = nrm(ks[0], (BATCH, SEQ, D), 1.0)
    mem = nrm(ks[1], (BATCH, N_MEM, D), 1.0)
    start = jax.random.randint(ks[2], (BATCH, 1), 0, 16) * CHUNK
    positions = (start + jnp.arange(SEQ, dtype=jnp.int32)[None, :]).astype(jnp.int32)
    n_idx = jnp.arange(S5_STATE, dtype=f32)
    return {
        "x": x,
        "mem": mem,
        "positions": positions,
        "w_in": nrm(ks[3], (L, D, IN_WIDTH), D ** -0.5),
        "w_gate": nrm(ks[4], (L, D, N_BRANCH * D), D ** -0.5),
        "b_gate": nrm(ks[5], (L, N_BRANCH * D), 0.02),
        "s5_lam_re": -0.5 + nrm(ks[6], (L, S5_GROUPS, S5_STATE), 0.01),
        "s5_lam_im": math.pi * n_idx + nrm(ks[7], (L, S5_GROUPS, S5_STATE), 0.01),
        "s5_log_dt": jax.random.uniform(ks[8], (L, S5_GROUPS), f32, math.log(1e-3), math.log(1e-1)),
        "s5_b_re": nrm(ks[9], (L, S5_GROUPS, S5_STATE, S5_GROUP), (0.5 / S5_GROUP) ** 0.5),
        "s5_b_im": nrm(ks[10], (L, S5_GROUPS, S5_STATE, S5_GROUP), (0.5 / S5_GROUP) ** 0.5),
        "s5_c_re": nrm(ks[11], (L, S5_GROUPS, S5_GROUP, S5_STATE), (0.5 / S5_STATE) ** 0.5),
        "s5_c_im": nrm(ks[12], (L, S5_GROUPS, S5_GROUP, S5_STATE), (0.5 / S5_STATE) ** 0.5),
        "s5_d": nrm(ks[13], (L, S5_WIDTH), 1.0),
        "w_glu": nrm(ks[14], (L, S5_WIDTH, S5_WIDTH), S5_WIDTH ** -0.5),
        "b_glu": nrm(ks[15], (L, S5_WIDTH), 0.02),
        "w_mem_kv": nrm(ks[16], (L, D, 2 * MEM_HEADS * MEM_HEAD_DIM), D ** -0.5),
        "w_proj_a": nrm(ks[17], (L, ATT_HEADS * ATT_HEAD_DIM, D), (ATT_HEADS * ATT_HEAD_DIM) ** -0.5),
        "w_proj_b": nrm(ks[18], (L, S5_WIDTH, D), S5_WIDTH ** -0.5),
        "w_proj_c": nrm(ks[19], (L, MEM_HEADS * MEM_HEAD_DIM, D), (MEM_HEADS * MEM_HEAD_DIM) ** -0.5),
        "w_out": nrm(ks[20], (L, D, D), BETA * D ** -0.5),
        "ln1_g": 1.0 + nrm(ks[21], (L, D), 0.02),
        "ln1_b": nrm(ks[22], (L, D), 0.02),
        "w_up": nrm(ks[23], (L, D, 2 * FFN_DIM), D ** -0.5),
        "conv_w": nrm(ks[24], (L, CONV_WIDTH, 2 * FFN_DIM), CONV_WIDTH ** -0.5),
        "conv_b": nrm(ks[25], (L, 2 * FFN_DIM), 0.02),
        "w_down": nrm(ks[26], (L, FFN_DIM, D), BETA * FFN_DIM ** -0.5),
        "ln2_g": 1.0 + nrm(ks[27], (L, D), 0.02),
        "ln2_b": nrm(ks[28], (L, D), 0.02),
    }


def reference(x, mem, positions, w_in, w_gate, b_gate, s5_lam_re, s5_lam_im, s5_log_dt, s5_b_re, s5_b_im,
              s5_c_re, s5_c_im, s5_d, w_glu, b_glu, w_mem_kv, w_proj_a, w_proj_b, w_proj_c, w_out,
              ln1_g, ln1_b, w_up, conv_w, conv_b, w_down, ln2_g, ln2_b):
    h = x
    for l in range(DEPTH):
        mix = hybrid_mixer(h, mem, positions, w_in[l], w_gate[l], b_gate[l], s5_lam_re[l], s5_lam_im[l],
                           s5_log_dt[l], s5_b_re[l], s5_b_im[l], s5_c_re[l], s5_c_im[l], s5_d[l],
                           w_glu[l], b_glu[l], w_mem_kv[l], w_proj_a[l], w_proj_b[l], w_proj_c[l], w_out[l])
        h = layer_norm(ALPHA * h + mix, ln1_g[l], ln1_b[l])
        f = conv_ffn(h, w_up[l], conv_w[l], conv_b[l], w_down[l])
        h = layer_norm(ALPHA * h + f, ln2_g[l], ln2_b[l])
    return h
```

```python
import functools
import math

import jax
import jax.numpy as jnp
from jax import lax
from jax.experimental import pallas as pl
from jax.experimental.pallas import tpu as pltpu

F32 = jnp.float32
BF16 = jnp.bfloat16
I32 = jnp.int32

CHUNK = 64
ATT_HEADS = 8
ATT_KV_HEADS = 2
ATT_HEAD_DIM = 64
ATT_REP = ATT_HEADS // ATT_KV_HEADS
IDX_HEADS = 4
IDX_DIM = 64
TOPK_MAX = 256
ROPE_THETA = 10000.0
S5_WIDTH = 512
S5_GROUP = 16
S5_GROUPS = S5_WIDTH // S5_GROUP
S5_STATE = 64
MEM_HEADS = 4
MEM_HEAD_DIM = 128
CONV_WIDTH = 3
N_BRANCH = 3
LN_EPS = 1e-5
NEG = -1e30
INT_MIN = -(2 ** 31)

LANES = 128
SUBLANES = 8
VMEM_LIMIT = 56 * 1024 * 1024

_Q0 = 0
_K0 = _Q0 + ATT_HEADS * ATT_HEAD_DIM
_V0 = _K0 + ATT_KV_HEADS * ATT_HEAD_DIM
_QI0 = _V0 + ATT_KV_HEADS * ATT_HEAD_DIM
_KI0 = _QI0 + IDX_HEADS * IDX_DIM
_WI0 = _KI0 + IDX_DIM
_U0 = _WI0 + IDX_HEADS
_QM0 = _U0 + S5_WIDTH
_END = _QM0 + MEM_HEADS * MEM_HEAD_DIM
_PAD = LANES - IDX_DIM - IDX_HEADS
_PU0 = _KI0 + LANES
_PQM0 = _PU0 + S5_WIDTH
_PEND = _PQM0 + MEM_HEADS * MEM_HEAD_DIM


def _params(*sem):
    return pltpu.CompilerParams(dimension_semantics=sem, vmem_limit_bytes=VMEM_LIMIT)


def _proj_kernel(x_ref, pos_ref, invf_ref, w_ref,
                 q_ref, kT_ref, v_ref, qi_ref, kiT_ref, wi_ref, u_ref, qm_ref):
    T = x_ref.shape[1]
    y = jnp.dot(x_ref[0].astype(BF16), w_ref[...], preferred_element_type=F32)

    ang = pos_ref[0].astype(F32) * invf_ref[...]
    cos = jnp.cos(ang)
    sin = jnp.sin(ang)
    lane = lax.broadcasted_iota(I32, (T, LANES), 1)
    first = (lane % ATT_HEAD_DIM) < (ATT_HEAD_DIM // 2)
    sin = jnp.where(first, -sin, sin)

    def rope(z):
        partner = jnp.where(first, pltpu.roll(z, LANES - 32, 1), pltpu.roll(z, 32, 1))
        return z * cos + partner * sin

    att_scale = ATT_HEAD_DIM ** -0.5
    for c in range(ATT_HEADS // 2):
        z = rope(y[:, _Q0 + c * LANES:_Q0 + (c + 1) * LANES]) * att_scale
        q_ref[0, 2 * c] = z[:, :ATT_HEAD_DIM].astype(BF16)
        q_ref[0, 2 * c + 1] = z[:, ATT_HEAD_DIM:].astype(BF16)
    kT_ref[0] = rope(y[:, _K0:_K0 + LANES]).T.astype(BF16)
    v_ref[0] = y[:, _V0:_V0 + LANES].astype(BF16)
    idx_scale = IDX_DIM ** -0.5
    for c in range(IDX_HEADS // 2):
        z = rope(y[:, _QI0 + c * LANES:_QI0 + (c + 1) * LANES]) * idx_scale
        qi_ref[0, 2 * c] = z[:, :IDX_DIM].astype(BF16)
        qi_ref[0, 2 * c + 1] = z[:, IDX_DIM:].astype(BF16)
    kw = y[:, _KI0:_KI0 + LANES]
    kiT_ref[0] = rope(kw).T[:IDX_DIM].astype(BF16)
    wi_ref[0] = kw[:, IDX_DIM:IDX_DIM + IDX_HEADS] * (IDX_HEADS ** -0.5)
    u_ref[...] = y[:, _PU0:_PU0 + S5_WIDTH]
    qm_ref[0] = (y[:, _PQM0:_PEND] * (MEM_HEAD_DIM ** -0.5)).astype(BF16)


def _proj(x, positions, w_in, tile):
    B, S, D = x.shape
    nt = S // tile
    w_pad = jnp.concatenate(
        [w_in[:, :_U0], jnp.zeros((D, _PAD), w_in.dtype), w_in[:, _U0:]], axis=1).astype(BF16)
    half = ATT_HEAD_DIM // 2
    inv_freq = ROPE_THETA ** (-jnp.arange(half, dtype=F32) / half)
    invf = jnp.tile(inv_freq, LANES // half)[None, :]
    pos3 = positions.reshape(B, S, 1)
    out_shape = (
        jax.ShapeDtypeStruct((B, ATT_HEADS, S, ATT_HEAD_DIM), BF16),
        jax.ShapeDtypeStruct((B, LANES, S), BF16),
        jax.ShapeDtypeStruct((B, S, LANES), BF16),
        jax.ShapeDtypeStruct((B, IDX_HEADS, S, IDX_DIM), BF16),
        jax.ShapeDtypeStruct((B, IDX_DIM, S), BF16),
        jax.ShapeDtypeStruct((B, S, IDX_HEADS), F32),
        jax.ShapeDtypeStruct((S, B * S5_WIDTH), F32),
        jax.ShapeDtypeStruct((B, S, MEM_HEADS * MEM_HEAD_DIM), BF16),
    )
    out_specs = (
        pl.BlockSpec((1, ATT_HEADS, tile, ATT_HEAD_DIM), lambda b, t: (b, 0, t, 0)),
        pl.BlockSpec((1, LANES, tile), lambda b, t: (b, 0, t)),
        pl.BlockSpec((1, tile, LANES), lambda b, t: (b, t, 0)),
        pl.BlockSpec((1, IDX_HEADS, tile, IDX_DIM), lambda b, t: (b, 0, t, 0)),
        pl.BlockSpec((1, IDX_DIM, tile), lambda b, t: (b, 0, t)),
        pl.BlockSpec((1, tile, IDX_HEADS), lambda b, t: (b, t, 0)),
        pl.BlockSpec((tile, S5_WIDTH), lambda b, t: (t, b)),
        pl.BlockSpec((1, tile, MEM_HEADS * MEM_HEAD_DIM), lambda b, t: (b, t, 0)),
    )
    return pl.pallas_call(
        _proj_kernel,
        out_shape=out_shape,
        grid=(B, nt),
        in_specs=[
            pl.BlockSpec((1, tile, D), lambda b, t: (b, t, 0)),
            pl.BlockSpec((1, tile, 1), lambda b, t: (b, t, 0)),
            pl.BlockSpec((1, LANES), lambda b, t: (0, 0)),
            pl.BlockSpec((D, _PEND), lambda b, t: (0, 0)),
        ],
        out_specs=out_specs,
        compiler_params=_params("parallel", "parallel"),
        name="proj",
    )(x, pos3, invf, w_pad)


DSA_TQ = 128
DSA_TK = 256


def _dsa_kernel(q_ref, qi_ref, wi_ref, kiT_ref, kT_ref, v_ref, o_ref,
                sc_ref, bias_ref, m_ref, l_ref, acc_ref, *, topk):
    TQ, TK = DSA_TQ, DSA_TK
    i = pl.program_id(1)
    n_kt = (i * TQ + TQ + TK - 1) // TK
    col = lax.broadcasted_iota(I32, (TQ, TK), 1)
    row = lax.broadcasted_iota(I32, (TQ, 1), 0)
    limit = ((i * TQ + row) // CHUNK + 1) * CHUNK
    wi = wi_ref[0]

    def tile_start(kt):
        return pl.multiple_of(kt * TK, TK)

    def score_tile(kt, carry):
        ks = tile_start(kt)
        kk = kiT_ref[0, :, pl.ds(ks, TK)]
        acc = jnp.zeros((TQ, TK), F32)
        for h in range(IDX_HEADS):
            logit = jnp.dot(qi_ref[0, h], kk, preferred_element_type=F32)
            acc = acc + wi[:, h:h + 1] * jnp.maximum(logit, 0.0)
        sc_ref[:, pl.ds(ks, TK)] = jnp.where(ks + col < limit, acc, NEG)
        return carry

    lax.fori_loop(0, n_kt, score_tile, 0)

    def count(pred):
        def body(kt, cnt):
            m = jnp.where(pred(sc_ref[:, pl.ds(tile_start(kt), TK)]), 1.0, 0.0)
            for c in range(TK // LANES):
                cnt = cnt + m[:, c * LANES:(c + 1) * LANES]
            return cnt
        cnt = lax.fori_loop(0, n_kt, body, jnp.zeros((TQ, LANES), F32))
        return jnp.sum(cnt, axis=1, keepdims=True)

    def key_to_float(key):
        bits = jnp.where(key < 0, jnp.int32(INT_MIN) - key, key)
        return lax.bitcast_convert_type(bits, F32)

    def bit_body(it, lo):
        cand = lo + jnp.left_shift(jnp.int32(1), 31 - it)
        cand_f = key_to_float(cand)
        c = count(lambda s: s >= cand_f)
        return jnp.where(c >= float(topk), cand, lo)

    lo = lax.fori_loop(0, 32, bit_body, jnp.full((TQ, 1), INT_MIN, I32))
    thr = key_to_float(lo)
    need = float(topk) - count(lambda s: s > thr)

    ri = lax.broadcasted_iota(I32, (TK, TK), 0)
    ci = lax.broadcasted_iota(I32, (TK, TK), 1)
    tri = jnp.where(ri < ci, 1.0, 0.0).astype(BF16)

    def bias_tile(kt, run):
        ks = tile_start(kt)
        s = sc_ref[:, pl.ds(ks, TK)]
        eq = jnp.where(s == thr, 1.0, 0.0)
        before = jnp.dot(eq.astype(BF16), tri, preferred_element_type=F32) + run
        sel = (s > thr) | ((s == thr) & (before < need))
        sel = sel & (ks + col < limit)
        bias_ref[:, pl.ds(ks, TK)] = jnp.where(sel, 0.0, NEG)
        return run + jnp.sum(eq, axis=1, keepdims=True)

    lax.fori_loop(0, n_kt, bias_tile, jnp.zeros((TQ, 1), F32))

    m_ref[...] = jnp.full(m_ref.shape, -3e38, F32)
    l_ref[...] = jnp.zeros(l_ref.shape, F32)
    acc_ref[...] = jnp.zeros(acc_ref.shape, F32)

    def att_tile(kt, carry):
        ks = tile_start(kt)
        bias = bias_ref[:, pl.ds(ks, TK)]
        vt = v_ref[0, pl.ds(ks, TK), :]
        for h in range(ATT_HEADS):
            g = h // ATT_REP
            kt_g = kT_ref[0, g * ATT_HEAD_DIM:(g + 1) * ATT_HEAD_DIM, pl.ds(ks, TK)]
            s = jnp.dot(q_ref[0, h], kt_g, preferred_element_type=F32) + bias
            m_old = m_ref[h]
            m_new = jnp.maximum(m_old, jnp.max(s, axis=1, keepdims=True))
            alpha = jnp.exp(m_old - m_new)
            p = jnp.exp(s - m_new)
            l_ref[h] = alpha * l_ref[h] + jnp.sum(p, axis=1, keepdims=True)
            acc_ref[h] = alpha * acc_ref[h] + jnp.dot(p.astype(BF16), vt, preferred_element_type=F32)
            m_ref[h] = m_new
        return carry

    lax.fori_loop(0, n_kt, att_tile, 0)

    outs = []
    for h in range(ATT_HEADS):
        g = h // ATT_REP
        outs.append(acc_ref[h][:, g * ATT_HEAD_DIM:(g + 1) * ATT_HEAD_DIM] / l_ref[h])
    o_ref[0] = jnp.concatenate(outs, axis=1).astype(BF16)


def _dsa(q, qi, wi, kiT, kT, v):
    B, _, S, _ = q.shape
    TQ = DSA_TQ
    topk = min(TOPK_MAX, S // 4)
    s_pad = pl.cdiv(S, DSA_TK) * DSA_TK
    return pl.pallas_call(
        functools.partial(_dsa_kernel, topk=topk),
        out_shape=jax.ShapeDtypeStruct((B, S, ATT_HEADS * ATT_HEAD_DIM), BF16),
        grid=(B, S // TQ),
        in_specs=[
            pl.BlockSpec((1, ATT_HEADS, TQ, ATT_HEAD_DIM), lambda b, i: (b, 0, i, 0)),
            pl.BlockSpec((1, IDX_HEADS, TQ, IDX_DIM), lambda b, i: (b, 0, i, 0)),
            pl.BlockSpec((1, TQ, IDX_HEADS), lambda b, i: (b, i, 0)),
            pl.BlockSpec((1, IDX_DIM, S), lambda b, i: (b, 0, 0)),
            pl.BlockSpec((1, LANES, S), lambda b, i: (b, 0, 0)),
            pl.BlockSpec((1, S, LANES), lambda b, i: (b, 0, 0)),
        ],
        out_specs=pl.BlockSpec((1, TQ, ATT_HEADS * ATT_HEAD_DIM), lambda b, i: (b, i, 0)),
        scratch_shapes=[
            pltpu.VMEM((TQ, s_pad), F32),
            pltpu.VMEM((TQ, s_pad), F32),
            pltpu.VMEM((ATT_HEADS, TQ, 1), F32),
            pltpu.VMEM((ATT_HEADS, TQ, 1), F32),
            pltpu.VMEM((ATT_HEADS, TQ, LANES), F32),
        ],
        compiler_params=_params("parallel", "parallel"),
        name="dsa",
    )(q, qi, wi, kiT, kT, v)


S5_SLABS = S5_WIDTH // LANES
S5_SLAB_STATES = (LANES // S5_GROUP) * S5_STATE


def _s5_param_kernel(lre_ref, lim_ref, ldt_ref, bre_ref, bim_ref,
                     lbre_ref, lbim_ref, bbre_ref, bbim_ref):
    lre = jnp.minimum(lre_ref[...], -1e-4)
    lim = lim_ref[...]
    dt = jnp.exp(ldt_ref[...])
    mag = jnp.exp(lre * dt)
    lbre = mag * jnp.cos(lim * dt)
    lbim = mag * jnp.sin(lim * dt)
    nre = lbre - 1.0
    den = lre * lre + lim * lim
    cre = (nre * lre + lbim * lim) / den
    cim = (lbim * lre - nre * lim) / den
    lbre_ref[...] = lbre
    lbim_ref[...] = lbim
    bbre_ref[...] = cre * bre_ref[...] - cim * bim_ref[...]
    bbim_ref[...] = cre * bim_ref[...] + cim * bre_ref[...]


def _s5_params(lam_re, lam_im, log_dt, b_re, b_im, c_re, c_im):
    G, P, H = S5_GROUPS, S5_STATE, S5_GROUP
    n = G * P
    flat = lambda a: a.astype(F32).reshape(1, n)
    ldt = jnp.repeat(log_dt.astype(F32), P).reshape(1, n)
    bt = lambda a: a.astype(F32).transpose(2, 0, 1).reshape(H, n)
    lbre, lbim, bbre, bbim = pl.pallas_call(
        _s5_param_kernel,
        out_shape=(jax.ShapeDtypeStruct((1, n), F32), jax.ShapeDtypeStruct((1, n), F32),
                   jax.ShapeDtypeStruct((H, n), F32), jax.ShapeDtypeStruct((H, n), F32)),
        name="s5_params",
    )(flat(lam_re), flat(lam_im), ldt, bt(b_re), bt(b_im))

    gl = LANES // H
    eye = jnp.eye(gl, dtype=F32)
    bb = jnp.stack([bbre, bbim]).reshape(2, H, S5_SLABS, gl, P)
    wb = jnp.einsum("ahjgp,gk->jghakp", bb, eye).reshape(S5_SLABS, LANES, 2 * gl * P)
    cc = jnp.stack([c_re.astype(F32), -c_im.astype(F32)]).reshape(2, S5_SLABS, gl, H, P)
    wc = jnp.einsum("ajghp,gk->jagpkh", cc, eye).reshape(S5_SLABS, 2 * gl * P, LANES)
    lam = lambda a: jnp.broadcast_to(a.reshape(S5_SLABS, 1, gl * P), (S5_SLABS, SUBLANES, gl * P))
    return lam(lbre), lam(lbim), wb.astype(BF16), wc.astype(BF16)


def _s5_kernel(u_ref, lbre_ref, lbim_ref, wb_ref, wc_ref, d_ref, wglu_ref, bglu_ref, o_ref,
               state_ref, h_ref, y_ref, *, steps, batch):
    NS = S5_SLAB_STATES
    t = pl.program_id(0)

    @pl.when(t == 0)
    def _():
        state_ref[...] = jnp.zeros(state_ref.shape, F32)

    u = u_ref[...]
    ub = u.astype(BF16)
    for j in range(S5_SLABS):
        h_ref[j] = jnp.dot(ub[:, j * LANES:(j + 1) * LANES], wb_ref[j], preferred_element_type=F32)

    for j in range(S5_SLABS):
        lre = lbre_ref[j]
        lim = lbim_ref[j]

        def step(s, carry):
            hre, him = carry
            r0 = pl.multiple_of(s * batch, batch)
            nre = lre * hre - lim * him + h_ref[j, pl.ds(r0, batch), :NS]
            nim = lre * him + lim * hre + h_ref[j, pl.ds(r0, batch), NS:]
            h_ref[j, pl.ds(r0, batch), :NS] = nre
            h_ref[j, pl.ds(r0, batch), NS:] = nim
            return nre, nim

        hre, him = lax.fori_loop(0, steps, step, (state_ref[j, :, :NS], state_ref[j, :, NS:]),
                                 unroll=4)
        state_ref[j, :, :NS] = hre
        state_ref[j, :, NS:] = him

    ys = [jnp.dot(h_ref[j].astype(BF16), wc_ref[j], preferred_element_type=F32)
          for j in range(S5_SLABS)]
    y = jnp.concatenate(ys, axis=1) + d_ref[...] * u
    y = jax.nn.gelu(y)
    gate = jnp.dot(y.astype(BF16), wglu_ref[...], preferred_element_type=F32) + bglu_ref[...]
    y = y * jax.nn.sigmoid(gate)
    for j in range(S5_SLABS):
        y_ref[j] = y[:, j * LANES:(j + 1) * LANES]
    for b in range(batch):
        for j in range(S5_SLABS):
            o_ref[b, :, j * LANES:(j + 1) * LANES] = (
                y_ref[j, pl.ds(b, steps, stride=batch), :].astype(BF16))


def _s5(u2, lbre, lbim, wb, wc, d_skip, w_glu, b_glu, batch, steps):
    rows, W = u2.shape
    S = rows // batch
    NS2 = 2 * S5_SLAB_STATES
    const = lambda *shape: pl.BlockSpec(shape, lambda t: (0,) * len(shape))
    return pl.pallas_call(
        functools.partial(_s5_kernel, steps=steps, batch=batch),
        out_shape=jax.ShapeDtypeStruct((batch, S, W), BF16),
        grid=(S // steps,),
        in_specs=[
            pl.BlockSpec((steps * batch, W), lambda t: (t, 0)),
            const(S5_SLABS, SUBLANES, S5_SLAB_STATES),
            const(S5_SLABS, SUBLANES, S5_SLAB_STATES),
            const(S5_SLABS, LANES, NS2),
            const(S5_SLABS, NS2, LANES),
            const(1, W),
            const(W, W),
            const(1, W),
        ],
        out_specs=pl.BlockSpec((batch, steps, W), lambda t: (0, t, 0)),
        scratch_shapes=[
            pltpu.VMEM((S5_SLABS, batch, NS2), F32),
            pltpu.VMEM((S5_SLABS, steps * batch, NS2), F32),
            pltpu.VMEM((S5_SLABS, steps * batch, LANES), F32),
        ],
        compiler_params=_params("arbitrary"),
        name="s5",
    )(u2, lbre, lbim, wb, wc, d_skip.astype(F32).reshape(1, W), w_glu.astype(BF16),
      b_glu.astype(F32).reshape(1, W))


def _layer_norm(z, g, b):
    mu = jnp.mean(z, axis=-1, keepdims=True)
    zc = z - mu
    var = jnp.mean(zc * zc, axis=-1, keepdims=True)
    return zc * lax.rsqrt(var + LN_EPS) * g + b


def _merge_kernel(x_ref, a_ref, yb_ref, qm_ref, mem_ref, wkv_ref, wg_ref, bg_ref,
                  wpa_ref, wpb_ref, wpc_ref, wout_ref, g_ref, b_ref, o_ref, kv_ref, *, alpha):
    D = x_ref.shape[2]
    HW = MEM_HEADS * MEM_HEAD_DIM

    @pl.when(pl.program_id(1) == 0)
    def _():
        kv_ref[...] = jnp.dot(mem_ref[0].astype(BF16), wkv_ref[...],
                              preferred_element_type=F32).astype(BF16)

    x = x_ref[0]
    xb = x.astype(BF16)
    qm = qm_ref[0]
    heads = []
    for h in range(MEM_HEADS):
        sl = slice(h * MEM_HEAD_DIM, (h + 1) * MEM_HEAD_DIM)
        s = lax.dot_general(qm[:, sl], kv_ref[:, sl], (((1,), (1,)), ((), ())),
                            preferred_element_type=F32)
        p = jnp.exp(s - jnp.max(s, axis=1, keepdims=True))
        p = p / jnp.sum(p, axis=1, keepdims=True)
        heads.append(jnp.dot(p.astype(BF16), kv_ref[:, HW + h * MEM_HEAD_DIM:HW + (h + 1) * MEM_HEAD_DIM],
                             preferred_element_type=F32))
    c_in = jnp.concatenate(heads, axis=1).astype(BF16)

    y_a = jnp.dot(a_ref[0], wpa_ref[...], preferred_element_type=F32)
    y_b = jnp.dot(yb_ref[0], wpb_ref[...], preferred_element_type=F32)
    y_c = jnp.dot(c_in, wpc_ref[...], preferred_element_type=F32)
    merged = None
    for k, yk in enumerate((y_a, y_b, y_c)):
        gk = jax.nn.sigmoid(jnp.dot(xb, wg_ref[:, k * D:(k + 1) * D], preferred_element_type=F32)
                            + bg_ref[:, k * D:(k + 1) * D])
        merged = gk * yk if merged is None else merged + gk * yk
    mix = jnp.dot(merged.astype(BF16), wout_ref[...], preferred_element_type=F32)
    o_ref[0] = _layer_norm(alpha * x + mix, g_ref[...], b_ref[...])


def _merge(x, att, yb, qm, mem, w_mem_kv, w_gate, b_gate, w_proj_a, w_proj_b, w_proj_c, w_out,
           ln_g, ln_b, alpha, tile):
    B, S, D = x.shape
    n_mem = mem.shape[1]
    HW = MEM_HEADS * MEM_HEAD_DIM
    const = lambda *shape: pl.BlockSpec(shape, lambda b, t: (0,) * len(shape))
    row = lambda width: pl.BlockSpec((1, tile, width), lambda b, t: (b, t, 0))
    return pl.pallas_call(
        functools.partial(_merge_kernel, alpha=alpha),
        out_shape=jax.ShapeDtypeStruct((B, S, D), F32),
        grid=(B, S // tile),
        in_specs=[
            row(D), row(att.shape[2]), row(yb.shape[2]), row(HW),
            pl.BlockSpec((1, n_mem, D), lambda b, t: (b, 0, 0)),
            const(D, 2 * HW), const(D, N_BRANCH * D), const(1, N_BRANCH * D),
            const(att.shape[2], D), const(yb.shape[2], D), const(HW, D), const(D, D),
            const(1, D), const(1, D),
        ],
        out_specs=row(D),
        scratch_shapes=[pltpu.VMEM((n_mem, 2 * HW), BF16)],
        compiler_params=_params("parallel", "arbitrary"),
        name="merge",
    )(x, att, yb, qm, mem, w_mem_kv.astype(BF16), w_gate.astype(BF16),
      b_gate.astype(F32).reshape(1, -1), w_proj_a.astype(BF16), w_proj_b.astype(BF16),
      w_proj_c.astype(BF16), w_out.astype(BF16), ln_g.astype(F32).reshape(1, D),
      ln_b.astype(F32).reshape(1, D))


FFN_CHUNK = 256


def _ffn_kernel(h_ref, wup_ref, cw_ref, cb_ref, wdn_ref, g_ref, b_ref, o_ref,
                buf_ref, tail_ref, *, alpha, ffn_dim):
    T = h_ref.shape[1]
    FC = FFN_CHUNK
    HALO = SUBLANES

    @pl.when(pl.program_id(1) == 0)
    def _():
        tail_ref[...] = jnp.zeros(tail_ref.shape, F32)

    h = h_ref[0]
    hb = h.astype(BF16)

    def conv_cols(c0):
        up = jnp.dot(hb, wup_ref[:, c0:c0 + FC], preferred_element_type=F32)
        buf_ref[:HALO] = tail_ref[:, c0:c0 + FC]
        buf_ref[HALO:] = up
        tail_ref[:, c0:c0 + FC] = up[T - HALO:]
        out = cb_ref[:, c0:c0 + FC]
        for j in range(CONV_WIDTH):
            off = HALO - (CONV_WIDTH - 1) + j
            out = out + cw_ref[j:j + 1, c0:c0 + FC] * buf_ref[off:off + T]
        return out

    f = jnp.zeros((T, h.shape[1]), F32)
    for c in range(ffn_dim // FC):
        gate = conv_cols(c * FC)
        up = conv_cols(ffn_dim + c * FC)
        act = (gate * jax.nn.sigmoid(gate) * up).astype(BF16)
        f = f + jnp.dot(act, wdn_ref[c * FC:(c + 1) * FC, :], preferred_element_type=F32)
    o_ref[0] = _layer_norm(alpha * h + f, g_ref[...], b_ref[...])


def _ffn(h, w_up, conv_w, conv_b, w_down, ln_g, ln_b, alpha, tile):
    B, S, D = h.shape
    F2 = w_up.shape[1]
    const = lambda *shape: pl.BlockSpec(shape, lambda b, t: (0,) * len(shape))
    row = pl.BlockSpec((1, tile, D), lambda b, t: (b, t, 0))
    return pl.pallas_call(
        functools.partial(_ffn_kernel, alpha=alpha, ffn_dim=F2 // 2),
        out_shape=jax.ShapeDtypeStruct((B, S, D), F32),
        grid=(B, S // tile),
        in_specs=[row, const(D, F2), const(CONV_WIDTH, F2), const(1, F2), const(F2 // 2, D),
                  const(1, D), const(1, D)],
        out_specs=row,
        scratch_shapes=[
            pltpu.VMEM((SUBLANES + tile, FFN_CHUNK), F32),
            pltpu.VMEM((SUBLANES, F2), F32),
        ],
        compiler_params=_params("parallel", "arbitrary"),
        name="ffn",
    )(h, w_up.astype(BF16), conv_w.astype(F32), conv_b.astype(F32).reshape(1, F2),
      w_down.astype(BF16), ln_g.astype(F32).reshape(1, D), ln_b.astype(F32).reshape(1, D))


PROJ_TILE = 512
S5_STEPS = 64
MERGE_TILE = 512
FFN_TILE = 512


def kernel(x, mem, positions, w_in, w_gate, b_gate, s5_lam_re, s5_lam_im, s5_log_dt, s5_b_re, s5_b_im, s5_c_re, s5_c_im, s5_d, w_glu, b_glu, w_mem_kv, w_proj_a, w_proj_b, w_proj_c, w_out, ln1_g, ln1_b, w_up, conv_w, conv_b, w_down, ln2_g, ln2_b):
    B, S, D = x.shape
    depth = w_in.shape[0]
    alpha = (2.0 * depth) ** 0.25
    h = x
    for l in range(depth):
        q, kT, v, qi, kiT, wi, u, qm = _proj(h, positions, w_in[l], min(PROJ_TILE, S))
        att = _dsa(q, qi, wi, kiT, kT, v)
        lbre, lbim, wb, wc = _s5_params(s5_lam_re[l], s5_lam_im[l], s5_log_dt[l], s5_b_re[l],
                                        s5_b_im[l], s5_c_re[l], s5_c_im[l])
        yb = _s5(u.reshape(S * B, S5_WIDTH), lbre, lbim, wb, wc, s5_d[l], w_glu[l], b_glu[l],
                 B, min(S5_STEPS, S))
        h = _merge(h, att, yb, qm, mem, w_mem_kv[l], w_gate[l], b_gate[l], w_proj_a[l],
                   w_proj_b[l], w_proj_c[l], w_out[l], ln1_g[l], ln1_b[l], alpha, min(MERGE_TILE, S))
        h = _ffn(h, w_up[l], conv_w[l], conv_b[l], w_down[l], ln2_g[l], ln2_b[l], alpha,
                 min(FFN_TILE, S))
    return h
```

```python
import functools
import math

import jax
import jax.numpy as jnp
from jax import lax
from jax.experimental import pallas as pl
from jax.experimental.pallas import tpu as pltpu

F32 = jnp.float32
BF16 = jnp.bfloat16
I32 = jnp.int32

CHUNK = 64
ATT_HEADS = 8
ATT_KV_HEADS = 2
ATT_HEAD_DIM = 64
ATT_REP = ATT_HEADS // ATT_KV_HEADS
IDX_HEADS = 4
IDX_DIM = 64
TOPK_MAX = 256
ROPE_THETA = 10000.0
S5_WIDTH = 512
S5_GROUP = 16
S5_GROUPS = S5_WIDTH // S5_GROUP
S5_STATE = 64
MEM_HEADS = 4
MEM_HEAD_DIM = 128
CONV_WIDTH = 3
N_BRANCH = 3
LN_EPS = 1e-5
NEG = -1e30
INT_MIN = -(2 ** 31)
LOG2E = math.log2(math.e)

LANES = 128
SUBLANES = 8
VMEM_LIMIT = 56 * 1024 * 1024

_Q0 = 0
_K0 = _Q0 + ATT_HEADS * ATT_HEAD_DIM
_V0 = _K0 + ATT_KV_HEADS * ATT_HEAD_DIM
_QI0 = _V0 + ATT_KV_HEADS * ATT_HEAD_DIM
_KI0 = _QI0 + IDX_HEADS * IDX_DIM
_WI0 = _KI0 + IDX_DIM
_U0 = _WI0 + IDX_HEADS
_QM0 = _U0 + S5_WIDTH
_END = _QM0 + MEM_HEADS * MEM_HEAD_DIM
_PAD = LANES - IDX_DIM - IDX_HEADS
_PU0 = _KI0 + LANES
_PQM0 = _PU0 + S5_WIDTH
_PEND = _PQM0 + MEM_HEADS * MEM_HEAD_DIM


def _params(*sem):
    return pltpu.CompilerParams(dimension_semantics=sem, vmem_limit_bytes=VMEM_LIMIT)


def _proj_kernel(x_ref, pos_ref, invf_ref, w_ref,
                 qT_ref, k_ref, vT_ref, qiT_ref, ki_ref, wiT_ref, u_ref, qm_ref):
    T = x_ref.shape[1]
    y = jnp.dot(x_ref[0].astype(BF16), w_ref[...], preferred_element_type=F32)

    ang = pos_ref[0].astype(F32) * invf_ref[...]
    cos = jnp.cos(ang)
    sin = jnp.sin(ang)
    lane = lax.broadcasted_iota(I32, (T, LANES), 1)
    first = (lane % ATT_HEAD_DIM) < (ATT_HEAD_DIM // 2)
    sin = jnp.where(first, -sin, sin)

    def rope(z):
        partner = jnp.where(first, pltpu.roll(z, LANES - 32, 1), pltpu.roll(z, 32, 1))
        return z * cos + partner * sin

    att_scale = ATT_HEAD_DIM ** -0.5 * LOG2E
    for c in range(ATT_HEADS // 2):
        z = rope(y[:, _Q0 + c * LANES:_Q0 + (c + 1) * LANES]) * att_scale
        qT_ref[0, c * LANES:(c + 1) * LANES, :] = z.T.astype(BF16)
    k_ref[0] = rope(y[:, _K0:_K0 + LANES]).astype(BF16)
    vT_ref[0] = y[:, _V0:_V0 + LANES].T.astype(BF16)
    idx_scale = IDX_DIM ** -0.5
    for c in range(IDX_HEADS // 2):
        z = rope(y[:, _QI0 + c * LANES:_QI0 + (c + 1) * LANES]) * idx_scale
        qiT_ref[0, c * LANES:(c + 1) * LANES, :] = z.T.astype(BF16)
    kw = y[:, _KI0:_KI0 + LANES]
    ki_ref[0] = jnp.where(lane < IDX_DIM, rope(kw), 0.0).astype(BF16)
    wiT_ref[0] = kw.T[IDX_DIM:IDX_DIM + IDX_HEADS] * (IDX_HEADS ** -0.5)
    u_ref[...] = y[:, _PU0:_PU0 + S5_WIDTH]
    qm_ref[0] = (y[:, _PQM0:_PEND] * (MEM_HEAD_DIM ** -0.5)).astype(BF16)


def _proj(x, positions, w_in, tile):
    B, S, D = x.shape
    nt = S // tile
    w_pad = jnp.concatenate(
        [w_in[:, :_U0], jnp.zeros((D, _PAD), w_in.dtype), w_in[:, _U0:]], axis=1).astype(BF16)
    half = ATT_HEAD_DIM // 2
    inv_freq = ROPE_THETA ** (-jnp.arange(half, dtype=F32) / half)
    invf = jnp.tile(inv_freq, LANES // half)[None, :]
    pos3 = positions.reshape(B, S, 1)
    QW = ATT_HEADS * ATT_HEAD_DIM
    IW = IDX_HEADS * IDX_DIM
    out_shape = (
        jax.ShapeDtypeStruct((B, QW, S), BF16),
        jax.ShapeDtypeStruct((B, S, LANES), BF16),
        jax.ShapeDtypeStruct((B, LANES, S), BF16),
        jax.ShapeDtypeStruct((B, IW, S), BF16),
        jax.ShapeDtypeStruct((B, S, LANES), BF16),
        jax.ShapeDtypeStruct((B, IDX_HEADS, S), F32),
        jax.ShapeDtypeStruct((S, B * S5_WIDTH), F32),
        jax.ShapeDtypeStruct((B, S, MEM_HEADS * MEM_HEAD_DIM), BF16),
    )
    rows = lambda width: pl.BlockSpec((1, tile, width), lambda b, t: (b, t, 0))
    cols = lambda height: pl.BlockSpec((1, height, tile), lambda b, t: (b, 0, t))
    out_specs = (
        cols(QW), rows(LANES), cols(LANES), cols(IW), rows(LANES), cols(IDX_HEADS),
        pl.BlockSpec((tile, S5_WIDTH), lambda b, t: (t, b)),
        rows(MEM_HEADS * MEM_HEAD_DIM),
    )
    return pl.pallas_call(
        _proj_kernel,
        out_shape=out_shape,
        grid=(B, nt),
        in_specs=[
            pl.BlockSpec((1, tile, D), lambda b, t: (b, t, 0)),
            pl.BlockSpec((1, tile, 1), lambda b, t: (b, t, 0)),
            pl.BlockSpec((1, LANES), lambda b, t: (0, 0)),
            pl.BlockSpec((D, _PEND), lambda b, t: (0, 0)),
        ],
        out_specs=out_specs,
        compiler_params=_params("parallel", "parallel"),
        name="proj",
    )(x, pos3, invf, w_pad)


DSA_TQ = 256
DSA_TK = 256
COUNT_ROWS = 4 * SUBLANES


def _dsa_kernel(qT_ref, qiT_ref, wiT_ref, ki_ref, k_ref, vT_ref, o_ref,
                wq_ref, wqi_ref, sc_ref, bias_ref, m_ref, l_ref, acc_ref, alpha_ref, s_ref, p_ref,
                *, topk):
    TQ, TK = DSA_TQ, DSA_TK
    HD = ATT_HEAD_DIM
    i = pl.program_id(1)
    n_kt = (i * TQ + TQ + TK - 1) // TK
    key_in_tile = lax.broadcasted_iota(I32, (TK, TQ), 0)
    qpos = i * TQ + lax.broadcasted_iota(I32, (1, TQ), 1)
    limit = (qpos // CHUNK + 1) * CHUNK
    wi = wiT_ref[0]

    def tile_start(kt):
        return pl.multiple_of(kt * TK, TK)

    zeros = jnp.zeros((HD, TQ), BF16)
    for h in range(ATT_HEADS):
        qh = qT_ref[0, h * HD:(h + 1) * HD, :]
        g = h // ATT_REP
        wq_ref[h] = jnp.concatenate([zeros] * g + [qh] + [zeros] * (ATT_KV_HEADS - 1 - g), axis=0)
    for h in range(IDX_HEADS):
        wqi_ref[h] = jnp.concatenate([qiT_ref[0, h * IDX_DIM:(h + 1) * IDX_DIM, :], zeros], axis=0)

    def score_tile(kt, carry):
        ks = tile_start(kt)
        kk = ki_ref[0, pl.ds(ks, TK), :]
        acc = jnp.zeros((TK, TQ), F32)
        for h in range(IDX_HEADS):
            logit = jnp.dot(kk, wqi_ref[h], preferred_element_type=F32)
            acc = acc + wi[h:h + 1, :] * jnp.maximum(logit, 0.0)
        sc_ref[pl.ds(ks, TK), :] = jnp.where(ks + key_in_tile < limit, acc, NEG)
        return carry

    lax.fori_loop(0, n_kt, score_tile, 0)

    def count(pred):
        def body(kt, cnt):
            m = jnp.where(pred(sc_ref[pl.ds(tile_start(kt), TK), :]), 1.0, 0.0)
            return cnt + m.reshape(TK // COUNT_ROWS, COUNT_ROWS, TQ).sum(axis=0)
        cnt = lax.fori_loop(0, n_kt, body, jnp.zeros((COUNT_ROWS, TQ), F32))
        return jnp.sum(cnt, axis=0, keepdims=True)

    def key_to_float(key):
        bits = jnp.where(key < 0, jnp.int32(INT_MIN) - key, key)
        return lax.bitcast_convert_type(bits, F32)

    def bit_body(it, carry):
        lo, cnt_lo = carry
        cand = lo + jnp.left_shift(jnp.int32(1), 31 - it)
        cand_f = key_to_float(cand)
        c = count(lambda s: s >= cand_f)
        ok = c >= float(topk)
        return jnp.where(ok, cand, lo), jnp.where(ok, c, cnt_lo)

    visited = (n_kt * TK).astype(F32)
    lo, cnt_lo = lax.fori_loop(0, 32, bit_body, (jnp.full((1, TQ), INT_MIN, I32),
                                                 jnp.full((1, TQ), visited, F32)))
    thr = jnp.where(lo == INT_MIN, -jnp.inf, key_to_float(lo))
    has_ties = jnp.max(cnt_lo) > float(topk)

    @pl.when(jnp.logical_not(has_ties))
    def _():
        def bias_tile(kt, carry):
            ks = tile_start(kt)
            s = sc_ref[pl.ds(ks, TK), :]
            sel = (s >= thr) & (ks + key_in_tile < limit)
            bias_ref[pl.ds(ks, TK), :] = jnp.where(sel, 0.0, NEG)
            return carry
        lax.fori_loop(0, n_kt, bias_tile, 0)

    @pl.when(has_ties)
    def _():
        need = float(topk) - count(lambda s: s > thr)
        ri = lax.broadcasted_iota(I32, (TK, TK), 0)
        ci = lax.broadcasted_iota(I32, (TK, TK), 1)
        tri = jnp.where(ci < ri, 1.0, 0.0).astype(BF16)

        def bias_tile(kt, run):
            ks = tile_start(kt)
            s = sc_ref[pl.ds(ks, TK), :]
            eq = jnp.where(s == thr, 1.0, 0.0)
            before = jnp.dot(tri, eq.astype(BF16), preferred_element_type=F32) + run
            sel = (s > thr) | ((s == thr) & (before < need))
            sel = sel & (ks + key_in_tile < limit)
            bias_ref[pl.ds(ks, TK), :] = jnp.where(sel, 0.0, NEG)
            return run + jnp.sum(eq, axis=0, keepdims=True)
        lax.fori_loop(0, n_kt, bias_tile, jnp.zeros((1, TQ), F32))

    m_ref[...] = jnp.full(m_ref.shape, -3e38, F32)
    l_ref[...] = jnp.zeros(l_ref.shape, F32)
    acc_ref[...] = jnp.zeros(acc_ref.shape, F32)

    def att_tile(kt, carry):
        ks = tile_start(kt)
        kk = k_ref[0, pl.ds(ks, TK), :]
        bias = bias_ref[pl.ds(ks, TK), :]
        for h in range(ATT_HEADS):
            s_ref[h] = jnp.dot(kk, wq_ref[h], preferred_element_type=F32) + bias
        for h in range(ATT_HEADS):
            s = s_ref[h]
            m_old = m_ref[h:h + 1, :]
            m_new = jnp.maximum(m_old, jnp.max(s, axis=0, keepdims=True))
            alpha = jnp.exp2(m_old - m_new)
            p = jnp.exp2(s - m_new)
            l_ref[h:h + 1, :] = alpha * l_ref[h:h + 1, :] + jnp.sum(p, axis=0, keepdims=True)
            m_ref[h:h + 1, :] = m_new
            alpha_ref[h:h + 1, :] = alpha
            p_ref[h] = p.astype(BF16)
        for h in range(ATT_HEADS):
            g = h // ATT_REP
            pv = jnp.dot(vT_ref[0, g * HD:(g + 1) * HD, pl.ds(ks, TK)], p_ref[h],
                         preferred_element_type=F32)
            acc_ref[h * HD:(h + 1) * HD, :] = (
                alpha_ref[h:h + 1, :] * acc_ref[h * HD:(h + 1) * HD, :] + pv)
        return carry

    lax.fori_loop(0, n_kt, att_tile, 0)

    for c in range(ATT_HEADS // 2):
        pair = [acc_ref[h * HD:(h + 1) * HD, :] / l_ref[h:h + 1, :] for h in (2 * c, 2 * c + 1)]
        o_ref[0, :, c * LANES:(c + 1) * LANES] = jnp.concatenate(pair, axis=0).T.astype(BF16)


def _dsa(qT, qiT, wiT, ki, k, vT):
    B, QW, S = qT.shape
    TQ = DSA_TQ
    topk = min(TOPK_MAX, S // 4)
    assert S % TQ == 0 and TQ == DSA_TK and DSA_TK >= topk
    return pl.pallas_call(
        functools.partial(_dsa_kernel, topk=topk),
        out_shape=jax.ShapeDtypeStruct((B, S, QW), BF16),
        grid=(B, S // TQ),
        in_specs=[
            pl.BlockSpec((1, QW, TQ), lambda b, i: (b, 0, i)),
            pl.BlockSpec((1, IDX_HEADS * IDX_DIM, TQ), lambda b, i: (b, 0, i)),
            pl.BlockSpec((1, IDX_HEADS, TQ), lambda b, i: (b, 0, i)),
            pl.BlockSpec((1, S, LANES), lambda b, i: (b, 0, 0)),
            pl.BlockSpec((1, S, LANES), lambda b, i: (b, 0, 0)),
            pl.BlockSpec((1, LANES, S), lambda b, i: (b, 0, 0)),
        ],
        out_specs=pl.BlockSpec((1, TQ, QW), lambda b, i: (b, i, 0)),
        scratch_shapes=[
            pltpu.VMEM((ATT_HEADS, LANES, TQ), BF16),
            pltpu.VMEM((IDX_HEADS, LANES, TQ), BF16),
            pltpu.VMEM((S, TQ), F32),
            pltpu.VMEM((S, TQ), F32),
            pltpu.VMEM((ATT_HEADS, TQ), F32),
            pltpu.VMEM((ATT_HEADS, TQ), F32),
            pltpu.VMEM((QW, TQ), F32),
            pltpu.VMEM((ATT_HEADS, TQ), F32),
            pltpu.VMEM((ATT_HEADS, DSA_TK, TQ), F32),
            pltpu.VMEM((ATT_HEADS, DSA_TK, TQ), BF16),
        ],
        compiler_params=_params("parallel", "parallel"),
        name="dsa",
    )(qT, qiT, wiT, ki, k, vT)


S5_SLABS = S5_WIDTH // LANES
S5_SLAB_STATES = (LANES // S5_GROUP) * S5_STATE


def _s5_param_kernel(lre_ref, lim_ref, ldt_ref, bre_ref, bim_ref,
                     lbre_ref, lbim_ref, bbre_ref, bbim_ref):
    lre = jnp.minimum(lre_ref[...], -1e-4)
    lim = lim_ref[...]
    dt = jnp.exp(ldt_ref[...])
    mag = jnp.exp(lre * dt)
    lbre = mag * jnp.cos(lim * dt)
    lbim = mag * jnp.sin(lim * dt)
    nre = lbre - 1.0
    den = lre * lre + lim * lim
    cre = (nre * lre + lbim * lim) / den
    cim = (lbim * lre - nre * lim) / den
    lbre_ref[...] = lbre
    lbim_ref[...] = lbim
    bbre_ref[...] = cre * bre_ref[...] - cim * bim_ref[...]
    bbim_ref[...] = cre * bim_ref[...] + cim * bre_ref[...]


def _s5_params(lam_re, lam_im, log_dt, b_re, b_im, c_re, c_im):
    G, P, H = S5_GROUPS, S5_STATE, S5_GROUP
    n = G * P
    flat = lambda a: a.astype(F32).reshape(1, n)
    ldt = jnp.repeat(log_dt.astype(F32), P).reshape(1, n)
    bt = lambda a: a.astype(F32).transpose(2, 0, 1).reshape(H, n)
    lbre, lbim, bbre, bbim = pl.pallas_call(
        _s5_param_kernel,
        out_shape=(jax.ShapeDtypeStruct((1, n), F32), jax.ShapeDtypeStruct((1, n), F32),
                   jax.ShapeDtypeStruct((H, n), F32), jax.ShapeDtypeStruct((H, n), F32)),
        name="s5_params",
    )(flat(lam_re), flat(lam_im), ldt, bt(b_re), bt(b_im))

    gl = LANES // H
    eye = jnp.eye(gl, dtype=F32)
    bb = jnp.stack([bbre, bbim]).reshape(2, H, S5_SLABS, gl, P)
    wb = jnp.einsum("ahjgp,gk->jghakp", bb, eye).reshape(S5_SLABS, LANES, 2 * gl * P)
    cc = jnp.stack([c_re.astype(F32), -c_im.astype(F32)]).reshape(2, S5_SLABS, gl, H, P)
    wc = jnp.einsum("ajghp,gk->jagpkh", cc, eye).reshape(S5_SLABS, 2 * gl * P, LANES)
    lam = lambda a: jnp.broadcast_to(a.reshape(S5_SLABS, 1, gl * P), (S5_SLABS, SUBLANES, gl * P))
    return lam(lbre), lam(lbim), wb.astype(BF16), wc.astype(BF16)


def _s5_kernel(u_ref, lbre_ref, lbim_ref, wb_ref, wc_ref, d_ref, wglu_ref, bglu_ref, o_ref,
               state_ref, h_ref, y_ref, *, steps, batch):
    NS = S5_SLAB_STATES
    t = pl.program_id(0)

    @pl.when(t == 0)
    def _():
        state_ref[...] = jnp.zeros(state_ref.shape, F32)

    u = u_ref[...]
    ub = u.astype(BF16)
    for j in range(S5_SLABS):
        h_ref[j] = jnp.dot(ub[:, j * LANES:(j + 1) * LANES], wb_ref[j], preferred_element_type=F32)

    for j in range(S5_SLABS):
        lre = lbre_ref[j]
        lim = lbim_ref[j]

        def step(s, carry):
            hre, him = carry
            r0 = pl.multiple_of(s * batch, batch)
            nre = lre * hre - lim * him + h_ref[j, pl.ds(r0, batch), :NS]
            nim = lre * him + lim * hre + h_ref[j, pl.ds(r0, batch), NS:]
            h_ref[j, pl.ds(r0, batch), :NS] = nre
            h_ref[j, pl.ds(r0, batch), NS:] = nim
            return nre, nim

        hre, him = lax.fori_loop(0, steps, step, (state_ref[j, :, :NS], state_ref[j, :, NS:]),
                                 unroll=4)
        state_ref[j, :, :NS] = hre
        state_ref[j, :, NS:] = him

    ys = [jnp.dot(h_ref[j].astype(BF16), wc_ref[j], preferred_element_type=F32)
          for j in range(S5_SLABS)]
    y = jnp.concatenate(ys, axis=1) + d_ref[...] * u
    y = jax.nn.gelu(y)
    gate = jnp.dot(y.astype(BF16), wglu_ref[...], preferred_element_type=F32) + bglu_ref[...]
    y = y * jax.nn.sigmoid(gate)
    for j in range(S5_SLABS):
        y_ref[j] = y[:, j * LANES:(j + 1) * LANES]
    for b in range(batch):
        for j in range(S5_SLABS):
            o_ref[b, :, j * LANES:(j + 1) * LANES] = (
                y_ref[j, pl.ds(b, steps, stride=batch), :].astype(BF16))


def _s5(u2, lbre, lbim, wb, wc, d_skip, w_glu, b_glu, batch, steps):
    rows, W = u2.shape
    S = rows // batch
    NS2 = 2 * S5_SLAB_STATES
    const = lambda *shape: pl.BlockSpec(shape, lambda t: (0,) * len(shape))
    return pl.pallas_call(
        functools.partial(_s5_kernel, steps=steps, batch=batch),
        out_shape=jax.ShapeDtypeStruct((batch, S, W), BF16),
        grid=(S // steps,),
        in_specs=[
            pl.BlockSpec((steps * batch, W), lambda t: (t, 0)),
            const(S5_SLABS, SUBLANES, S5_SLAB_STATES),
            const(S5_SLABS, SUBLANES, S5_SLAB_STATES),
            const(S5_SLABS, LANES, NS2),
            const(S5_SLABS, NS2, LANES),
            const(1, W),
            const(W, W),
            const(1, W),
        ],
        out_specs=pl.BlockSpec((batch, steps, W), lambda t: (0, t, 0)),
        scratch_shapes=[
            pltpu.VMEM((S5_SLABS, batch, NS2), F32),
            pltpu.VMEM((S5_SLABS, steps * batch, NS2), F32),
            pltpu.VMEM((S5_SLABS, steps * batch, LANES), F32),
        ],
        compiler_params=_params("arbitrary"),
        name="s5",
    )(u2, lbre, lbim, wb, wc, d_skip.astype(F32).reshape(1, W), w_glu.astype(BF16),
      b_glu.astype(F32).reshape(1, W))


def _layer_norm(z, g, b):
    mu = jnp.mean(z, axis=-1, keepdims=True)
    zc = z - mu
    var = jnp.mean(zc * zc, axis=-1, keepdims=True)
    return zc * lax.rsqrt(var + LN_EPS) * g + b


def _merge_kernel(x_ref, a_ref, yb_ref, qm_ref, mem_ref, wkv_ref, wg_ref, bg_ref,
                  wpa_ref, wpb_ref, wpc_ref, wout_ref, g_ref, b_ref, o_ref, kv_ref, *, alpha):
    D = x_ref.shape[2]
    HW = MEM_HEADS * MEM_HEAD_DIM

    @pl.when(pl.program_id(1) == 0)
    def _():
        kv_ref[...] = jnp.dot(mem_ref[0].astype(BF16), wkv_ref[...],
                              preferred_element_type=F32).astype(BF16)

    x = x_ref[0]
    xb = x.astype(BF16)
    qm = qm_ref[0]
    heads = []
    for h in range(MEM_HEADS):
        sl = slice(h * MEM_HEAD_DIM, (h + 1) * MEM_HEAD_DIM)
        s = lax.dot_general(qm[:, sl], kv_ref[:, sl], (((1,), (1,)), ((), ())),
                            preferred_element_type=F32)
        p = jnp.exp(s - jnp.max(s, axis=1, keepdims=True))
        p = p / jnp.sum(p, axis=1, keepdims=True)
        heads.append(jnp.dot(p.astype(BF16), kv_ref[:, HW + h * MEM_HEAD_DIM:HW + (h + 1) * MEM_HEAD_DIM],
                             preferred_element_type=F32))
    c_in = jnp.concatenate(heads, axis=1).astype(BF16)

    y_a = jnp.dot(a_ref[0], wpa_ref[...], preferred_element_type=F32)
    y_b = jnp.dot(yb_ref[0], wpb_ref[...], preferred_element_type=F32)
    y_c = jnp.dot(c_in, wpc_ref[...], preferred_element_type=F32)
    merged = None
    for k, yk in enumerate((y_a, y_b, y_c)):
        gk = jax.nn.sigmoid(jnp.dot(xb, wg_ref[:, k * D:(k + 1) * D], preferred_element_type=F32)
                            + bg_ref[:, k * D:(k + 1) * D])
        merged = gk * yk if merged is None else merged + gk * yk
    mix = jnp.dot(merged.astype(BF16), wout_ref[...], preferred_element_type=F32)
    o_ref[0] = _layer_norm(alpha * x + mix, g_ref[...], b_ref[...])


def _merge(x, att, yb, qm, mem, w_mem_kv, w_gate, b_gate, w_proj_a, w_proj_b, w_proj_c, w_out,
           ln_g, ln_b, alpha, tile):
    B, S, D = x.shape
    n_mem = mem.shape[1]
    HW = MEM_HEADS * MEM_HEAD_DIM
    const = lambda *shape: pl.BlockSpec(shape, lambda b, t: (0,) * len(shape))
    row = lambda width: pl.BlockSpec((1, tile, width), lambda b, t: (b, t, 0))
    return pl.pallas_call(
        functools.partial(_merge_kernel, alpha=alpha),
        out_shape=jax.ShapeDtypeStruct((B, S, D), F32),
        grid=(B, S // tile),
        in_specs=[
            row(D), row(att.shape[2]), row(yb.shape[2]), row(HW),
            pl.BlockSpec((1, n_mem, D), lambda b, t: (b, 0, 0)),
            const(D, 2 * HW), const(D, N_BRANCH * D), const(1, N_BRANCH * D),
            const(att.shape[2], D), const(yb.shape[2], D), const(HW, D), const(D, D),
            const(1, D), const(1, D),
        ],
        out_specs=row(D),
        scratch_shapes=[pltpu.VMEM((n_mem, 2 * HW), BF16)],
        compiler_params=_params("parallel", "arbitrary"),
        name="merge",
    )(x, att, yb, qm, mem, w_mem_kv.astype(BF16), w_gate.astype(BF16),
      b_gate.astype(F32).reshape(1, -1), w_proj_a.astype(BF16), w_proj_b.astype(BF16),
      w_proj_c.astype(BF16), w_out.astype(BF16), ln_g.astype(F32).reshape(1, D),
      ln_b.astype(F32).reshape(1, D))


FFN_CHUNK = 256


def _ffn_kernel(h_ref, wup_ref, cw_ref, cb_ref, wdn_ref, g_ref, b_ref, o_ref,
                buf_ref, tail_ref, *, alpha, ffn_dim):
    T = h_ref.shape[1]
    FC = FFN_CHUNK
    HALO = SUBLANES

    @pl.when(pl.program_id(1) == 0)
    def _():
        tail_ref[...] = jnp.zeros(tail_ref.shape, F32)

    h = h_ref[0]
    hb = h.astype(BF16)

    def conv_cols(c0):
        up = jnp.dot(hb, wup_ref[:, c0:c0 + FC], preferred_element_type=F32)
        buf_ref[:HALO] = tail_ref[:, c0:c0 + FC]
        buf_ref[HALO:] = up
        tail_ref[:, c0:c0 + FC] = up[T - HALO:]
        out = cb_ref[:, c0:c0 + FC]
        for j in range(CONV_WIDTH):
            off = HALO - (CONV_WIDTH - 1) + j
            out = out + cw_ref[j:j + 1, c0:c0 + FC] * buf_ref[off:off + T]
        return out

    f = jnp.zeros((T, h.shape[1]), F32)
    for c in range(ffn_dim // FC):
        gate = conv_cols(c * FC)
        up = conv_cols(ffn_dim + c * FC)
        act = (gate * jax.nn.sigmoid(gate) * up).astype(BF16)
        f = f + jnp.dot(act, wdn_ref[c * FC:(c + 1) * FC, :], preferred_element_type=F32)
    o_ref[0] = _layer_norm(alpha * h + f, g_ref[...], b_ref[...])


def _ffn(h, w_up, conv_w, conv_b, w_down, ln_g, ln_b, alpha, tile):
    B, S, D = h.shape
    F2 = w_up.shape[1]
    const = lambda *shape: pl.BlockSpec(shape, lambda b, t: (0,) * len(shape))
    row = pl.BlockSpec((1, tile, D), lambda b, t: (b, t, 0))
    return pl.pallas_call(
        functools.partial(_ffn_kernel, alpha=alpha, ffn_dim=F2 // 2),
        out_shape=jax.ShapeDtypeStruct((B, S, D), F32),
        grid=(B, S // tile),
        in_specs=[row, const(D, F2), const(CONV_WIDTH, F2), const(1, F2), const(F2 // 2, D),
                  const(1, D), const(1, D)],
        out_specs=row,
        scratch_shapes=[
            pltpu.VMEM((SUBLANES + tile, FFN_CHUNK), F32),
            pltpu.VMEM((SUBLANES, F2), F32),
        ],
        compiler_params=_params("parallel", "arbitrary"),
        name="ffn",
    )(h, w_up.astype(BF16), conv_w.astype(F32), conv_b.astype(F32).reshape(1, F2),
      w_down.astype(BF16), ln_g.astype(F32).reshape(1, D), ln_b.astype(F32).reshape(1, D))


PROJ_TILE = 512
S5_STEPS = 64
MERGE_TILE = 512
FFN_TILE = 512


def kernel(x, mem, positions, w_in, w_gate, b_gate, s5_lam_re, s5_lam_im, s5_log_dt, s5_b_re, s5_b_im, s5_c_re, s5_c_im, s5_d, w_glu, b_glu, w_mem_kv, w_proj_a, w_proj_b, w_proj_c, w_out, ln1_g, ln1_b, w_up, conv_w, conv_b, w_down, ln2_g, ln2_b):
    B, S, D = x.shape
    depth = w_in.shape[0]
    alpha = (2.0 * depth) ** 0.25
    h = x
    for l in range(depth):
        qT, k, vT, qiT, ki, wiT, u, qm = _proj(h, positions, w_in[l], min(PROJ_TILE, S))
        att = _dsa(qT, qiT, wiT, ki, k, vT)
        lbre, lbim, wb, wc = _s5_params(s5_lam_re[l], s5_lam_im[l], s5_log_dt[l], s5_b_re[l],
                                        s5_b_im[l], s5_c_re[l], s5_c_im[l])
        yb = _s5(u.reshape(S * B, S5_WIDTH), lbre, lbim, wb, wc, s5_d[l], w_glu[l], b_glu[l],
                 B, min(S5_STEPS, S))
        h = _merge(h, att, yb, qm, mem, w_mem_kv[l], w_gate[l], b_gate[l], w_proj_a[l],
                   w_proj_b[l], w_proj_c[l], w_out[l], ln1_g[l], ln1_b[l], alpha, min(MERGE_TILE, S))
        h = _ffn(h, w_up[l], conv_w[l], conv_b[l], w_down[l], ln2_g[l], ln2_b[l], alpha,
                 min(FFN_TILE, S))
    return h
```

```python
import functools
import math

import jax
import jax.numpy as jnp
from jax import lax
from jax.experimental import pallas as pl
from jax.experimental.pallas import tpu as pltpu

F32 = jnp.float32
BF16 = jnp.bfloat16
I32 = jnp.int32
I16 = jnp.int16

CHUNK = 64
ATT_HEADS = 8
ATT_KV_HEADS = 2
ATT_HEAD_DIM = 64
ATT_REP = ATT_HEADS // ATT_KV_HEADS
IDX_HEADS = 4
IDX_DIM = 64
TOPK_MAX = 256
ROPE_THETA = 10000.0
S5_WIDTH = 512
S5_GROUP = 16
S5_GROUPS = S5_WIDTH // S5_GROUP
S5_STATE = 64
MEM_HEADS = 4
MEM_HEAD_DIM = 128
CONV_WIDTH = 3
N_BRANCH = 3
LN_EPS = 1e-5
NEG = -1e30
INT_MIN = -(2 ** 31)
I16_MIN = -(2 ** 15)
I16_MAX = 2 ** 15 - 1
LOG2E = math.log2(math.e)

LANES = 128
SUBLANES = 8
VMEM_LIMIT = 56 * 1024 * 1024

_Q0 = 0
_K0 = _Q0 + ATT_HEADS * ATT_HEAD_DIM
_V0 = _K0 + ATT_KV_HEADS * ATT_HEAD_DIM
_QI0 = _V0 + ATT_KV_HEADS * ATT_HEAD_DIM
_KI0 = _QI0 + IDX_HEADS * IDX_DIM
_WI0 = _KI0 + IDX_DIM
_U0 = _WI0 + IDX_HEADS
_QM0 = _U0 + S5_WIDTH
_END = _QM0 + MEM_HEADS * MEM_HEAD_DIM
_PAD = LANES - IDX_DIM - IDX_HEADS
_PU0 = _KI0 + LANES
_PQM0 = _PU0 + S5_WIDTH
_PEND = _PQM0 + MEM_HEADS * MEM_HEAD_DIM
V_ROWS = ATT_HEAD_DIM + 16


def _params(*sem):
    return pltpu.CompilerParams(dimension_semantics=sem, vmem_limit_bytes=VMEM_LIMIT)


def _proj_kernel(x_ref, pos_ref, invf_ref, w_ref,
                 qT_ref, k_ref, vT_ref, qiT_ref, ki_ref, wiT_ref, u_ref, qm_ref):
    T = x_ref.shape[1]
    y = jnp.dot(x_ref[0].astype(BF16), w_ref[...], preferred_element_type=F32)

    ang = pos_ref[0].astype(F32) * invf_ref[...]
    cos = jnp.cos(ang)
    sin = jnp.sin(ang)
    lane = lax.broadcasted_iota(I32, (T, LANES), 1)
    first = (lane % ATT_HEAD_DIM) < (ATT_HEAD_DIM // 2)
    sin = jnp.where(first, -sin, sin)

    def rope(z):
        partner = jnp.where(first, pltpu.roll(z, LANES - 32, 1), pltpu.roll(z, 32, 1))
        return z * cos + partner * sin

    att_scale = ATT_HEAD_DIM ** -0.5 * LOG2E
    for c in range(ATT_HEADS // 2):
        z = rope(y[:, _Q0 + c * LANES:_Q0 + (c + 1) * LANES]) * att_scale
        qT_ref[0, c * LANES:(c + 1) * LANES, :] = z.T.astype(BF16)
    k_ref[0] = rope(y[:, _K0:_K0 + LANES]).astype(BF16)
    vT = y[:, _V0:_V0 + LANES].T.astype(BF16)
    for g in range(ATT_KV_HEADS):
        vT_ref[0, g * V_ROWS:g * V_ROWS + ATT_HEAD_DIM, :] = vT[g * ATT_HEAD_DIM:(g + 1) * ATT_HEAD_DIM]
        vT_ref[0, g * V_ROWS + ATT_HEAD_DIM:(g + 1) * V_ROWS, :] = jnp.ones((V_ROWS - ATT_HEAD_DIM, T), BF16)
    idx_scale = IDX_DIM ** -0.5
    for c in range(IDX_HEADS // 2):
        z = rope(y[:, _QI0 + c * LANES:_QI0 + (c + 1) * LANES]) * idx_scale
        qiT_ref[0, c * LANES:(c + 1) * LANES, :] = z.T.astype(BF16)
    kw = y[:, _KI0:_KI0 + LANES]
    ki_ref[0] = jnp.where(lane < IDX_DIM, rope(kw), 0.0).astype(BF16)
    wiT_ref[0] = kw.T[IDX_DIM:IDX_DIM + IDX_HEADS] * (IDX_HEADS ** -0.5)
    u_ref[...] = y[:, _PU0:_PU0 + S5_WIDTH]
    qm_ref[0] = (y[:, _PQM0:_PEND] * (MEM_HEAD_DIM ** -0.5)).astype(BF16)


def _proj(x, positions, w_in, tile):
    B, S, D = x.shape
    nt = S // tile
    w_pad = jnp.concatenate(
        [w_in[:, :_U0], jnp.zeros((D, _PAD), w_in.dtype), w_in[:, _U0:]], axis=1).astype(BF16)
    half = ATT_HEAD_DIM // 2
    inv_freq = ROPE_THETA ** (-jnp.arange(half, dtype=F32) / half)
    invf = jnp.tile(inv_freq, LANES // half)[None, :]
    pos3 = positions.reshape(B, S, 1)
    QW = ATT_HEADS * ATT_HEAD_DIM
    IW = IDX_HEADS * IDX_DIM
    out_shape = (
        jax.ShapeDtypeStruct((B, QW, S), BF16),
        jax.ShapeDtypeStruct((B, S, LANES), BF16),
        jax.ShapeDtypeStruct((B, ATT_KV_HEADS * V_ROWS, S), BF16),
        jax.ShapeDtypeStruct((B, IW, S), BF16),
        jax.ShapeDtypeStruct((B, S, LANES), BF16),
        jax.ShapeDtypeStruct((B, IDX_HEADS, S), F32),
        jax.ShapeDtypeStruct((S, B * S5_WIDTH), F32),
        jax.ShapeDtypeStruct((B, S, MEM_HEADS * MEM_HEAD_DIM), BF16),
    )
    rows = lambda width: pl.BlockSpec((1, tile, width), lambda b, t: (b, t, 0))
    cols = lambda height: pl.BlockSpec((1, height, tile), lambda b, t: (b, 0, t))
    out_specs = (
        cols(QW), rows(LANES), cols(ATT_KV_HEADS * V_ROWS), cols(IW), rows(LANES), cols(IDX_HEADS),
        pl.BlockSpec((tile, S5_WIDTH), lambda b, t: (t, b)),
        rows(MEM_HEADS * MEM_HEAD_DIM),
    )
    return pl.pallas_call(
        _proj_kernel,
        out_shape=out_shape,
        grid=(B, nt),
        in_specs=[
            pl.BlockSpec((1, tile, D), lambda b, t: (b, t, 0)),
            pl.BlockSpec((1, tile, 1), lambda b, t: (b, t, 0)),
            pl.BlockSpec((1, LANES), lambda b, t: (0, 0)),
            pl.BlockSpec((D, _PEND), lambda b, t: (0, 0)),
        ],
        out_specs=out_specs,
        compiler_params=_params("parallel", "parallel"),
        name="proj",
    )(x, pos3, invf, w_pad)


DSA_TQ = 256
DSA_TK = 256
COUNT_ROWS = 64
FCOUNT_ROWS = 4 * SUBLANES


def _dsa_kernel(qT_ref, qiT_ref, wiT_ref, ki_ref, k_ref, vT_ref, o_ref,
                wq_ref, wqi_ref, sc_ref, hi_ref, lo_ref, bias_ref, m_ref, acc_ref, alpha_ref,
                s_ref, p_ref, thr_ref, cnt_ref,
                *, topk):
    TQ, TK = DSA_TQ, DSA_TK
    HD = ATT_HEAD_DIM
    i = pl.program_id(1)
    n_kt = (i * TQ + TQ + TK - 1) // TK
    key_in_tile = lax.broadcasted_iota(I32, (TK, TQ), 0)
    qpos = i * TQ + lax.broadcasted_iota(I32, (1, TQ), 1)
    limit = (qpos // CHUNK + 1) * CHUNK
    wi = wiT_ref[0]

    def tile_start(kt):
        return pl.multiple_of(kt * TK, TK)

    zeros = jnp.zeros((HD, TQ), BF16)
    for h in range(ATT_HEADS):
        qh = qT_ref[0, h * HD:(h + 1) * HD, :]
        g = h // ATT_REP
        wq_ref[h] = jnp.concatenate([zeros] * g + [qh] + [zeros] * (ATT_KV_HEADS - 1 - g), axis=0)
    for h in range(IDX_HEADS):
        wqi_ref[h] = jnp.concatenate([qiT_ref[0, h * IDX_DIM:(h + 1) * IDX_DIM, :], zeros], axis=0)

    def score_tile(kt, carry):
        ks = tile_start(kt)
        kk = ki_ref[0, pl.ds(ks, TK), :]
        acc = jnp.zeros((TK, TQ), F32)
        for h in range(IDX_HEADS):
            logit = jnp.dot(kk, wqi_ref[h], preferred_element_type=F32)
            acc = acc + wi[h:h + 1, :] * jnp.maximum(logit, 0.0)
        sc = jnp.where(ks + key_in_tile < limit, acc, NEG)
        sc_ref[pl.ds(ks, TK), :] = sc
        bits = lax.bitcast_convert_type(sc, I32)
        key = jnp.where(bits < 0, jnp.int32(INT_MIN) - bits, bits)
        hi_ref[pl.ds(ks, TK), :] = jnp.right_shift(key, 16).astype(I16)
        lo_ref[pl.ds(ks, TK), :] = jnp.bitwise_xor(key, 0x8000).astype(I16)
        return carry

    lax.fori_loop(0, n_kt, score_tile, 0)

    def count16(ref, cand):
        c16 = cand.astype(I16)
        def body(kt, cnt):
            m = jnp.where(ref[pl.ds(tile_start(kt), TK), :] >= c16, jnp.int16(1), jnp.int16(0))
            for r in range(TK // COUNT_ROWS):
                cnt = cnt + m[r * COUNT_ROWS:(r + 1) * COUNT_ROWS]
            return cnt
        cnt = lax.fori_loop(0, n_kt, body, jnp.zeros((COUNT_ROWS, TQ), I16))
        return jnp.sum(cnt.astype(I32), axis=0, keepdims=True)

    def search16(ref, need, cnt_all):
        def body(it, carry):
            lo, cnt_lo = carry
            cand = lo + jnp.left_shift(jnp.int32(1), 15 - it)
            c = count16(ref, cand)
            ok = c >= need
            return jnp.where(ok, cand, lo), jnp.where(ok, c, cnt_lo)
        return lax.fori_loop(0, 16, body, (jnp.full((1, TQ), I16_MIN, I32), cnt_all))

    visited = jnp.full((1, TQ), n_kt * TK, I32)
    t_hi, cnt_hi = search16(hi_ref, topk, visited)
    cnt_above = jnp.where(t_hi == I16_MAX, 0, count16(hi_ref, jnp.minimum(t_hi + 1, I16_MAX)))
    t_hi16 = t_hi.astype(I16)

    def keep_tile(kt, carry):
        ks = tile_start(kt)
        lo_ref[pl.ds(ks, TK), :] = jnp.where(hi_ref[pl.ds(ks, TK), :] == t_hi16,
                                             lo_ref[pl.ds(ks, TK), :], jnp.int16(I16_MIN))
        return carry

    lax.fori_loop(0, n_kt, keep_tile, 0)
    t_lo, cnt_lo = search16(lo_ref, topk - cnt_above, cnt_hi - cnt_above)
    thr_key = t_hi * 65536 + (t_lo - I16_MIN)

    def key_to_float(key):
        bits = jnp.where(key < 0, jnp.int32(INT_MIN) - key, key)
        return lax.bitcast_convert_type(bits, F32)

    def count(pred):
        def body(kt, cnt):
            m = jnp.where(pred(sc_ref[pl.ds(tile_start(kt), TK), :]), 1.0, 0.0)
            return cnt + m.reshape(TK // FCOUNT_ROWS, FCOUNT_ROWS, TQ).sum(axis=0)
        cnt = lax.fori_loop(0, n_kt, body, jnp.zeros((FCOUNT_ROWS, TQ), F32))
        return jnp.sum(cnt, axis=0, keepdims=True)

    thr_ref[...] = jnp.where(thr_key == INT_MIN, -jnp.inf, key_to_float(thr_key))
    cnt_ref[...] = (cnt_above + cnt_lo).astype(F32)
    recount = count(lambda s: s >= thr_ref[...])
    disagree = jnp.max(jnp.abs(recount - cnt_ref[...])) > 0.0

    @pl.when(disagree)
    def _():
        def body(it, carry):
            lo, cnt_lo = carry
            cand = lo + jnp.left_shift(jnp.int32(1), 31 - it)
            cand_f = key_to_float(cand)
            c = count(lambda s: s >= cand_f)
            ok = c >= float(topk)
            return jnp.where(ok, cand, lo), jnp.where(ok, c, cnt_lo)
        lo, cnt_lo = lax.fori_loop(0, 32, body, (jnp.full((1, TQ), INT_MIN, I32),
                                                 jnp.full((1, TQ), n_kt * TK, I32).astype(F32)))
        thr_ref[...] = jnp.where(lo == INT_MIN, -jnp.inf, key_to_float(lo))
        cnt_ref[...] = cnt_lo

    thr = thr_ref[...]
    has_ties = jnp.max(cnt_ref[...]) > float(topk)

    @pl.when(jnp.logical_not(has_ties))
    def _():
        def bias_tile(kt, carry):
            ks = tile_start(kt)
            sel = (sc_ref[pl.ds(ks, TK), :] >= thr) & (ks + key_in_tile < limit)
            bias_ref[pl.ds(ks, TK), :] = jnp.where(sel, 0.0, NEG)
            return carry
        lax.fori_loop(0, n_kt, bias_tile, 0)

    @pl.when(has_ties)
    def _():
        need = float(topk) - count(lambda s: s > thr)
        ri = lax.broadcasted_iota(I32, (TK, TK), 0)
        ci = lax.broadcasted_iota(I32, (TK, TK), 1)
        tri = jnp.where(ci < ri, 1.0, 0.0).astype(BF16)

        def bias_tile(kt, run):
            ks = tile_start(kt)
            s = sc_ref[pl.ds(ks, TK), :]
            eq = jnp.where(s == thr, 1.0, 0.0)
            before = jnp.dot(tri, eq.astype(BF16), preferred_element_type=F32) + run
            sel = (s > thr) | ((s == thr) & (before < need))
            sel = sel & (ks + key_in_tile < limit)
            bias_ref[pl.ds(ks, TK), :] = jnp.where(sel, 0.0, NEG)
            return run + jnp.sum(eq, axis=0, keepdims=True)
        lax.fori_loop(0, n_kt, bias_tile, jnp.zeros((1, TQ), F32))

    m_ref[...] = jnp.full(m_ref.shape, -3e38, F32)
    acc_ref[...] = jnp.zeros(acc_ref.shape, F32)

    def scores(kt, slot):
        kk = k_ref[0, pl.ds(tile_start(jnp.minimum(kt, n_kt - 1)), TK), :]
        bias = bias_ref[pl.ds(tile_start(jnp.minimum(kt, n_kt)), TK), :]
        for h in range(ATT_HEADS):
            s_ref[slot, h] = jnp.dot(kk, wq_ref[h], preferred_element_type=F32) + bias

    def softmax(slot):
        for h in range(ATT_HEADS):
            for c in range(TQ // LANES):
                ql = slice(c * LANES, (c + 1) * LANES)
                s = s_ref[slot, h, :, ql]
                m_old = m_ref[h:h + 1, ql]
                m_new = jnp.maximum(m_old, jnp.max(s, axis=0, keepdims=True))
                p_ref[slot, h, :, ql] = jnp.exp2(s - m_new).astype(BF16)
                alpha_ref[slot, h:h + 1, ql] = jnp.exp2(m_old - m_new)
                m_ref[h:h + 1, ql] = m_new

    def weighted_values(kt, slot):
        ks = tile_start(jnp.clip(kt, 0, n_kt - 1))
        for h in range(ATT_HEADS):
            g = h // ATT_REP
            pv = jnp.dot(vT_ref[0, g * V_ROWS:(g + 1) * V_ROWS, pl.ds(ks, TK)], p_ref[slot, h],
                         preferred_element_type=F32)
            acc_ref[h] = alpha_ref[slot, h:h + 1, :] * acc_ref[h] + pv

    bias_ref[pl.ds(tile_start(n_kt), TK), :] = jnp.full((TK, TQ), NEG, F32)
    scores(0, 0)
    p_ref[1] = jnp.zeros(p_ref.shape[1:], BF16)
    alpha_ref[1] = jnp.ones(alpha_ref.shape[1:], F32)

    def att_pair(j, carry):
        scores(2 * j + 1, 1)
        softmax(0)
        weighted_values(2 * j - 1, 1)
        scores(2 * j + 2, 0)
        softmax(1)
        weighted_values(2 * j, 0)
        return carry

    n_pairs = (n_kt + 1) // 2
    lax.fori_loop(0, n_pairs, att_pair, 0)
    weighted_values(2 * n_pairs - 1, 1)

    for c in range(ATT_HEADS // 2):
        pair = [acc_ref[h, :HD, :] / acc_ref[h, HD:HD + 1, :] for h in (2 * c, 2 * c + 1)]
        o_ref[0, :, c * LANES:(c + 1) * LANES] = jnp.concatenate(pair, axis=0).T.astype(BF16)


def _dsa(qT, qiT, wiT, ki, k, vT):
    B, QW, S = qT.shape
    TQ = DSA_TQ
    topk = min(TOPK_MAX, S // 4)
    assert S % TQ == 0 and TQ == DSA_TK and DSA_TK >= topk
    return pl.pallas_call(
        functools.partial(_dsa_kernel, topk=topk),
        out_shape=jax.ShapeDtypeStruct((B, S, QW), BF16),
        grid=(B, S // TQ),
        in_specs=[
            pl.BlockSpec((1, QW, TQ), lambda b, i: (b, 0, i)),
            pl.BlockSpec((1, IDX_HEADS * IDX_DIM, TQ), lambda b, i: (b, 0, i)),
            pl.BlockSpec((1, IDX_HEADS, TQ), lambda b, i: (b, 0, i)),
            pl.BlockSpec((1, S, LANES), lambda b, i: (b, 0, 0)),
            pl.BlockSpec((1, S, LANES), lambda b, i: (b, 0, 0)),
            pl.BlockSpec((1, ATT_KV_HEADS * V_ROWS, S), lambda b, i: (b, 0, 0)),
        ],
        out_specs=pl.BlockSpec((1, TQ, QW), lambda b, i: (b, i, 0)),
        scratch_shapes=[
            pltpu.VMEM((ATT_HEADS, LANES, TQ), BF16),
            pltpu.VMEM((IDX_HEADS, LANES, TQ), BF16),
            pltpu.VMEM((S, TQ), F32),
            pltpu.VMEM((S, TQ), I16),
            pltpu.VMEM((S, TQ), I16),
            pltpu.VMEM((S + DSA_TK, TQ), F32),
            pltpu.VMEM((ATT_HEADS, TQ), F32),
            pltpu.VMEM((ATT_HEADS, V_ROWS, TQ), F32),
            pltpu.VMEM((2, ATT_HEADS, TQ), F32),
            pltpu.VMEM((2, ATT_HEADS, DSA_TK, TQ), F32),
            pltpu.VMEM((2, ATT_HEADS, DSA_TK, TQ), BF16),
            pltpu.VMEM((1, TQ), F32),
            pltpu.VMEM((1, TQ), F32),
        ],
        compiler_params=_params("parallel", "parallel"),
        name="dsa",
    )(qT, qiT, wiT, ki, k, vT)


S5_SLABS = S5_WIDTH // LANES
S5_SLAB_STATES = (LANES // S5_GROUP) * S5_STATE


def _s5_param_kernel(lre_ref, lim_ref, ldt_ref, bre_ref, bim_ref,
                     lbre_ref, lbim_ref, bbre_ref, bbim_ref):
    lre = jnp.minimum(lre_ref[...], -1e-4)
    lim = lim_ref[...]
    dt = jnp.exp(ldt_ref[...])
    mag = jnp.exp(lre * dt)
    lbre = mag * jnp.cos(lim * dt)
    lbim = mag * jnp.sin(lim * dt)
    nre = lbre - 1.0
    den = lre * lre + lim * lim
    cre = (nre * lre + lbim * lim) / den
    cim = (lbim * lre - nre * lim) / den
    lbre_ref[...] = lbre
    lbim_ref[...] = lbim
    bbre_ref[...] = cre * bre_ref[...] - cim * bim_ref[...]
    bbim_ref[...] = cre * bim_ref[...] + cim * bre_ref[...]


def _s5_params(lam_re, lam_im, log_dt, b_re, b_im, c_re, c_im):
    G, P, H = S5_GROUPS, S5_STATE, S5_GROUP
    n = G * P
    flat = lambda a: a.astype(F32).reshape(1, n)
    ldt = jnp.repeat(log_dt.astype(F32), P).reshape(1, n)
    bt = lambda a: a.astype(F32).transpose(2, 0, 1).reshape(H, n)
    lbre, lbim, bbre, bbim = pl.pallas_call(
        _s5_param_kernel,
        out_shape=(jax.ShapeDtypeStruct((1, n), F32), jax.ShapeDtypeStruct((1, n), F32),
                   jax.ShapeDtypeStruct((H, n), F32), jax.ShapeDtypeStruct((H, n), F32)),
        name="s5_params",
    )(flat(lam_re), flat(lam_im), ldt, bt(b_re), bt(b_im))

    gl = LANES // H
    eye = jnp.eye(gl, dtype=F32)
    bb = jnp.stack([bbre, bbim]).reshape(2, H, S5_SLABS, gl, P)
    wb = jnp.einsum("ahjgp,gk->jghakp", bb, eye).reshape(S5_SLABS, LANES, 2 * gl * P)
    cc = jnp.stack([c_re.astype(F32), -c_im.astype(F32)]).reshape(2, S5_SLABS, gl, H, P)
    wc = jnp.einsum("ajghp,gk->jagpkh", cc, eye).reshape(S5_SLABS, 2 * gl * P, LANES)
    lam = lambda a: jnp.broadcast_to(a.reshape(S5_SLABS, 1, gl * P), (S5_SLABS, SUBLANES, gl * P))
    return lam(lbre), lam(lbim), wb.astype(BF16), wc.astype(BF16)


def _s5_kernel(u_ref, lbre_ref, lbim_ref, wb_ref, wc_ref, d_ref, wglu_ref, bglu_ref, o_ref,
               state_ref, h_ref, y_ref, u_sc, *, steps, batch):
    NS = S5_SLAB_STATES
    t = pl.program_id(0)

    @pl.when(t == 0)
    def _():
        state_ref[...] = jnp.zeros(state_ref.shape, F32)

    W = S5_WIDTH
    for b in range(batch):
        for j in range(S5_SLABS):
            c0 = b * W + j * LANES
            u_sc[j, pl.ds(b, steps, stride=batch), :] = u_ref[:, c0:c0 + LANES]
    for j in range(S5_SLABS):
        h_ref[j] = jnp.dot(u_sc[j].astype(BF16), wb_ref[j], preferred_element_type=F32)

    for j in range(S5_SLABS):
        lre = lbre_ref[j]
        lim = lbim_ref[j]

        def step(s, carry):
            hre, him = carry
            r0 = pl.multiple_of(s * batch, batch)
            nre = lre * hre - lim * him + h_ref[j, pl.ds(r0, batch), :NS]
            nim = lre * him + lim * hre + h_ref[j, pl.ds(r0, batch), NS:]
            h_ref[j, pl.ds(r0, batch), :NS] = nre
            h_ref[j, pl.ds(r0, batch), NS:] = nim
            return nre, nim

        hre, him = lax.fori_loop(0, steps, step, (state_ref[j, :, :NS], state_ref[j, :, NS:]),
                                 unroll=4)
        state_ref[j, :, :NS] = hre
        state_ref[j, :, NS:] = him

    ys = [jnp.dot(h_ref[j].astype(BF16), wc_ref[j], preferred_element_type=F32)
          for j in range(S5_SLABS)]
    u = jnp.concatenate([u_sc[j] for j in range(S5_SLABS)], axis=1)
    y = jnp.concatenate(ys, axis=1) + d_ref[...] * u
    y = jax.nn.gelu(y)
    gate = jnp.dot(y.astype(BF16), wglu_ref[...], preferred_element_type=F32) + bglu_ref[...]
    y = y * jax.nn.sigmoid(gate)
    for j in range(S5_SLABS):
        y_ref[j] = y[:, j * LANES:(j + 1) * LANES]
    for b in range(batch):
        for j in range(S5_SLABS):
            o_ref[b, :, j * LANES:(j + 1) * LANES] = (
                y_ref[j, pl.ds(b, steps, stride=batch), :].astype(BF16))


def _s5(u2, lbre, lbim, wb, wc, d_skip, w_glu, b_glu, batch, steps):
    S, W = u2.shape[0], S5_WIDTH
    NS2 = 2 * S5_SLAB_STATES
    const = lambda *shape: pl.BlockSpec(shape, lambda t: (0,) * len(shape))
    return pl.pallas_call(
        functools.partial(_s5_kernel, steps=steps, batch=batch),
        out_shape=jax.ShapeDtypeStruct((batch, S, W), BF16),
        grid=(S // steps,),
        in_specs=[
            pl.BlockSpec((steps, batch * W), lambda t: (t, 0)),
            const(S5_SLABS, SUBLANES, S5_SLAB_STATES),
            const(S5_SLABS, SUBLANES, S5_SLAB_STATES),
            const(S5_SLABS, LANES, NS2),
            const(S5_SLABS, NS2, LANES),
            const(1, W),
            const(W, W),
            const(1, W),
        ],
        out_specs=pl.BlockSpec((batch, steps, W), lambda t: (0, t, 0)),
        scratch_shapes=[
            pltpu.VMEM((S5_SLABS, batch, NS2), F32),
            pltpu.VMEM((S5_SLABS, steps * batch, NS2), F32),
            pltpu.VMEM((S5_SLABS, steps * batch, LANES), F32),
            pltpu.VMEM((S5_SLABS, steps * batch, LANES), F32),
        ],
        compiler_params=_params("arbitrary"),
        name="s5",
    )(u2, lbre, lbim, wb, wc, d_skip.astype(F32).reshape(1, W), w_glu.astype(BF16),
      b_glu.astype(F32).reshape(1, W))


def _layer_norm(z, g, b):
    mu = jnp.mean(z, axis=-1, keepdims=True)
    zc = z - mu
    var = jnp.mean(zc * zc, axis=-1, keepdims=True)
    return zc * lax.rsqrt(var + LN_EPS) * g + b


def _merge_kernel(x_ref, a_ref, yb_ref, qm_ref, mem_ref, wkv_ref, wg_ref, bg_ref,
                  wpa_ref, wpb_ref, wpc_ref, wout_ref, g_ref, b_ref, o_ref, kv_ref, *, alpha):
    D = x_ref.shape[2]
    HW = MEM_HEADS * MEM_HEAD_DIM

    @pl.when(pl.program_id(1) == 0)
    def _():
        kv_ref[...] = jnp.dot(mem_ref[0].astype(BF16), wkv_ref[...],
                              preferred_element_type=F32).astype(BF16)

    x = x_ref[0]
    xb = x.astype(BF16)
    qm = qm_ref[0]
    heads = []
    for h in range(MEM_HEADS):
        sl = slice(h * MEM_HEAD_DIM, (h + 1) * MEM_HEAD_DIM)
        s = lax.dot_general(qm[:, sl], kv_ref[:, sl], (((1,), (1,)), ((), ())),
                            preferred_element_type=F32)
        p = jnp.exp(s - jnp.max(s, axis=1, keepdims=True))
        p = p / jnp.sum(p, axis=1, keepdims=True)
        heads.append(jnp.dot(p.astype(BF16), kv_ref[:, HW + h * MEM_HEAD_DIM:HW + (h + 1) * MEM_HEAD_DIM],
                             preferred_element_type=F32))
    c_in = jnp.concatenate(heads, axis=1).astype(BF16)

    y_a = jnp.dot(a_ref[0], wpa_ref[...], preferred_element_type=F32)
    y_b = jnp.dot(yb_ref[0], wpb_ref[...], preferred_element_type=F32)
    y_c = jnp.dot(c_in, wpc_ref[...], preferred_element_type=F32)
    merged = None
    for k, yk in enumerate((y_a, y_b, y_c)):
        gk = jax.nn.sigmoid(jnp.dot(xb, wg_ref[:, k * D:(k + 1) * D], preferred_element_type=F32)
                            + bg_ref[:, k * D:(k + 1) * D])
        merged = gk * yk if merged is None else merged + gk * yk
    mix = jnp.dot(merged.astype(BF16), wout_ref[...], preferred_element_type=F32)
    o_ref[0] = _layer_norm(alpha * x + mix, g_ref[...], b_ref[...])


def _merge(x, att, yb, qm, mem, w_mem_kv, w_gate, b_gate, w_proj_a, w_proj_b, w_proj_c, w_out,
           ln_g, ln_b, alpha, tile):
    B, S, D = x.shape
    n_mem = mem.shape[1]
    HW = MEM_HEADS * MEM_HEAD_DIM
    const = lambda *shape: pl.BlockSpec(shape, lambda b, t: (0,) * len(shape))
    row = lambda width: pl.BlockSpec((1, tile, width), lambda b, t: (b, t, 0))
    return pl.pallas_call(
        functools.partial(_merge_kernel, alpha=alpha),
        out_shape=jax.ShapeDtypeStruct((B, S, D), F32),
        grid=(B, S // tile),
        in_specs=[
            row(D), row(att.shape[2]), row(yb.shape[2]), row(HW),
            pl.BlockSpec((1, n_mem, D), lambda b, t: (b, 0, 0)),
            const(D, 2 * HW), const(D, N_BRANCH * D), const(1, N_BRANCH * D),
            const(att.shape[2], D), const(yb.shape[2], D), const(HW, D), const(D, D),
            const(1, D), const(1, D),
        ],
        out_specs=row(D),
        scratch_shapes=[pltpu.VMEM((n_mem, 2 * HW), BF16)],
        compiler_params=_params("parallel", "arbitrary"),
        name="merge",
    )(x, att, yb, qm, mem, w_mem_kv.astype(BF16), w_gate.astype(BF16),
      b_gate.astype(F32).reshape(1, -1), w_proj_a.astype(BF16), w_proj_b.astype(BF16),
      w_proj_c.astype(BF16), w_out.astype(BF16), ln_g.astype(F32).reshape(1, D),
      ln_b.astype(F32).reshape(1, D))


FFN_CHUNK = 256


def _ffn_kernel(h_ref, wup_ref, cw_ref, cb_ref, wdn_ref, g_ref, b_ref, o_ref,
                act_ref, tail_ref, *, alpha, ffn_dim):
    T = h_ref.shape[1]
    FC = FFN_CHUNK
    HALO = SUBLANES

    @pl.when(pl.program_id(1) == 0)
    def _():
        tail_ref[...] = jnp.zeros(tail_ref.shape, F32)

    h = h_ref[0]
    hb = h.astype(BF16)
    halo_row = lax.broadcasted_iota(I32, (HALO, FC), 0)

    def conv_cols(c0):
        up = jnp.dot(hb, wup_ref[:, c0:c0 + FC], preferred_element_type=F32)
        tail = tail_ref[:, c0:c0 + FC]
        tail_ref[:, c0:c0 + FC] = up[T - HALO:]
        out = cb_ref[:, c0:c0 + FC] + cw_ref[CONV_WIDTH - 1:CONV_WIDTH, c0:c0 + FC] * up
        for d in range(1, CONV_WIDTH):
            rolled = pltpu.roll(up, d, 0)
            top = jnp.where(halo_row < d, pltpu.roll(tail, d, 0), rolled[:HALO])
            delayed = jnp.concatenate([top, rolled[HALO:]], axis=0)
            out = out + cw_ref[CONV_WIDTH - 1 - d:CONV_WIDTH - d, c0:c0 + FC] * delayed
        return out

    for c in range(ffn_dim // FC):
        gate = conv_cols(c * FC)
        up = conv_cols(ffn_dim + c * FC)
        act_ref[:, c * FC:(c + 1) * FC] = (gate * jax.nn.sigmoid(gate) * up).astype(BF16)
    f = jnp.dot(act_ref[...], wdn_ref[...], preferred_element_type=F32)
    o_ref[0] = _layer_norm(alpha * h + f, g_ref[...], b_ref[...])


def _ffn(h, w_up, conv_w, conv_b, w_down, ln_g, ln_b, alpha, tile):
    B, S, D = h.shape
    F2 = w_up.shape[1]
    const = lambda *shape: pl.BlockSpec(shape, lambda b, t: (0,) * len(shape))
    row = pl.BlockSpec((1, tile, D), lambda b, t: (b, t, 0))
    return pl.pallas_call(
        functools.partial(_ffn_kernel, alpha=alpha, ffn_dim=F2 // 2),
        out_shape=jax.ShapeDtypeStruct((B, S, D), F32),
        grid=(B, S // tile),
        in_specs=[row, const(D, F2), const(CONV_WIDTH, F2), const(1, F2), const(F2 // 2, D),
                  const(1, D), const(1, D)],
        out_specs=row,
        scratch_shapes=[
            pltpu.VMEM((tile, F2 // 2), BF16),
            pltpu.VMEM((SUBLANES, F2), F32),
        ],
        compiler_params=_params("parallel", "arbitrary"),
        name="ffn",
    )(h, w_up.astype(BF16), conv_w.astype(F32), conv_b.astype(F32).reshape(1, F2),
      w_down.astype(BF16), ln_g.astype(F32).reshape(1, D), ln_b.astype(F32).reshape(1, D))


PROJ_TILE = 512
S5_STEPS = 64
MERGE_TILE = 512
FFN_TILE = 512


def kernel(x, mem, positions, w_in, w_gate, b_gate, s5_lam_re, s5_lam_im, s5_log_dt, s5_b_re, s5_b_im, s5_c_re, s5_c_im, s5_d, w_glu, b_glu, w_mem_kv, w_proj_a, w_proj_b, w_proj_c, w_out, ln1_g, ln1_b, w_up, conv_w, conv_b, w_down, ln2_g, ln2_b):
    B, S, D = x.shape
    depth = w_in.shape[0]
    alpha = (2.0 * depth) ** 0.25
    h = x
    for l in range(depth):
        qT, k, vT, qiT, ki, wiT, u, qm = _proj(h, positions, w_in[l], min(PROJ_TILE, S))
        att = _dsa(qT, qiT, wiT, ki, k, vT)
        lbre, lbim, wb, wc = _s5_params(s5_lam_re[l], s5_lam_im[l], s5_log_dt[l], s5_b_re[l],
                                        s5_b_im[l], s5_c_re[l], s5_c_im[l])
        yb = _s5(u, lbre, lbim, wb, wc, s5_d[l], w_glu[l], b_glu[l], B, min(S5_STEPS, S))
        h = _merge(h, att, yb, qm, mem, w_mem_kv[l], w_gate[l], b_gate[l], w_proj_a[l],
                   w_proj_b[l], w_proj_c[l], w_out[l], ln1_g[l], ln1_b[l], alpha, min(MERGE_TILE, S))
        h = _ffn(h, w_up[l], conv_w[l], conv_b[l], w_down[l], ln2_g[l], ln2_b[l], alpha,
                 min(FFN_TILE, S))
    return h
```

```python
import functools
import math

import jax
import jax.numpy as jnp
from jax import lax
from jax.experimental import pallas as pl
from jax.experimental.pallas import tpu as pltpu

F32 = jnp.float32
BF16 = jnp.bfloat16
I32 = jnp.int32

CHUNK = 64
ATT_HEADS = 8
ATT_KV_HEADS = 2
ATT_HEAD_DIM = 64
ATT_REP = ATT_HEADS // ATT_KV_HEADS
IDX_HEADS = 4
IDX_DIM = 64
TOPK_MAX = 256
ROPE_THETA = 10000.0
S5_WIDTH = 512
S5_GROUP = 16
S5_GROUPS = S5_WIDTH // S5_GROUP
S5_STATE = 64
MEM_HEADS = 4
MEM_HEAD_DIM = 128
CONV_WIDTH = 3
N_BRANCH = 3
LN_EPS = 1e-5
NEG = -1e30
INT_MIN = -(2 ** 31)
LOG2E = math.log2(math.e)

LANES = 128
SUBLANES = 8
VMEM_LIMIT = 56 * 1024 * 1024

_Q0 = 0
_K0 = _Q0 + ATT_HEADS * ATT_HEAD_DIM
_V0 = _K0 + ATT_KV_HEADS * ATT_HEAD_DIM
_QI0 = _V0 + ATT_KV_HEADS * ATT_HEAD_DIM
_KI0 = _QI0 + IDX_HEADS * IDX_DIM
_WI0 = _KI0 + IDX_DIM
_U0 = _WI0 + IDX_HEADS
_QM0 = _U0 + S5_WIDTH
_END = _QM0 + MEM_HEADS * MEM_HEAD_DIM
_PAD = LANES - IDX_DIM - IDX_HEADS
_PU0 = _KI0 + LANES
_PQM0 = _PU0 + S5_WIDTH
_PEND = _PQM0 + MEM_HEADS * MEM_HEAD_DIM
V_ROWS = ATT_HEAD_DIM + 16


def _params(*sem):
    return pltpu.CompilerParams(dimension_semantics=sem, vmem_limit_bytes=VMEM_LIMIT)


def _proj_kernel(x_ref, pos_ref, invf_ref, w_ref,
                 qT_ref, k_ref, vT_ref, qiT_ref, ki_ref, wiT_ref, u_ref, qm_ref):
    T = x_ref.shape[1]
    y = jnp.dot(x_ref[0].astype(BF16), w_ref[...], preferred_element_type=F32)

    ang = pos_ref[0].astype(F32) * invf_ref[...]
    cos = jnp.cos(ang)
    sin = jnp.sin(ang)
    lane = lax.broadcasted_iota(I32, (T, LANES), 1)
    first = (lane % ATT_HEAD_DIM) < (ATT_HEAD_DIM // 2)
    sin = jnp.where(first, -sin, sin)

    def rope(z):
        partner = jnp.where(first, pltpu.roll(z, LANES - 32, 1), pltpu.roll(z, 32, 1))
        return z * cos + partner * sin

    att_scale = ATT_HEAD_DIM ** -0.5 * LOG2E
    for c in range(ATT_HEADS // 2):
        z = rope(y[:, _Q0 + c * LANES:_Q0 + (c + 1) * LANES]) * att_scale
        qT_ref[0, c * LANES:(c + 1) * LANES, :] = z.T.astype(BF16)
    k_ref[0] = rope(y[:, _K0:_K0 + LANES]).astype(BF16)
    vT = y[:, _V0:_V0 + LANES].T.astype(BF16)
    for g in range(ATT_KV_HEADS):
        vT_ref[0, g * V_ROWS:g * V_ROWS + ATT_HEAD_DIM, :] = vT[g * ATT_HEAD_DIM:(g + 1) * ATT_HEAD_DIM]
        vT_ref[0, g * V_ROWS + ATT_HEAD_DIM:(g + 1) * V_ROWS, :] = jnp.ones((V_ROWS - ATT_HEAD_DIM, T), BF16)
    idx_scale = IDX_DIM ** -0.5
    for c in range(IDX_HEADS // 2):
        z = rope(y[:, _QI0 + c * LANES:_QI0 + (c + 1) * LANES]) * idx_scale
        qiT_ref[0, c * LANES:(c + 1) * LANES, :] = z.T.astype(BF16)
    kw = y[:, _KI0:_KI0 + LANES]
    ki_ref[0] = jnp.where(lane < IDX_DIM, rope(kw), 0.0).astype(BF16)
    wiT_ref[0] = kw.T[IDX_DIM:IDX_DIM + IDX_HEADS] * (IDX_HEADS ** -0.5)
    u_ref[...] = y[:, _PU0:_PU0 + S5_WIDTH]
    qm_ref[0] = (y[:, _PQM0:_PEND] * (MEM_HEAD_DIM ** -0.5)).astype(BF16)


def _proj(x, positions, w_in, tile):
    B, S, D = x.shape
    nt = S // tile
    w_pad = jnp.concatenate(
        [w_in[:, :_U0], jnp.zeros((D, _PAD), w_in.dtype), w_in[:, _U0:]], axis=1).astype(BF16)
    half = ATT_HEAD_DIM // 2
    inv_freq = ROPE_THETA ** (-jnp.arange(half, dtype=F32) / half)
    invf = jnp.tile(inv_freq, LANES // half)[None, :]
    pos3 = positions.reshape(B, S, 1)
    QW = ATT_HEADS * ATT_HEAD_DIM
    IW = IDX_HEADS * IDX_DIM
    out_shape = (
        jax.ShapeDtypeStruct((B, QW, S), BF16),
        jax.ShapeDtypeStruct((B, S, LANES), BF16),
        jax.ShapeDtypeStruct((B, ATT_KV_HEADS * V_ROWS, S), BF16),
        jax.ShapeDtypeStruct((B, IW, S), BF16),
        jax.ShapeDtypeStruct((B, S, LANES), BF16),
        jax.ShapeDtypeStruct((B, IDX_HEADS, S), F32),
        jax.ShapeDtypeStruct((S, B * S5_WIDTH), F32),
        jax.ShapeDtypeStruct((B, S, MEM_HEADS * MEM_HEAD_DIM), BF16),
    )
    rows = lambda width: pl.BlockSpec((1, tile, width), lambda b, t: (b, t, 0))
    cols = lambda height: pl.BlockSpec((1, height, tile), lambda b, t: (b, 0, t))
    out_specs = (
        cols(QW), rows(LANES), cols(ATT_KV_HEADS * V_ROWS), cols(IW), rows(LANES), cols(IDX_HEADS),
        pl.BlockSpec((tile, S5_WIDTH), lambda b, t: (t, b)),
        rows(MEM_HEADS * MEM_HEAD_DIM),
    )
    return pl.pallas_call(
        _proj_kernel,
        out_shape=out_shape,
        grid=(B, nt),
        in_specs=[
            pl.BlockSpec((1, tile, D), lambda b, t: (b, t, 0)),
            pl.BlockSpec((1, tile, 1), lambda b, t: (b, t, 0)),
            pl.BlockSpec((1, LANES), lambda b, t: (0, 0)),
            pl.BlockSpec((D, _PEND), lambda b, t: (0, 0)),
        ],
        out_specs=out_specs,
        compiler_params=_params("parallel", "parallel"),
        name="proj",
    )(x, pos3, invf, w_pad)


DSA_TQ = 256
DSA_TK = 256
PLANE_KEYS = 32 * SUBLANES
FCOUNT_ROWS = 4 * SUBLANES


def _bit_transpose32(words):
    a = list(words)
    j, mask = 16, 0x0000FFFF
    while j:
        k = 0
        while k < 32:
            t = (a[k] ^ lax.shift_right_logical(a[k + j], jnp.int32(j))) & jnp.int32(mask)
            a[k] = a[k] ^ t
            a[k + j] = a[k + j] ^ lax.shift_left(t, jnp.int32(j))
            k = (k + j + 1) & ~j
        j >>= 1
        mask = (mask ^ (mask << j)) & 0xFFFFFFFF
    return a


def _dsa_kernel(qT_ref, qiT_ref, wiT_ref, ki_ref, k_ref, vT_ref, o_ref,
                wq_ref, wqi_ref, sc_ref, planes_ref, bias_ref, m_ref, acc_ref, alpha_ref,
                s_ref, p_ref, thr_ref, cnt_ref,
                *, topk):
    TQ, TK = DSA_TQ, DSA_TK
    HD = ATT_HEAD_DIM
    i = pl.program_id(1)
    n_kt = (i * TQ + TQ + TK - 1) // TK
    key_in_tile = lax.broadcasted_iota(I32, (TK, TQ), 0)
    qpos = i * TQ + lax.broadcasted_iota(I32, (1, TQ), 1)
    limit = (qpos // CHUNK + 1) * CHUNK
    wi = wiT_ref[0]

    def tile_start(kt):
        return pl.multiple_of(kt * TK, TK)

    zeros = jnp.zeros((HD, TQ), BF16)
    for h in range(ATT_HEADS):
        qh = qT_ref[0, h * HD:(h + 1) * HD, :]
        g = h // ATT_REP
        wq_ref[h] = jnp.concatenate([zeros] * g + [qh] + [zeros] * (ATT_KV_HEADS - 1 - g), axis=0)
    for h in range(IDX_HEADS):
        wqi_ref[h] = jnp.concatenate([qiT_ref[0, h * IDX_DIM:(h + 1) * IDX_DIM, :], zeros], axis=0)

    @pl.when((pl.program_id(0) == 0) & (i == 0))
    def _():
        planes_ref[...] = jnp.zeros(planes_ref.shape, I32)

    def score_tile(kt, carry):
        ks = tile_start(kt)
        kk = ki_ref[0, pl.ds(ks, TK), :]
        acc = jnp.zeros((TK, TQ), F32)
        for h in range(IDX_HEADS):
            logit = jnp.dot(kk, wqi_ref[h], preferred_element_type=F32)
            acc = acc + wi[h:h + 1, :] * jnp.maximum(logit, 0.0)
        sc = jnp.where(ks + key_in_tile < limit, acc, NEG)
        sc_ref[pl.ds(ks, TK), :] = sc
        bits = lax.bitcast_convert_type(sc, I32)
        ukey = jnp.where(bits < 0, -bits, bits ^ jnp.int32(INT_MIN))
        for g in range(TK // PLANE_KEYS):
            rows = ukey[g * PLANE_KEYS:(g + 1) * PLANE_KEYS]
            words = _bit_transpose32([rows[j * SUBLANES:(j + 1) * SUBLANES] for j in range(32)])
            for b in range(32):
                planes_ref[b, pl.ds(pl.multiple_of(kt * (TK // 32) + g * SUBLANES, SUBLANES), SUBLANES), :] = (
                    words[31 - b])
        return carry

    lax.fori_loop(0, n_kt, score_tile, 0)

    n_words = planes_ref.shape[1]
    word_row = lax.broadcasted_iota(I32, (n_words, TQ), 0)
    cand0 = jnp.where(word_row < n_kt * (TK // 32), jnp.int32(-1), jnp.int32(0))

    def popcount_rows(w):
        return jnp.sum(lax.population_count(w), axis=0, keepdims=True)

    def bit_body(it, carry):
        cand, need, ukey_thr = carry
        b = 31 - it
        ones = cand & planes_ref[b]
        c1 = popcount_rows(ones)
        ok = c1 >= need
        cand = jnp.where(ok, ones, cand ^ ones)
        need = jnp.where(ok, need, need - c1)
        ukey_thr = ukey_thr | jnp.left_shift(ok.astype(I32), b)
        return cand, need, ukey_thr

    cand, need, ukey_thr = lax.fori_loop(
        0, 32, bit_body, (cand0, jnp.full((1, TQ), topk, I32), jnp.zeros((1, TQ), I32)))
    thr_key = ukey_thr ^ jnp.int32(INT_MIN)
    cnt_key = (topk - need) + popcount_rows(cand)

    def key_to_float(key):
        bits = jnp.where(key < 0, jnp.int32(INT_MIN) - key, key)
        return lax.bitcast_convert_type(bits, F32)

    def count(pred):
        def body(kt, cnt):
            m = jnp.where(pred(sc_ref[pl.ds(tile_start(kt), TK), :]), 1.0, 0.0)
            return cnt + m.reshape(TK // FCOUNT_ROWS, FCOUNT_ROWS, TQ).sum(axis=0)
        cnt = lax.fori_loop(0, n_kt, body, jnp.zeros((FCOUNT_ROWS, TQ), F32))
        return jnp.sum(cnt, axis=0, keepdims=True)

    thr_ref[...] = jnp.where(thr_key == INT_MIN, -jnp.inf, key_to_float(thr_key))
    cnt_ref[...] = cnt_key.astype(F32)
    recount = count(lambda s: s >= thr_ref[...])
    disagree = jnp.max(jnp.abs(recount - cnt_ref[...])) > 0.0

    @pl.when(disagree)
    def _():
        def body(it, carry):
            lo, cnt_lo = carry
            cand = lo + jnp.left_shift(jnp.int32(1), 31 - it)
            cand_f = key_to_float(cand)
            c = count(lambda s: s >= cand_f)
            ok = c >= float(topk)
            return jnp.where(ok, cand, lo), jnp.where(ok, c, cnt_lo)
        lo, cnt_lo = lax.fori_loop(0, 32, body, (jnp.full((1, TQ), INT_MIN, I32),
                                                 jnp.full((1, TQ), n_kt * TK, I32).astype(F32)))
        thr_ref[...] = jnp.where(lo == INT_MIN, -jnp.inf, key_to_float(lo))
        cnt_ref[...] = cnt_lo

    thr = thr_ref[...]
    has_ties = jnp.max(cnt_ref[...]) > float(topk)

    @pl.when(jnp.logical_not(has_ties))
    def _():
        def bias_tile(kt, carry):
            ks = tile_start(kt)
            sel = (sc_ref[pl.ds(ks, TK), :] >= thr) & (ks + key_in_tile < limit)
            bias_ref[pl.ds(ks, TK), :] = jnp.where(sel, 0.0, NEG)
            return carry
        lax.fori_loop(0, n_kt, bias_tile, 0)

    @pl.when(has_ties)
    def _():
        need = float(topk) - count(lambda s: s > thr)
        ri = lax.broadcasted_iota(I32, (TK, TK), 0)
        ci = lax.broadcasted_iota(I32, (TK, TK), 1)
        tri = jnp.where(ci < ri, 1.0, 0.0).astype(BF16)

        def bias_tile(kt, run):
            ks = tile_start(kt)
            s = sc_ref[pl.ds(ks, TK), :]
            eq = jnp.where(s == thr, 1.0, 0.0)
            before = jnp.dot(tri, eq.astype(BF16), preferred_element_type=F32) + run
            sel = (s > thr) | ((s == thr) & (before < need))
            sel = sel & (ks + key_in_tile < limit)
            bias_ref[pl.ds(ks, TK), :] = jnp.where(sel, 0.0, NEG)
            return run + jnp.sum(eq, axis=0, keepdims=True)
        lax.fori_loop(0, n_kt, bias_tile, jnp.zeros((1, TQ), F32))

    m_ref[...] = jnp.full(m_ref.shape, -3e38, F32)
    acc_ref[...] = jnp.zeros(acc_ref.shape, F32)

    def scores(kt, slot):
        kk = k_ref[0, pl.ds(tile_start(jnp.minimum(kt, n_kt - 1)), TK), :]
        bias = bias_ref[pl.ds(tile_start(jnp.minimum(kt, n_kt)), TK), :]
        for h in range(ATT_HEADS):
            s_ref[slot, h] = jnp.dot(kk, wq_ref[h], preferred_element_type=F32) + bias

    def softmax(slot):
        for h in range(ATT_HEADS):
            for c in range(TQ // LANES):
                ql = slice(c * LANES, (c + 1) * LANES)
                s = s_ref[slot, h, :, ql]
                m_old = m_ref[h:h + 1, ql]
                m_new = jnp.maximum(m_old, jnp.max(s, axis=0, keepdims=True))
                p_ref[slot, h, :, ql] = jnp.exp2(s - m_new).astype(BF16)
                alpha_ref[slot, h:h + 1, ql] = jnp.exp2(m_old - m_new)
                m_ref[h:h + 1, ql] = m_new

    def weighted_values(kt, slot):
        ks = tile_start(jnp.clip(kt, 0, n_kt - 1))
        for h in range(ATT_HEADS):
            g = h // ATT_REP
            pv = jnp.dot(vT_ref[0, g * V_ROWS:(g + 1) * V_ROWS, pl.ds(ks, TK)], p_ref[slot, h],
                         preferred_element_type=F32)
            acc_ref[h] = alpha_ref[slot, h:h + 1, :] * acc_ref[h] + pv

    bias_ref[pl.ds(tile_start(n_kt), TK), :] = jnp.full((TK, TQ), NEG, F32)
    scores(0, 0)
    p_ref[1] = jnp.zeros(p_ref.shape[1:], BF16)
    alpha_ref[1] = jnp.ones(alpha_ref.shape[1:], F32)

    def att_pair(j, carry):
        scores(2 * j + 1, 1)
        softmax(0)
        weighted_values(2 * j - 1, 1)
        scores(2 * j + 2, 0)
        softmax(1)
        weighted_values(2 * j, 0)
        return carry

    n_pairs = (n_kt + 1) // 2
    lax.fori_loop(0, n_pairs, att_pair, 0)
    weighted_values(2 * n_pairs - 1, 1)

    for c in range(ATT_HEADS // 2):
        pair = [acc_ref[h, :HD, :] / acc_ref[h, HD:HD + 1, :] for h in (2 * c, 2 * c + 1)]
        o_ref[0, :, c * LANES:(c + 1) * LANES] = jnp.concatenate(pair, axis=0).T.astype(BF16)


def _dsa(qT, qiT, wiT, ki, k, vT):
    B, QW, S = qT.shape
    TQ = DSA_TQ
    topk = min(TOPK_MAX, S // 4)
    assert S % TQ == 0 and TQ == DSA_TK and DSA_TK >= topk
    return pl.pallas_call(
        functools.partial(_dsa_kernel, topk=topk),
        out_shape=jax.ShapeDtypeStruct((B, S, QW), BF16),
        grid=(B, S // TQ),
        in_specs=[
            pl.BlockSpec((1, QW, TQ), lambda b, i: (b, 0, i)),
            pl.BlockSpec((1, IDX_HEADS * IDX_DIM, TQ), lambda b, i: (b, 0, i)),
            pl.BlockSpec((1, IDX_HEADS, TQ), lambda b, i: (b, 0, i)),
            pl.BlockSpec((1, S, LANES), lambda b, i: (b, 0, 0)),
            pl.BlockSpec((1, S, LANES), lambda b, i: (b, 0, 0)),
            pl.BlockSpec((1, ATT_KV_HEADS * V_ROWS, S), lambda b, i: (b, 0, 0)),
        ],
        out_specs=pl.BlockSpec((1, TQ, QW), lambda b, i: (b, i, 0)),
        scratch_shapes=[
            pltpu.VMEM((ATT_HEADS, LANES, TQ), BF16),
            pltpu.VMEM((IDX_HEADS, LANES, TQ), BF16),
            pltpu.VMEM((S, TQ), F32),
            pltpu.VMEM((32, S // 32, TQ), I32),
            pltpu.VMEM((S + DSA_TK, TQ), F32),
            pltpu.VMEM((ATT_HEADS, TQ), F32),
            pltpu.VMEM((ATT_HEADS, V_ROWS, TQ), F32),
            pltpu.VMEM((2, ATT_HEADS, TQ), F32),
            pltpu.VMEM((2, ATT_HEADS, DSA_TK, TQ), F32),
            pltpu.VMEM((2, ATT_HEADS, DSA_TK, TQ), BF16),
            pltpu.VMEM((1, TQ), F32),
            pltpu.VMEM((1, TQ), F32),
        ],
        compiler_params=_params("arbitrary", "arbitrary"),
        name="dsa",
    )(qT, qiT, wiT, ki, k, vT)


S5_SLABS = S5_WIDTH // LANES
S5_SLAB_STATES = (LANES // S5_GROUP) * S5_STATE


def _s5_param_kernel(lre_ref, lim_ref, ldt_ref, bre_ref, bim_ref,
                     lbre_ref, lbim_ref, bbre_ref, bbim_ref):
    lre = jnp.minimum(lre_ref[...], -1e-4)
    lim = lim_ref[...]
    dt = jnp.exp(ldt_ref[...])
    mag = jnp.exp(lre * dt)
    lbre = mag * jnp.cos(lim * dt)
    lbim = mag * jnp.sin(lim * dt)
    nre = lbre - 1.0
    den = lre * lre + lim * lim
    cre = (nre * lre + lbim * lim) / den
    cim = (lbim * lre - nre * lim) / den
    lbre_ref[...] = lbre
    lbim_ref[...] = lbim
    bbre_ref[...] = cre * bre_ref[...] - cim * bim_ref[...]
    bbim_ref[...] = cre * bim_ref[...] + cim * bre_ref[...]


def _s5_params(lam_re, lam_im, log_dt, b_re, b_im, c_re, c_im):
    G, P, H = S5_GROUPS, S5_STATE, S5_GROUP
    n = G * P
    flat = lambda a: a.astype(F32).reshape(1, n)
    ldt = jnp.repeat(log_dt.astype(F32), P).reshape(1, n)
    bt = lambda a: a.astype(F32).transpose(2, 0, 1).reshape(H, n)
    lbre, lbim, bbre, bbim = pl.pallas_call(
        _s5_param_kernel,
        out_shape=(jax.ShapeDtypeStruct((1, n), F32), jax.ShapeDtypeStruct((1, n), F32),
                   jax.ShapeDtypeStruct((H, n), F32), jax.ShapeDtypeStruct((H, n), F32)),
        name="s5_params",
    )(flat(lam_re), flat(lam_im), ldt, bt(b_re), bt(b_im))

    gl = LANES // H
    eye = jnp.eye(gl, dtype=F32)
    bb = jnp.stack([bbre, bbim]).reshape(2, H, S5_SLABS, gl, P)
    wb = jnp.einsum("ahjgp,gk->jghakp", bb, eye).reshape(S5_SLABS, LANES, 2 * gl * P)
    cc = jnp.stack([c_re.astype(F32), -c_im.astype(F32)]).reshape(2, S5_SLABS, gl, H, P)
    wc = jnp.einsum("ajghp,gk->jagpkh", cc, eye).reshape(S5_SLABS, 2 * gl * P, LANES)
    lam = lambda a: jnp.broadcast_to(a.reshape(S5_SLABS, 1, gl * P), (S5_SLABS, SUBLANES, gl * P))
    return lam(lbre), lam(lbim), wb.astype(BF16), wc.astype(BF16)


def _s5_kernel(u_ref, lbre_ref, lbim_ref, wb_ref, wc_ref, d_ref, wglu_ref, bglu_ref, o_ref,
               state_ref, h_ref, y_ref, u_sc, *, steps, batch):
    NS = S5_SLAB_STATES
    t = pl.program_id(0)

    @pl.when(t == 0)
    def _():
        state_ref[...] = jnp.zeros(state_ref.shape, F32)

    W = S5_WIDTH
    for b in range(batch):
        for j in range(S5_SLABS):
            c0 = b * W + j * LANES
            u_sc[j, pl.ds(b, steps, stride=batch), :] = u_ref[:, c0:c0 + LANES]
    for j in range(S5_SLABS):
        h_ref[j] = jnp.dot(u_sc[j].astype(BF16), wb_ref[j], preferred_element_type=F32)

    for j in range(S5_SLABS):
        lre = lbre_ref[j]
        lim = lbim_ref[j]

        def step(s, carry):
            hre, him = carry
            r0 = pl.multiple_of(s * batch, batch)
            nre = lre * hre - lim * him + h_ref[j, pl.ds(r0, batch), :NS]
            nim = lre * him + lim * hre + h_ref[j, pl.ds(r0, batch), NS:]
            h_ref[j, pl.ds(r0, batch), :NS] = nre
            h_ref[j, pl.ds(r0, batch), NS:] = nim
            return nre, nim

        hre, him = lax.fori_loop(0, steps, step, (state_ref[j, :, :NS], state_ref[j, :, NS:]),
                                 unroll=4)
        state_ref[j, :, :NS] = hre
        state_ref[j, :, NS:] = him

    ys = [jnp.dot(h_ref[j].astype(BF16), wc_ref[j], preferred_element_type=F32)
          for j in range(S5_SLABS)]
    u = jnp.concatenate([u_sc[j] for j in range(S5_SLABS)], axis=1)
    y = jnp.concatenate(ys, axis=1) + d_ref[...] * u
    y = jax.nn.gelu(y)
    gate = jnp.dot(y.astype(BF16), wglu_ref[...], preferred_element_type=F32) + bglu_ref[...]
    y = y * jax.nn.sigmoid(gate)
    for j in range(S5_SLABS):
        y_ref[j] = y[:, j * LANES:(j + 1) * LANES]
    for b in range(batch):
        for j in range(S5_SLABS):
            o_ref[b, :, j * LANES:(j + 1) * LANES] = (
                y_ref[j, pl.ds(b, steps, stride=batch), :].astype(BF16))


def _s5(u2, lbre, lbim, wb, wc, d_skip, w_glu, b_glu, batch, steps):
    S, W = u2.shape[0], S5_WIDTH
    NS2 = 2 * S5_SLAB_STATES
    const = lambda *shape: pl.BlockSpec(shape, lambda t: (0,) * len(shape))
    return pl.pallas_call(
        functools.partial(_s5_kernel, steps=steps, batch=batch),
        out_shape=jax.ShapeDtypeStruct((batch, S, W), BF16),
        grid=(S // steps,),
        in_specs=[
            pl.BlockSpec((steps, batch * W), lambda t: (t, 0)),
            const(S5_SLABS, SUBLANES, S5_SLAB_STATES),
            const(S5_SLABS, SUBLANES, S5_SLAB_STATES),
            const(S5_SLABS, LANES, NS2),
            const(S5_SLABS, NS2, LANES),
            const(1, W),
            const(W, W),
            const(1, W),
        ],
        out_specs=pl.BlockSpec((batch, steps, W), lambda t: (0, t, 0)),
        scratch_shapes=[
            pltpu.VMEM((S5_SLABS, batch, NS2), F32),
            pltpu.VMEM((S5_SLABS, steps * batch, NS2), F32),
            pltpu.VMEM((S5_SLABS, steps * batch, LANES), F32),
            pltpu.VMEM((S5_SLABS, steps * batch, LANES), F32),
        ],
        compiler_params=_params("arbitrary"),
        name="s5",
    )(u2, lbre, lbim, wb, wc, d_skip.astype(F32).reshape(1, W), w_glu.astype(BF16),
      b_glu.astype(F32).reshape(1, W))


def _layer_norm(z, g, b):
    mu = jnp.mean(z, axis=-1, keepdims=True)
    zc = z - mu
    var = jnp.mean(zc * zc, axis=-1, keepdims=True)
    return zc * lax.rsqrt(var + LN_EPS) * g + b


def _merge_kernel(x_ref, a_ref, yb_ref, qm_ref, mem_ref, wkv_ref, wg_ref, bg_ref,
                  wpa_ref, wpb_ref, wpc_ref, wout_ref, g_ref, b_ref, o_ref, kv_ref, *, alpha):
    D = x_ref.shape[2]
    HW = MEM_HEADS * MEM_HEAD_DIM

    @pl.when(pl.program_id(1) == 0)
    def _():
        kv_ref[...] = jnp.dot(mem_ref[0].astype(BF16), wkv_ref[...],
                              preferred_element_type=F32).astype(BF16)

    x = x_ref[0]
    xb = x.astype(BF16)
    qm = qm_ref[0]
    heads = []
    for h in range(MEM_HEADS):
        sl = slice(h * MEM_HEAD_DIM, (h + 1) * MEM_HEAD_DIM)
        s = lax.dot_general(qm[:, sl], kv_ref[:, sl], (((1,), (1,)), ((), ())),
                            preferred_element_type=F32)
        p = jnp.exp(s - jnp.max(s, axis=1, keepdims=True))
        p = p / jnp.sum(p, axis=1, keepdims=True)
        heads.append(jnp.dot(p.astype(BF16), kv_ref[:, HW + h * MEM_HEAD_DIM:HW + (h + 1) * MEM_HEAD_DIM],
                             preferred_element_type=F32))
    c_in = jnp.concatenate(heads, axis=1).astype(BF16)

    y_a = jnp.dot(a_ref[0], wpa_ref[...], preferred_element_type=F32)
    y_b = jnp.dot(yb_ref[0], wpb_ref[...], preferred_element_type=F32)
    y_c = jnp.dot(c_in, wpc_ref[...], preferred_element_type=F32)
    merged = None
    for k, yk in enumerate((y_a, y_b, y_c)):
        gk = jax.nn.sigmoid(jnp.dot(xb, wg_ref[:, k * D:(k + 1) * D], preferred_element_type=F32)
                            + bg_ref[:, k * D:(k + 1) * D])
        merged = gk * yk if merged is None else merged + gk * yk
    mix = jnp.dot(merged.astype(BF16), wout_ref[...], preferred_element_type=F32)
    o_ref[0] = _layer_norm(alpha * x + mix, g_ref[...], b_ref[...])


def _merge(x, att, yb, qm, mem, w_mem_kv, w_gate, b_gate, w_proj_a, w_proj_b, w_proj_c, w_out,
           ln_g, ln_b, alpha, tile):
    B, S, D = x.shape
    n_mem = mem.shape[1]
    HW = MEM_HEADS * MEM_HEAD_DIM
    const = lambda *shape: pl.BlockSpec(shape, lambda b, t: (0,) * len(shape))
    row = lambda width: pl.BlockSpec((1, tile, width), lambda b, t: (b, t, 0))
    return pl.pallas_call(
        functools.partial(_merge_kernel, alpha=alpha),
        out_shape=jax.ShapeDtypeStruct((B, S, D), F32),
        grid=(B, S // tile),
        in_specs=[
            row(D), row(att.shape[2]), row(yb.shape[2]), row(HW),
            pl.BlockSpec((1, n_mem, D), lambda b, t: (b, 0, 0)),
            const(D, 2 * HW), const(D, N_BRANCH * D), const(1, N_BRANCH * D),
            const(att.shape[2], D), const(yb.shape[2], D), const(HW, D), const(D, D),
            const(1, D), const(1, D),
        ],
        out_specs=row(D),
        scratch_shapes=[pltpu.VMEM((n_mem, 2 * HW), BF16)],
        compiler_params=_params("parallel", "arbitrary"),
        name="merge",
    )(x, att, yb, qm, mem, w_mem_kv.astype(BF16), w_gate.astype(BF16),
      b_gate.astype(F32).reshape(1, -1), w_proj_a.astype(BF16), w_proj_b.astype(BF16),
      w_proj_c.astype(BF16), w_out.astype(BF16), ln_g.astype(F32).reshape(1, D),
      ln_b.astype(F32).reshape(1, D))


FFN_CHUNK = 256


def _ffn_kernel(h_ref, wup_ref, cw_ref, cb_ref, wdn_ref, g_ref, b_ref, o_ref,
                act_ref, tail_ref, *, alpha, ffn_dim):
    T = h_ref.shape[1]
    FC = FFN_CHUNK
    HALO = SUBLANES

    @pl.when(pl.program_id(1) == 0)
    def _():
        tail_ref[...] = jnp.zeros(tail_ref.shape, F32)

    h = h_ref[0]
    hb = h.astype(BF16)
    halo_row = lax.broadcasted_iota(I32, (HALO, FC), 0)

    def conv_cols(c0):
        up = jnp.dot(hb, wup_ref[:, c0:c0 + FC], preferred_element_type=F32)
        tail = tail_ref[:, c0:c0 + FC]
        tail_ref[:, c0:c0 + FC] = up[T - HALO:]
        out = cb_ref[:, c0:c0 + FC] + cw_ref[CONV_WIDTH - 1:CONV_WIDTH, c0:c0 + FC] * up
        for d in range(1, CONV_WIDTH):
            rolled = pltpu.roll(up, d, 0)
            top = jnp.where(halo_row < d, pltpu.roll(tail, d, 0), rolled[:HALO])
            delayed = jnp.concatenate([top, rolled[HALO:]], axis=0)
            out = out + cw_ref[CONV_WIDTH - 1 - d:CONV_WIDTH - d, c0:c0 + FC] * delayed
        return out

    for c in range(ffn_dim // FC):
        gate = conv_cols(c * FC)
        up = conv_cols(ffn_dim + c * FC)
        act_ref[:, c * FC:(c + 1) * FC] = (gate * jax.nn.sigmoid(gate) * up).astype(BF16)
    f = jnp.dot(act_ref[...], wdn_ref[...], preferred_element_type=F32)
    o_ref[0] = _layer_norm(alpha * h + f, g_ref[...], b_ref[...])


def _ffn(h, w_up, conv_w, conv_b, w_down, ln_g, ln_b, alpha, tile):
    B, S, D = h.shape
    F2 = w_up.shape[1]
    const = lambda *shape: pl.BlockSpec(shape, lambda b, t: (0,) * len(shape))
    row = pl.BlockSpec((1, tile, D), lambda b, t: (b, t, 0))
    return pl.pallas_call(
        functools.partial(_ffn_kernel, alpha=alpha, ffn_dim=F2 // 2),
        out_shape=jax.ShapeDtypeStruct((B, S, D), F32),
        grid=(B, S // tile),
        in_specs=[row, const(D, F2), const(CONV_WIDTH, F2), const(1, F2), const(F2 // 2, D),
                  const(1, D), const(1, D)],
        out_specs=row,
        scratch_shapes=[
            pltpu.VMEM((tile, F2 // 2), BF16),
            pltpu.VMEM((SUBLANES, F2), F32),
        ],
        compiler_params=_params("parallel", "arbitrary"),
        name="ffn",
    )(h, w_up.astype(BF16), conv_w.astype(F32), conv_b.astype(F32).reshape(1, F2),
      w_down.astype(BF16), ln_g.astype(F32).reshape(1, D), ln_b.astype(F32).reshape(1, D))


PROJ_TILE = 512
S5_STEPS = 64
MERGE_TILE = 512
FFN_TILE = 512


def kernel(x, mem, positions, w_in, w_gate, b_gate, s5_lam_re, s5_lam_im, s5_log_dt, s5_b_re, s5_b_im, s5_c_re, s5_c_im, s5_d, w_glu, b_glu, w_mem_kv, w_proj_a, w_proj_b, w_proj_c, w_out, ln1_g, ln1_b, w_up, conv_w, conv_b, w_down, ln2_g, ln2_b):
    B, S, D = x.shape
    depth = w_in.shape[0]
    alpha = (2.0 * depth) ** 0.25
    h = x
    for l in range(depth):
        qT, k, vT, qiT, ki, wiT, u, qm = _proj(h, positions, w_in[l], min(PROJ_TILE, S))
        att = _dsa(qT, qiT, wiT, ki, k, vT)
        lbre, lbim, wb, wc = _s5_params(s5_lam_re[l], s5_lam_im[l], s5_log_dt[l], s5_b_re[l],
                                        s5_b_im[l], s5_c_re[l], s5_c_im[l])
        yb = _s5(u, lbre, lbim, wb, wc, s5_d[l], w_glu[l], b_glu[l], B, min(S5_STEPS, S))
        h = _merge(h, att, yb, qm, mem, w_mem_kv[l], w_gate[l], b_gate[l], w_proj_a[l],
                   w_proj_b[l], w_proj_c[l], w_out[l], ln1_g[l], ln1_b[l], alpha, min(MERGE_TILE, S))
        h = _ffn(h, w_up[l], conv_w[l], conv_b[l], w_down[l], ln2_g[l], ln2_b[l], alpha,
                 min(FFN_TILE, S))
    return h
```

```python
import functools
import math

import jax
import jax.numpy as jnp
from jax import lax
from jax.experimental import pallas as pl
from jax.experimental.pallas import tpu as pltpu

F32 = jnp.float32
BF16 = jnp.bfloat16
I32 = jnp.int32

CHUNK = 64
ATT_HEADS = 8
ATT_KV_HEADS = 2
ATT_HEAD_DIM = 64
ATT_REP = ATT_HEADS // ATT_KV_HEADS
IDX_HEADS = 4
IDX_DIM = 64
TOPK_MAX = 256
ROPE_THETA = 10000.0
S5_WIDTH = 512
S5_GROUP = 16
S5_GROUPS = S5_WIDTH // S5_GROUP
S5_STATE = 64
MEM_HEADS = 4
MEM_HEAD_DIM = 128
CONV_WIDTH = 3
N_BRANCH = 3
LN_EPS = 1e-5
NEG = -1e30
INT_MIN = -(2 ** 31)
LOG2E = math.log2(math.e)

LANES = 128
SUBLANES = 8
VMEM_LIMIT = 56 * 1024 * 1024

_Q0 = 0
_K0 = _Q0 + ATT_HEADS * ATT_HEAD_DIM
_V0 = _K0 + ATT_KV_HEADS * ATT_HEAD_DIM
_QI0 = _V0 + ATT_KV_HEADS * ATT_HEAD_DIM
_KI0 = _QI0 + IDX_HEADS * IDX_DIM
_WI0 = _KI0 + IDX_DIM
_U0 = _WI0 + IDX_HEADS
_QM0 = _U0 + S5_WIDTH
_END = _QM0 + MEM_HEADS * MEM_HEAD_DIM
_PAD = LANES - IDX_DIM - IDX_HEADS
_PU0 = _KI0 + LANES
_PQM0 = _PU0 + S5_WIDTH
_PEND = _PQM0 + MEM_HEADS * MEM_HEAD_DIM
V_ROWS = ATT_HEAD_DIM + 16


def _params(*sem):
    return pltpu.CompilerParams(dimension_semantics=sem, vmem_limit_bytes=VMEM_LIMIT)


def _proj_kernel(x_ref, pos_ref, invf_ref, w_ref,
                 qT_ref, k_ref, vT_ref, qiT_ref, ki_ref, wiT_ref, u_ref, qm_ref):
    T = x_ref.shape[1]
    y = jnp.dot(x_ref[0].astype(BF16), w_ref[...], preferred_element_type=F32)

    ang = pos_ref[0].astype(F32) * invf_ref[...]
    cos = jnp.cos(ang)
    sin = jnp.sin(ang)
    lane = lax.broadcasted_iota(I32, (T, LANES), 1)
    first = (lane % ATT_HEAD_DIM) < (ATT_HEAD_DIM // 2)
    sin = jnp.where(first, -sin, sin)

    def rope(z):
        partner = jnp.where(first, pltpu.roll(z, LANES - 32, 1), pltpu.roll(z, 32, 1))
        return z * cos + partner * sin

    att_scale = ATT_HEAD_DIM ** -0.5 * LOG2E
    for c in range(ATT_HEADS // 2):
        z = rope(y[:, _Q0 + c * LANES:_Q0 + (c + 1) * LANES]) * att_scale
        qT_ref[0, c * LANES:(c + 1) * LANES, :] = z.T.astype(BF16)
    k_ref[0] = rope(y[:, _K0:_K0 + LANES]).astype(BF16)
    vT = y[:, _V0:_V0 + LANES].T.astype(BF16)
    for g in range(ATT_KV_HEADS):
        vT_ref[0, g * V_ROWS:g * V_ROWS + ATT_HEAD_DIM, :] = vT[g * ATT_HEAD_DIM:(g + 1) * ATT_HEAD_DIM]
        vT_ref[0, g * V_ROWS + ATT_HEAD_DIM:(g + 1) * V_ROWS, :] = jnp.ones((V_ROWS - ATT_HEAD_DIM, T), BF16)
    idx_scale = IDX_DIM ** -0.5
    for c in range(IDX_HEADS // 2):
        z = rope(y[:, _QI0 + c * LANES:_QI0 + (c + 1) * LANES]) * idx_scale
        qiT_ref[0, c * LANES:(c + 1) * LANES, :] = z.T.astype(BF16)
    kw = y[:, _KI0:_KI0 + LANES]
    ki_ref[0] = jnp.where(lane < IDX_DIM, rope(kw), 0.0).astype(BF16)
    wiT_ref[0] = kw.T[IDX_DIM:IDX_DIM + IDX_HEADS] * (IDX_HEADS ** -0.5)
    u_ref[...] = y[:, _PU0:_PU0 + S5_WIDTH]
    qm_ref[0] = (y[:, _PQM0:_PEND] * (MEM_HEAD_DIM ** -0.5)).astype(BF16)


def _proj(x, positions, w_in, tile):
    B, S, D = x.shape
    nt = S // tile
    w_pad = jnp.concatenate(
        [w_in[:, :_U0], jnp.zeros((D, _PAD), w_in.dtype), w_in[:, _U0:]], axis=1).astype(BF16)
    half = ATT_HEAD_DIM // 2
    inv_freq = ROPE_THETA ** (-jnp.arange(half, dtype=F32) / half)
    invf = jnp.tile(inv_freq, LANES // half)[None, :]
    pos3 = positions.reshape(B, S, 1)
    QW = ATT_HEADS * ATT_HEAD_DIM
    IW = IDX_HEADS * IDX_DIM
    out_shape = (
        jax.ShapeDtypeStruct((B, QW, S), BF16),
        jax.ShapeDtypeStruct((B, S, LANES), BF16),
        jax.ShapeDtypeStruct((B, ATT_KV_HEADS * V_ROWS, S), BF16),
        jax.ShapeDtypeStruct((B, IW, S), BF16),
        jax.ShapeDtypeStruct((B, S, LANES), BF16),
        jax.ShapeDtypeStruct((B, IDX_HEADS, S), F32),
        jax.ShapeDtypeStruct((S, B * S5_WIDTH), F32),
        jax.ShapeDtypeStruct((B, S, MEM_HEADS * MEM_HEAD_DIM), BF16),
    )
    rows = lambda width: pl.BlockSpec((1, tile, width), lambda b, t: (b, t, 0))
    cols = lambda height: pl.BlockSpec((1, height, tile), lambda b, t: (b, 0, t))
    out_specs = (
        cols(QW), rows(LANES), cols(ATT_KV_HEADS * V_ROWS), cols(IW), rows(LANES), cols(IDX_HEADS),
        pl.BlockSpec((tile, S5_WIDTH), lambda b, t: (t, b)),
        rows(MEM_HEADS * MEM_HEAD_DIM),
    )
    return pl.pallas_call(
        _proj_kernel,
        out_shape=out_shape,
        grid=(B, nt),
        in_specs=[
            pl.BlockSpec((1, tile, D), lambda b, t: (b, t, 0)),
            pl.BlockSpec((1, tile, 1), lambda b, t: (b, t, 0)),
            pl.BlockSpec((1, LANES), lambda b, t: (0, 0)),
            pl.BlockSpec((D, _PEND), lambda b, t: (0, 0)),
        ],
        out_specs=out_specs,
        compiler_params=_params("parallel", "parallel"),
        name="proj",
    )(x, pos3, invf, w_pad)


DSA_TQ = 256
DSA_TK = 256
PLANE_KEYS = 32 * SUBLANES
FCOUNT_ROWS = 4 * SUBLANES


def _bit_transpose32(words):
    a = list(words)
    j, mask = 16, 0x0000FFFF
    while j:
        k = 0
        while k < 32:
            t = (a[k] ^ lax.shift_right_logical(a[k + j], jnp.int32(j))) & jnp.int32(mask)
            a[k] = a[k] ^ t
            a[k + j] = a[k + j] ^ lax.shift_left(t, jnp.int32(j))
            k = (k + j + 1) & ~j
        j >>= 1
        mask = (mask ^ (mask << j)) & 0xFFFFFFFF
    return a


def _dsa_kernel(qT_ref, qiT_ref, wiT_ref, ki_ref, k_ref, vT_ref, o_ref,
                wq_ref, wqi_ref, sc_ref, planes_ref, bias_ref, m_ref, acc_ref, alpha_ref,
                s_ref, p_ref, thr_ref, cnt_ref,
                *, topk):
    TQ, TK = DSA_TQ, DSA_TK
    HD = ATT_HEAD_DIM
    i = pl.program_id(1)
    n_kt = (i * TQ + TQ + TK - 1) // TK
    key_in_tile = lax.broadcasted_iota(I32, (TK, TQ), 0)
    qpos = i * TQ + lax.broadcasted_iota(I32, (1, TQ), 1)
    limit = (qpos // CHUNK + 1) * CHUNK
    wi = wiT_ref[0]

    def tile_start(kt):
        return pl.multiple_of(kt * TK, TK)

    zeros = jnp.zeros((HD, TQ), BF16)
    for h in range(ATT_HEADS):
        qh = qT_ref[0, h * HD:(h + 1) * HD, :]
        g = h // ATT_REP
        wq_ref[h] = jnp.concatenate([zeros] * g + [qh] + [zeros] * (ATT_KV_HEADS - 1 - g), axis=0)
    for h in range(IDX_HEADS):
        wqi_ref[h] = jnp.concatenate([qiT_ref[0, h * IDX_DIM:(h + 1) * IDX_DIM, :], zeros], axis=0)

    @pl.when((pl.program_id(0) == 0) & (i == 0))
    def _():
        planes_ref[...] = jnp.zeros(planes_ref.shape, I32)

    def score_tile(kt):
        ks = tile_start(jnp.minimum(kt, n_kt - 1))
        kk = ki_ref[0, pl.ds(ks, TK), :]
        acc = jnp.zeros((TK, TQ), F32)
        for h in range(IDX_HEADS):
            logit = jnp.dot(kk, wqi_ref[h], preferred_element_type=F32)
            acc = acc + wi[h:h + 1, :] * jnp.maximum(logit, 0.0)
        sc_ref[pl.ds(ks, TK), :] = jnp.where(ks + key_in_tile < limit, acc, NEG)

    def score_pair(j, carry):
        score_tile(2 * j)
        score_tile(2 * j + 1)
        return carry

    lax.fori_loop(0, (n_kt + 1) // 2, score_pair, 0)

    def plane_tile(kt, carry):
        bits = lax.bitcast_convert_type(sc_ref[pl.ds(tile_start(kt), TK), :], I32)
        ukey = jnp.where(bits < 0, -bits, bits ^ jnp.int32(INT_MIN))
        for g in range(TK // PLANE_KEYS):
            row0 = pl.multiple_of(kt * (TK // 32) + g * SUBLANES, SUBLANES)
            for c in range(TQ // LANES):
                rows = ukey[g * PLANE_KEYS:(g + 1) * PLANE_KEYS, c * LANES:(c + 1) * LANES]
                words = _bit_transpose32([rows[j * SUBLANES:(j + 1) * SUBLANES] for j in range(32)])
                for b in range(32):
                    planes_ref[b, pl.ds(row0, SUBLANES), c * LANES:(c + 1) * LANES] = words[31 - b]
        return carry

    lax.fori_loop(0, n_kt, plane_tile, 0)

    n_words = planes_ref.shape[1]
    word_row = lax.broadcasted_iota(I32, (n_words, TQ), 0)
    cand0 = jnp.where(word_row < n_kt * (TK // 32), jnp.int32(-1), jnp.int32(0))

    def popcount_rows(w):
        return jnp.sum(lax.population_count(w), axis=0, keepdims=True)

    def bit_body(it, carry):
        cand, need, ukey_thr = carry
        b = 31 - it
        ones = cand & planes_ref[b]
        c1 = popcount_rows(ones)
        ok = c1 >= need
        cand = jnp.where(ok, ones, cand ^ ones)
        need = jnp.where(ok, need, need - c1)
        ukey_thr = ukey_thr | jnp.left_shift(ok.astype(I32), b)
        return cand, need, ukey_thr

    cand, need, ukey_thr = lax.fori_loop(
        0, 32, bit_body, (cand0, jnp.full((1, TQ), topk, I32), jnp.zeros((1, TQ), I32)))
    thr_key = ukey_thr ^ jnp.int32(INT_MIN)
    cnt_key = (topk - need) + popcount_rows(cand)

    def key_to_float(key):
        bits = jnp.where(key < 0, jnp.int32(INT_MIN) - key, key)
        return lax.bitcast_convert_type(bits, F32)

    def count(pred):
        def body(kt, cnt):
            m = jnp.where(pred(sc_ref[pl.ds(tile_start(kt), TK), :]), 1.0, 0.0)
            return cnt + m.reshape(TK // FCOUNT_ROWS, FCOUNT_ROWS, TQ).sum(axis=0)
        cnt = lax.fori_loop(0, n_kt, body, jnp.zeros((FCOUNT_ROWS, TQ), F32))
        return jnp.sum(cnt, axis=0, keepdims=True)

    thr_ref[...] = jnp.where(thr_key == INT_MIN, -jnp.inf, key_to_float(thr_key))
    cnt_ref[...] = cnt_key.astype(F32)
    recount = count(lambda s: s >= thr_ref[...])
    disagree = jnp.max(jnp.abs(recount - cnt_ref[...])) > 0.0

    @pl.when(disagree)
    def _():
        def body(it, carry):
            lo, cnt_lo = carry
            cand = lo + jnp.left_shift(jnp.int32(1), 31 - it)
            cand_f = key_to_float(cand)
            c = count(lambda s: s >= cand_f)
            ok = c >= float(topk)
            return jnp.where(ok, cand, lo), jnp.where(ok, c, cnt_lo)
        lo, cnt_lo = lax.fori_loop(0, 32, body, (jnp.full((1, TQ), INT_MIN, I32),
                                                 jnp.full((1, TQ), n_kt * TK, I32).astype(F32)))
        thr_ref[...] = jnp.where(lo == INT_MIN, -jnp.inf, key_to_float(lo))
        cnt_ref[...] = cnt_lo

    thr = thr_ref[...]
    has_ties = jnp.max(cnt_ref[...]) > float(topk)

    @pl.when(jnp.logical_not(has_ties))
    def _():
        def bias_tile(kt, carry):
            ks = tile_start(kt)
            sel = (sc_ref[pl.ds(ks, TK), :] >= thr) & (ks + key_in_tile < limit)
            bias_ref[pl.ds(ks, TK), :] = jnp.where(sel, 0.0, NEG).astype(BF16)
            return carry
        lax.fori_loop(0, n_kt, bias_tile, 0)

    @pl.when(has_ties)
    def _():
        need = float(topk) - count(lambda s: s > thr)
        ri = lax.broadcasted_iota(I32, (TK, TK), 0)
        ci = lax.broadcasted_iota(I32, (TK, TK), 1)
        tri = jnp.where(ci < ri, 1.0, 0.0).astype(BF16)

        def bias_tile(kt, run):
            ks = tile_start(kt)
            s = sc_ref[pl.ds(ks, TK), :]
            eq = jnp.where(s == thr, 1.0, 0.0)
            before = jnp.dot(tri, eq.astype(BF16), preferred_element_type=F32) + run
            sel = (s > thr) | ((s == thr) & (before < need))
            sel = sel & (ks + key_in_tile < limit)
            bias_ref[pl.ds(ks, TK), :] = jnp.where(sel, 0.0, NEG).astype(BF16)
            return run + jnp.sum(eq, axis=0, keepdims=True)
        lax.fori_loop(0, n_kt, bias_tile, jnp.zeros((1, TQ), F32))

    m_ref[...] = jnp.full(m_ref.shape, -3e38, F32)
    acc_ref[...] = jnp.zeros(acc_ref.shape, F32)

    def scores(kt, slot):
        kk = k_ref[0, pl.ds(tile_start(jnp.minimum(kt, n_kt - 1)), TK), :]
        bias = bias_ref[pl.ds(tile_start(jnp.minimum(kt, n_kt)), TK), :]
        for h in range(ATT_HEADS):
            s_ref[slot, h] = jnp.dot(kk, wq_ref[h], preferred_element_type=F32).astype(BF16) + bias

    def softmax(slot):
        for h in range(ATT_HEADS):
            for c in range(TQ // LANES):
                ql = slice(c * LANES, (c + 1) * LANES)
                s = s_ref[slot, h, :, ql]
                m_old = m_ref[h:h + 1, ql]
                m_new = jnp.maximum(m_old, jnp.max(s, axis=0, keepdims=True).astype(F32))
                p_ref[slot, h, :, ql] = jnp.exp2(s - m_new.astype(BF16))
                alpha_ref[slot, h:h + 1, ql] = jnp.exp2(m_old - m_new)
                m_ref[h:h + 1, ql] = m_new

    def weighted_values(kt, slot):
        ks = tile_start(jnp.clip(kt, 0, n_kt - 1))
        for h in range(ATT_HEADS):
            g = h // ATT_REP
            pv = jnp.dot(vT_ref[0, g * V_ROWS:(g + 1) * V_ROWS, pl.ds(ks, TK)], p_ref[slot, h],
                         preferred_element_type=F32)
            acc_ref[h] = alpha_ref[slot, h:h + 1, :] * acc_ref[h] + pv

    bias_ref[pl.ds(tile_start(n_kt), TK), :] = jnp.full((TK, TQ), NEG, BF16)
    scores(0, 0)
    p_ref[1] = jnp.zeros(p_ref.shape[1:], BF16)
    alpha_ref[1] = jnp.ones(alpha_ref.shape[1:], F32)

    def att_pair(j, carry):
        scores(2 * j + 1, 1)
        softmax(0)
        weighted_values(2 * j - 1, 1)
        scores(2 * j + 2, 0)
        softmax(1)
        weighted_values(2 * j, 0)
        return carry

    n_pairs = (n_kt + 1) // 2
    lax.fori_loop(0, n_pairs, att_pair, 0)
    weighted_values(2 * n_pairs - 1, 1)

    for c in range(ATT_HEADS // 2):
        pair = [acc_ref[h, :HD, :] / acc_ref[h, HD:HD + 1, :] for h in (2 * c, 2 * c + 1)]
        o_ref[0, :, c * LANES:(c + 1) * LANES] = jnp.concatenate(pair, axis=0).T.astype(BF16)


def _dsa(qT, qiT, wiT, ki, k, vT):
    B, QW, S = qT.shape
    TQ = DSA_TQ
    topk = min(TOPK_MAX, S // 4)
    assert S % TQ == 0 and TQ == DSA_TK and DSA_TK >= topk
    return pl.pallas_call(
        functools.partial(_dsa_kernel, topk=topk),
        out_shape=jax.ShapeDtypeStruct((B, S, QW), BF16),
        grid=(B, S // TQ),
        in_specs=[
            pl.BlockSpec((1, QW, TQ), lambda b, i: (b, 0, i)),
            pl.BlockSpec((1, IDX_HEADS * IDX_DIM, TQ), lambda b, i: (b, 0, i)),
            pl.BlockSpec((1, IDX_HEADS, TQ), lambda b, i: (b, 0, i)),
            pl.BlockSpec((1, S, LANES), lambda b, i: (b, 0, 0)),
            pl.BlockSpec((1, S, LANES), lambda b, i: (b, 0, 0)),
            pl.BlockSpec((1, ATT_KV_HEADS * V_ROWS, S), lambda b, i: (b, 0, 0)),
        ],
        out_specs=pl.BlockSpec((1, TQ, QW), lambda b, i: (b, i, 0)),
        scratch_shapes=[
            pltpu.VMEM((ATT_HEADS, LANES, TQ), BF16),
            pltpu.VMEM((IDX_HEADS, LANES, TQ), BF16),
            pltpu.VMEM((S, TQ), F32),
            pltpu.VMEM((32, S // 32, TQ), I32),
            pltpu.VMEM((S + DSA_TK, TQ), BF16),
            pltpu.VMEM((ATT_HEADS, TQ), F32),
            pltpu.VMEM((ATT_HEADS, V_ROWS, TQ), F32),
            pltpu.VMEM((2, ATT_HEADS, TQ), F32),
            pltpu.VMEM((2, ATT_HEADS, DSA_TK, TQ), BF16),
            pltpu.VMEM((2, ATT_HEADS, DSA_TK, TQ), BF16),
            pltpu.VMEM((1, TQ), F32),
            pltpu.VMEM((1, TQ), F32),
        ],
        compiler_params=_params("arbitrary", "arbitrary"),
        name="dsa",
    )(qT, qiT, wiT, ki, k, vT)


S5_SLABS = S5_WIDTH // LANES
S5_SLAB_STATES = (LANES // S5_GROUP) * S5_STATE


def _s5_param_kernel(lre_ref, lim_ref, ldt_ref, bre_ref, bim_ref,
                     lbre_ref, lbim_ref, bbre_ref, bbim_ref):
    lre = jnp.minimum(lre_ref[...], -1e-4)
    lim = lim_ref[...]
    dt = jnp.exp(ldt_ref[...])
    mag = jnp.exp(lre * dt)
    lbre = mag * jnp.cos(lim * dt)
    lbim = mag * jnp.sin(lim * dt)
    nre = lbre - 1.0
    den = lre * lre + lim * lim
    cre = (nre * lre + lbim * lim) / den
    cim = (lbim * lre - nre * lim) / den
    lbre_ref[...] = lbre
    lbim_ref[...] = lbim
    bbre_ref[...] = cre * bre_ref[...] - cim * bim_ref[...]
    bbim_ref[...] = cre * bim_ref[...] + cim * bre_ref[...]


def _s5_params(lam_re, lam_im, log_dt, b_re, b_im, c_re, c_im):
    G, P, H = S5_GROUPS, S5_STATE, S5_GROUP
    n = G * P
    flat = lambda a: a.astype(F32).reshape(1, n)
    ldt = jnp.repeat(log_dt.astype(F32), P).reshape(1, n)
    bt = lambda a: a.astype(F32).transpose(2, 0, 1).reshape(H, n)
    lbre, lbim, bbre, bbim = pl.pallas_call(
        _s5_param_kernel,
        out_shape=(jax.ShapeDtypeStruct((1, n), F32), jax.ShapeDtypeStruct((1, n), F32),
                   jax.ShapeDtypeStruct((H, n), F32), jax.ShapeDtypeStruct((H, n), F32)),
        name="s5_params",
    )(flat(lam_re), flat(lam_im), ldt, bt(b_re), bt(b_im))

    gl = LANES // H
    eye = jnp.eye(gl, dtype=F32)
    bb = jnp.stack([bbre, bbim]).reshape(2, H, S5_SLABS, gl, P)
    wb = jnp.einsum("ahjgp,gk->jghakp", bb, eye).reshape(S5_SLABS, LANES, 2 * gl * P)
    cc = jnp.stack([c_re.astype(F32), -c_im.astype(F32)]).reshape(2, S5_SLABS, gl, H, P)
    wc = jnp.einsum("ajghp,gk->jagpkh", cc, eye).reshape(S5_SLABS, 2 * gl * P, LANES)
    lam = lambda a: jnp.broadcast_to(a.reshape(S5_SLABS, 1, gl * P), (S5_SLABS, SUBLANES, gl * P))
    return lam(lbre), lam(lbim), wb.astype(BF16), wc.astype(BF16)


def _s5_kernel(u_ref, lbre_ref, lbim_ref, wb_ref, wc_ref, d_ref, wglu_ref, bglu_ref, o_ref,
               state_ref, h_ref, y_ref, u_sc, *, steps, batch):
    NS = S5_SLAB_STATES
    t = pl.program_id(0)

    @pl.when(t == 0)
    def _():
        state_ref[...] = jnp.zeros(state_ref.shape, F32)

    W = S5_WIDTH
    for b in range(batch):
        for j in range(S5_SLABS):
            c0 = b * W + j * LANES
            u_sc[j, pl.ds(b, steps, stride=batch), :] = u_ref[:, c0:c0 + LANES]
    for j in range(S5_SLABS):
        h_ref[j] = jnp.dot(u_sc[j].astype(BF16), wb_ref[j], preferred_element_type=F32)

    for j in range(S5_SLABS):
        lre = lbre_ref[j]
        lim = lbim_ref[j]

        def step(s, carry):
            hre, him = carry
            r0 = pl.multiple_of(s * batch, batch)
            nre = lre * hre - lim * him + h_ref[j, pl.ds(r0, batch), :NS]
            nim = lre * him + lim * hre + h_ref[j, pl.ds(r0, batch), NS:]
            h_ref[j, pl.ds(r0, batch), :NS] = nre
            h_ref[j, pl.ds(r0, batch), NS:] = nim
            return nre, nim

        hre, him = lax.fori_loop(0, steps, step, (state_ref[j, :, :NS], state_ref[j, :, NS:]),
                                 unroll=4)
        state_ref[j, :, :NS] = hre
        state_ref[j, :, NS:] = him

    ys = [jnp.dot(h_ref[j].astype(BF16), wc_ref[j], preferred_element_type=F32)
          for j in range(S5_SLABS)]
    u = jnp.concatenate([u_sc[j] for j in range(S5_SLABS)], axis=1)
    y = jnp.concatenate(ys, axis=1) + d_ref[...] * u
    y = jax.nn.gelu(y)
    gate = jnp.dot(y.astype(BF16), wglu_ref[...], preferred_element_type=F32) + bglu_ref[...]
    y = y * jax.nn.sigmoid(gate)
    for j in range(S5_SLABS):
        y_ref[j] = y[:, j * LANES:(j + 1) * LANES]
    for b in range(batch):
        for j in range(S5_SLABS):
            o_ref[b, :, j * LANES:(j + 1) * LANES] = (
                y_ref[j, pl.ds(b, steps, stride=batch), :].astype(BF16))


def _s5(u2, lbre, lbim, wb, wc, d_skip, w_glu, b_glu, batch, steps):
    S, W = u2.shape[0], S5_WIDTH
    NS2 = 2 * S5_SLAB_STATES
    const = lambda *shape: pl.BlockSpec(shape, lambda t: (0,) * len(shape))
    return pl.pallas_call(
        functools.partial(_s5_kernel, steps=steps, batch=batch),
        out_shape=jax.ShapeDtypeStruct((batch, S, W), BF16),
        grid=(S // steps,),
        in_specs=[
            pl.BlockSpec((steps, batch * W), lambda t: (t, 0)),
            const(S5_SLABS, SUBLANES, S5_SLAB_STATES),
            const(S5_SLABS, SUBLANES, S5_SLAB_STATES),
            const(S5_SLABS, LANES, NS2),
            const(S5_SLABS, NS2, LANES),
            const(1, W),
            const(W, W),
            const(1, W),
        ],
        out_specs=pl.BlockSpec((batch, steps, W), lambda t: (0, t, 0)),
        scratch_shapes=[
            pltpu.VMEM((S5_SLABS, batch, NS2), F32),
            pltpu.VMEM((S5_SLABS, steps * batch, NS2), F32),
            pltpu.VMEM((S5_SLABS, steps * batch, LANES), F32),
            pltpu.VMEM((S5_SLABS, steps * batch, LANES), F32),
        ],
        compiler_params=_params("arbitrary"),
        name="s5",
    )(u2, lbre, lbim, wb, wc, d_skip.astype(F32).reshape(1, W), w_glu.astype(BF16),
      b_glu.astype(F32).reshape(1, W))


def _layer_norm(z, g, b):
    mu = jnp.mean(z, axis=-1, keepdims=True)
    zc = z - mu
    var = jnp.mean(zc * zc, axis=-1, keepdims=True)
    return zc * lax.rsqrt(var + LN_EPS) * g + b


def _merge_kernel(x_ref, a_ref, yb_ref, qm_ref, mem_ref, wkv_ref, wg_ref, bg_ref,
                  wpa_ref, wpb_ref, wpc_ref, wout_ref, g_ref, b_ref, o_ref, kv_ref, *, alpha):
    D = x_ref.shape[2]
    HW = MEM_HEADS * MEM_HEAD_DIM

    @pl.when(pl.program_id(1) == 0)
    def _():
        kv_ref[...] = jnp.dot(mem_ref[0].astype(BF16), wkv_ref[...],
                              preferred_element_type=F32).astype(BF16)

    x = x_ref[0]
    xb = x.astype(BF16)
    qm = qm_ref[0]
    heads = []
    for h in range(MEM_HEADS):
        sl = slice(h * MEM_HEAD_DIM, (h + 1) * MEM_HEAD_DIM)
        s = lax.dot_general(qm[:, sl], kv_ref[:, sl], (((1,), (1,)), ((), ())),
                            preferred_element_type=F32)
        p = jnp.exp(s - jnp.max(s, axis=1, keepdims=True))
        p = p / jnp.sum(p, axis=1, keepdims=True)
        heads.append(jnp.dot(p.astype(BF16), kv_ref[:, HW + h * MEM_HEAD_DIM:HW + (h + 1) * MEM_HEAD_DIM],
                             preferred_element_type=F32))
    c_in = jnp.concatenate(heads, axis=1).astype(BF16)

    y_a = jnp.dot(a_ref[0], wpa_ref[...], preferred_element_type=F32)
    y_b = jnp.dot(yb_ref[0], wpb_ref[...], preferred_element_type=F32)
    y_c = jnp.dot(c_in, wpc_ref[...], preferred_element_type=F32)
    merged = None
    for k, yk in enumerate((y_a, y_b, y_c)):
        gk = jax.nn.sigmoid(jnp.dot(xb, wg_ref[:, k * D:(k + 1) * D], preferred_element_type=F32)
                            + bg_ref[:, k * D:(k + 1) * D])
        merged = gk * yk if merged is None else merged + gk * yk
    mix = jnp.dot(merged.astype(BF16), wout_ref[...], preferred_element_type=F32)
    o_ref[0] = _layer_norm(alpha * x + mix, g_ref[...], b_ref[...])


def _merge(x, att, yb, qm, mem, w_mem_kv, w_gate, b_gate, w_proj_a, w_proj_b, w_proj_c, w_out,
           ln_g, ln_b, alpha, tile):
    B, S, D = x.shape
    n_mem = mem.shape[1]
    HW = MEM_HEADS * MEM_HEAD_DIM
    const = lambda *shape: pl.BlockSpec(shape, lambda b, t: (0,) * len(shape))
    row = lambda width: pl.BlockSpec((1, tile, width), lambda b, t: (b, t, 0))
    return pl.pallas_call(
        functools.partial(_merge_kernel, alpha=alpha),
        out_shape=jax.ShapeDtypeStruct((B, S, D), F32),
        grid=(B, S // tile),
        in_specs=[
            row(D), row(att.shape[2]), row(yb.shape[2]), row(HW),
            pl.BlockSpec((1, n_mem, D), lambda b, t: (b, 0, 0)),
            const(D, 2 * HW), const(D, N_BRANCH * D), const(1, N_BRANCH * D),
            const(att.shape[2], D), const(yb.shape[2], D), const(HW, D), const(D, D),
            const(1, D), const(1, D),
        ],
        out_specs=row(D),
        scratch_shapes=[pltpu.VMEM((n_mem, 2 * HW), BF16)],
        compiler_params=_params("parallel", "arbitrary"),
        name="merge",
    )(x, att, yb, qm, mem, w_mem_kv.astype(BF16), w_gate.astype(BF16),
      b_gate.astype(F32).reshape(1, -1), w_proj_a.astype(BF16), w_proj_b.astype(BF16),
      w_proj_c.astype(BF16), w_out.astype(BF16), ln_g.astype(F32).reshape(1, D),
      ln_b.astype(F32).reshape(1, D))


FFN_CHUNK = 256


def _ffn_kernel(h_ref, wup_ref, cw_ref, cb_ref, wdn_ref, g_ref, b_ref, o_ref,
                act_ref, tail_ref, *, alpha, ffn_dim):
    T = h_ref.shape[1]
    FC = FFN_CHUNK
    HALO = SUBLANES

    @pl.when(pl.program_id(1) == 0)
    def _():
        tail_ref[...] = jnp.zeros(tail_ref.shape, F32)

    h = h_ref[0]
    hb = h.astype(BF16)
    halo_row = lax.broadcasted_iota(I32, (HALO, FC), 0)

    def conv_cols(c0):
        up = jnp.dot(hb, wup_ref[:, c0:c0 + FC], preferred_element_type=F32)
        tail = tail_ref[:, c0:c0 + FC]
        tail_ref[:, c0:c0 + FC] = up[T - HALO:]
        out = cb_ref[:, c0:c0 + FC] + cw_ref[CONV_WIDTH - 1:CONV_WIDTH, c0:c0 + FC] * up
        for d in range(1, CONV_WIDTH):
            rolled = pltpu.roll(up, d, 0)
            top = jnp.where(halo_row < d, pltpu.roll(tail, d, 0), rolled[:HALO])
            delayed = jnp.concatenate([top, rolled[HALO:]], axis=0)
            out = out + cw_ref[CONV_WIDTH - 1 - d:CONV_WIDTH - d, c0:c0 + FC] * delayed
        return out

    for c in range(ffn_dim // FC):
        gate = conv_cols(c * FC)
        up = conv_cols(ffn_dim + c * FC)
        act_ref[:, c * FC:(c + 1) * FC] = (gate * jax.nn.sigmoid(gate) * up).astype(BF16)
    f = jnp.dot(act_ref[...], wdn_ref[...], preferred_element_type=F32)
    o_ref[0] = _layer_norm(alpha * h + f, g_ref[...], b_ref[...])


def _ffn(h, w_up, conv_w, conv_b, w_down, ln_g, ln_b, alpha, tile):
    B, S, D = h.shape
    F2 = w_up.shape[1]
    const = lambda *shape: pl.BlockSpec(shape, lambda b, t: (0,) * len(shape))
    row = pl.BlockSpec((1, tile, D), lambda b, t: (b, t, 0))
    return pl.pallas_call(
        functools.partial(_ffn_kernel, alpha=alpha, ffn_dim=F2 // 2),
        out_shape=jax.ShapeDtypeStruct((B, S, D), F32),
        grid=(B, S // tile),
        in_specs=[row, const(D, F2), const(CONV_WIDTH, F2), const(1, F2), const(F2 // 2, D),
                  const(1, D), const(1, D)],
        out_specs=row,
        scratch_shapes=[
            pltpu.VMEM((tile, F2 // 2), BF16),
            pltpu.VMEM((SUBLANES, F2), F32),
        ],
        compiler_params=_params("parallel", "arbitrary"),
        name="ffn",
    )(h, w_up.astype(BF16), conv_w.astype(F32), conv_b.astype(F32).reshape(1, F2),
      w_down.astype(BF16), ln_g.astype(F32).reshape(1, D), ln_b.astype(F32).reshape(1, D))


PROJ_TILE = 512
S5_STEPS = 64
MERGE_TILE = 512
FFN_TILE = 512


def kernel(x, mem, positions, w_in, w_gate, b_gate, s5_lam_re, s5_lam_im, s5_log_dt, s5_b_re, s5_b_im, s5_c_re, s5_c_im, s5_d, w_glu, b_glu, w_mem_kv, w_proj_a, w_proj_b, w_proj_c, w_out, ln1_g, ln1_b, w_up, conv_w, conv_b, w_down, ln2_g, ln2_b):
    B, S, D = x.shape
    depth = w_in.shape[0]
    alpha = (2.0 * depth) ** 0.25
    h = x
    for l in range(depth):
        qT, k, vT, qiT, ki, wiT, u, qm = _proj(h, positions, w_in[l], min(PROJ_TILE, S))
        att = _dsa(qT, qiT, wiT, ki, k, vT)
        lbre, lbim, wb, wc = _s5_params(s5_lam_re[l], s5_lam_im[l], s5_log_dt[l], s5_b_re[l],
                                        s5_b_im[l], s5_c_re[l], s5_c_im[l])
        yb = _s5(u, lbre, lbim, wb, wc, s5_d[l], w_glu[l], b_glu[l], B, min(S5_STEPS, S))
        h = _merge(h, att, yb, qm, mem, w_mem_kv[l], w_gate[l], b_gate[l], w_proj_a[l],
                   w_proj_b[l], w_proj_c[l], w_out[l], ln1_g[l], ln1_b[l], alpha, min(MERGE_TILE, S))
        h = _ffn(h, w_up[l], conv_w[l], conv_b[l], w_down[l], ln2_g[l], ln2_b[l], alpha,
                 min(FFN_TILE, S))
    return h
```

```python
import functools
import math

import jax
import jax.numpy as jnp
from jax import lax
from jax.experimental import pallas as pl
from jax.experimental.pallas import tpu as pltpu

F32 = jnp.float32
BF16 = jnp.bfloat16
I32 = jnp.int32

CHUNK = 64
ATT_HEADS = 8
ATT_KV_HEADS = 2
ATT_HEAD_DIM = 64
ATT_REP = ATT_HEADS // ATT_KV_HEADS
IDX_HEADS = 4
IDX_DIM = 64
TOPK_MAX = 256
ROPE_THETA = 10000.0
S5_WIDTH = 512
S5_GROUP = 16
S5_GROUPS = S5_WIDTH // S5_GROUP
S5_STATE = 64
MEM_HEADS = 4
MEM_HEAD_DIM = 128
CONV_WIDTH = 3
N_BRANCH = 3
LN_EPS = 1e-5
NEG = -1e30
INT_MIN = -(2 ** 31)
LOG2E = math.log2(math.e)

LANES = 128
SUBLANES = 8
VMEM_LIMIT = 56 * 1024 * 1024

_Q0 = 0
_K0 = _Q0 + ATT_HEADS * ATT_HEAD_DIM
_V0 = _K0 + ATT_KV_HEADS * ATT_HEAD_DIM
_QI0 = _V0 + ATT_KV_HEADS * ATT_HEAD_DIM
_KI0 = _QI0 + IDX_HEADS * IDX_DIM
_WI0 = _KI0 + IDX_DIM
_U0 = _WI0 + IDX_HEADS
_QM0 = _U0 + S5_WIDTH
_END = _QM0 + MEM_HEADS * MEM_HEAD_DIM
_PAD = LANES - IDX_DIM - IDX_HEADS
_PU0 = _KI0 + LANES
_PQM0 = _PU0 + S5_WIDTH
_PEND = _PQM0 + MEM_HEADS * MEM_HEAD_DIM
V_ROWS = ATT_HEAD_DIM + 16


def _params(*sem):
    return pltpu.CompilerParams(dimension_semantics=sem, vmem_limit_bytes=VMEM_LIMIT)


def _proj_kernel(x_ref, pos_ref, invf_ref, w_ref,
                 qT_ref, k_ref, vT_ref, qiT_ref, ki_ref, wiT_ref, u_ref, qm_ref):
    T = x_ref.shape[1]
    y = jnp.dot(x_ref[0].astype(BF16), w_ref[...], preferred_element_type=F32)

    ang = pos_ref[0].astype(F32) * invf_ref[...]
    cos = jnp.cos(ang)
    sin = jnp.sin(ang)
    lane = lax.broadcasted_iota(I32, (T, LANES), 1)
    first = (lane % ATT_HEAD_DIM) < (ATT_HEAD_DIM // 2)
    sin = jnp.where(first, -sin, sin)

    def rope(z):
        partner = jnp.where(first, pltpu.roll(z, LANES - 32, 1), pltpu.roll(z, 32, 1))
        return z * cos + partner * sin

    att_scale = ATT_HEAD_DIM ** -0.5 * LOG2E
    for c in range(ATT_HEADS // 2):
        z = rope(y[:, _Q0 + c * LANES:_Q0 + (c + 1) * LANES]) * att_scale
        qT_ref[0, c * LANES:(c + 1) * LANES, :] = z.T.astype(BF16)
    k_ref[0] = rope(y[:, _K0:_K0 + LANES]).astype(BF16)
    vT = y[:, _V0:_V0 + LANES].T.astype(BF16)
    for g in range(ATT_KV_HEADS):
        vT_ref[0, g * V_ROWS:g * V_ROWS + ATT_HEAD_DIM, :] = vT[g * ATT_HEAD_DIM:(g + 1) * ATT_HEAD_DIM]
        vT_ref[0, g * V_ROWS + ATT_HEAD_DIM:(g + 1) * V_ROWS, :] = jnp.ones((V_ROWS - ATT_HEAD_DIM, T), BF16)
    idx_scale = IDX_DIM ** -0.5
    for c in range(IDX_HEADS // 2):
        z = rope(y[:, _QI0 + c * LANES:_QI0 + (c + 1) * LANES]) * idx_scale
        qiT_ref[0, c * LANES:(c + 1) * LANES, :] = z.T.astype(BF16)
    kw = y[:, _KI0:_KI0 + LANES]
    ki_ref[0] = jnp.where(lane < IDX_DIM, rope(kw), 0.0).astype(BF16)
    wiT_ref[0] = kw.T[IDX_DIM:IDX_DIM + IDX_HEADS] * (IDX_HEADS ** -0.5)
    u_ref[...] = y[:, _PU0:_PU0 + S5_WIDTH]
    qm_ref[0] = (y[:, _PQM0:_PEND] * (MEM_HEAD_DIM ** -0.5)).astype(BF16)


def _proj(x, positions, w_in, tile):
    B, S, D = x.shape
    nt = S // tile
    w_pad = jnp.concatenate(
        [w_in[:, :_U0], jnp.zeros((D, _PAD), w_in.dtype), w_in[:, _U0:]], axis=1).astype(BF16)
    half = ATT_HEAD_DIM // 2
    inv_freq = ROPE_THETA ** (-jnp.arange(half, dtype=F32) / half)
    invf = jnp.tile(inv_freq, LANES // half)[None, :]
    pos3 = positions.reshape(B, S, 1)
    QW = ATT_HEADS * ATT_HEAD_DIM
    IW = IDX_HEADS * IDX_DIM
    out_shape = (
        jax.ShapeDtypeStruct((B, QW, S), BF16),
        jax.ShapeDtypeStruct((B, S, LANES), BF16),
        jax.ShapeDtypeStruct((B, ATT_KV_HEADS * V_ROWS, S), BF16),
        jax.ShapeDtypeStruct((B, IW, S), BF16),
        jax.ShapeDtypeStruct((B, S, LANES), BF16),
        jax.ShapeDtypeStruct((B, IDX_HEADS, S), F32),
        jax.ShapeDtypeStruct((S, B * S5_WIDTH), F32),
        jax.ShapeDtypeStruct((B, S, MEM_HEADS * MEM_HEAD_DIM), BF16),
    )
    rows = lambda width: pl.BlockSpec((1, tile, width), lambda b, t: (b, t, 0))
    cols = lambda height: pl.BlockSpec((1, height, tile), lambda b, t: (b, 0, t))
    out_specs = (
        cols(QW), rows(LANES), cols(ATT_KV_HEADS * V_ROWS), cols(IW), rows(LANES), cols(IDX_HEADS),
        pl.BlockSpec((tile, S5_WIDTH), lambda b, t: (t, b)),
        rows(MEM_HEADS * MEM_HEAD_DIM),
    )
    return pl.pallas_call(
        _proj_kernel,
        out_shape=out_shape,
        grid=(B, nt),
        in_specs=[
            pl.BlockSpec((1, tile, D), lambda b, t: (b, t, 0)),
            pl.BlockSpec((1, tile, 1), lambda b, t: (b, t, 0)),
            pl.BlockSpec((1, LANES), lambda b, t: (0, 0)),
            pl.BlockSpec((D, _PEND), lambda b, t: (0, 0)),
        ],
        out_specs=out_specs,
        compiler_params=_params("parallel", "parallel"),
        name="proj",
    )(x, pos3, invf, w_pad)


DSA_TQ = 256
DSA_TK = 256
PLANE_KEYS = 32 * SUBLANES
FCOUNT_ROWS = 4 * SUBLANES


def _bit_transpose32(words):
    a = list(words)
    j, mask = 16, 0x0000FFFF
    while j:
        k = 0
        while k < 32:
            t = (a[k] ^ lax.shift_right_logical(a[k + j], jnp.int32(j))) & jnp.int32(mask)
            a[k] = a[k] ^ t
            a[k + j] = a[k + j] ^ lax.shift_left(t, jnp.int32(j))
            k = (k + j + 1) & ~j
        j >>= 1
        mask = (mask ^ (mask << j)) & 0xFFFFFFFF
    return a


def _dsa_kernel(qT_ref, qiT_ref, wiT_ref, ki_ref, k_ref, vT_ref, o_ref,
                wq_ref, wqi_ref, sc_ref, planes_ref, bias_ref, m_ref, acc_ref, alpha_ref,
                s_ref, p_ref, thr_ref, cnt_ref,
                *, topk):
    TQ, TK = DSA_TQ, DSA_TK
    HD = ATT_HEAD_DIM
    i = pl.program_id(1)
    n_kt = (i * TQ + TQ + TK - 1) // TK
    key_in_tile = lax.broadcasted_iota(I32, (TK, TQ), 0)
    qpos = i * TQ + lax.broadcasted_iota(I32, (1, TQ), 1)
    limit = (qpos // CHUNK + 1) * CHUNK
    wi = wiT_ref[0]

    def tile_start(kt):
        return pl.multiple_of(kt * TK, TK)

    zeros = jnp.zeros((HD, TQ), BF16)
    for h in range(ATT_HEADS):
        qh = qT_ref[0, h * HD:(h + 1) * HD, :]
        g = h // ATT_REP
        wq_ref[h] = jnp.concatenate([zeros] * g + [qh] + [zeros] * (ATT_KV_HEADS - 1 - g), axis=0)
    for h in range(IDX_HEADS):
        wqi_ref[h] = jnp.concatenate([qiT_ref[0, h * IDX_DIM:(h + 1) * IDX_DIM, :], zeros], axis=0)

    @pl.when((pl.program_id(0) == 0) & (i == 0))
    def _():
        planes_ref[...] = jnp.zeros(planes_ref.shape, I32)

    def score_tile(kt):
        ks = tile_start(jnp.minimum(kt, n_kt - 1))
        kk = ki_ref[0, pl.ds(ks, TK), :]
        acc = jnp.zeros((TK, TQ), F32)
        for h in range(IDX_HEADS):
            logit = jnp.dot(kk, wqi_ref[h], preferred_element_type=F32)
            acc = acc + wi[h:h + 1, :] * jnp.maximum(logit, 0.0)
        sc_ref[pl.ds(ks, TK), :] = jnp.where(ks + key_in_tile < limit, acc, NEG)

    def score_pair(j, carry):
        score_tile(2 * j)
        score_tile(2 * j + 1)
        return carry

    lax.fori_loop(0, (n_kt + 1) // 2, score_pair, 0)

    def plane_tile(kt, carry):
        bits = lax.bitcast_convert_type(sc_ref[pl.ds(tile_start(kt), TK), :], I32)
        ukey = jnp.where(bits < 0, -bits, bits ^ jnp.int32(INT_MIN))
        for g in range(TK // PLANE_KEYS):
            row0 = pl.multiple_of(kt * (TK // 32) + g * SUBLANES, SUBLANES)
            for c in range(TQ // LANES):
                rows = ukey[g * PLANE_KEYS:(g + 1) * PLANE_KEYS, c * LANES:(c + 1) * LANES]
                words = _bit_transpose32([rows[j * SUBLANES:(j + 1) * SUBLANES] for j in range(32)])
                for b in range(32):
                    planes_ref[b, pl.ds(row0, SUBLANES), c * LANES:(c + 1) * LANES] = words[31 - b]
        return carry

    lax.fori_loop(0, n_kt, plane_tile, 0)

    n_words = planes_ref.shape[1]
    word_row = lax.broadcasted_iota(I32, (n_words, TQ), 0)
    cand0 = jnp.where(word_row < n_kt * (TK // 32), jnp.int32(-1), jnp.int32(0))

    def popcount_rows(w):
        return jnp.sum(lax.population_count(w), axis=0, keepdims=True)

    def bit_body(it, carry):
        cand, need, ukey_thr = carry
        b = 31 - it
        ones = cand & planes_ref[b]
        c1 = popcount_rows(ones)
        ok = c1 >= need
        cand = jnp.where(ok, ones, cand ^ ones)
        need = jnp.where(ok, need, need - c1)
        ukey_thr = ukey_thr | jnp.left_shift(ok.astype(I32), b)
        return cand, need, ukey_thr

    cand, need, ukey_thr = lax.fori_loop(
        0, 32, bit_body, (cand0, jnp.full((1, TQ), topk, I32), jnp.zeros((1, TQ), I32)))
    thr_key = ukey_thr ^ jnp.int32(INT_MIN)
    cnt_key = (topk - need) + popcount_rows(cand)

    def key_to_float(key):
        bits = jnp.where(key < 0, jnp.int32(INT_MIN) - key, key)
        return lax.bitcast_convert_type(bits, F32)

    def count(pred):
        def body(kt, cnt):
            m = jnp.where(pred(sc_ref[pl.ds(tile_start(kt), TK), :]), 1.0, 0.0)
            return cnt + m.reshape(TK // FCOUNT_ROWS, FCOUNT_ROWS, TQ).sum(axis=0)
        cnt = lax.fori_loop(0, n_kt, body, jnp.zeros((FCOUNT_ROWS, TQ), F32))
        return jnp.sum(cnt, axis=0, keepdims=True)

    thr_ref[...] = jnp.where(thr_key == INT_MIN, -jnp.inf, key_to_float(thr_key))
    cnt_ref[...] = cnt_key.astype(F32)
    recount = count(lambda s: s >= thr_ref[...])
    disagree = jnp.max(jnp.abs(recount - cnt_ref[...])) > 0.0

    @pl.when(disagree)
    def _():
        def body(it, carry):
            lo, cnt_lo = carry
            cand = lo + jnp.left_shift(jnp.int32(1), 31 - it)
            cand_f = key_to_float(cand)
            c = count(lambda s: s >= cand_f)
            ok = c >= float(topk)
            return jnp.where(ok, cand, lo), jnp.where(ok, c, cnt_lo)
        lo, cnt_lo = lax.fori_loop(0, 32, body, (jnp.full((1, TQ), INT_MIN, I32),
                                                 jnp.full((1, TQ), n_kt * TK, I32).astype(F32)))
        thr_ref[...] = jnp.where(lo == INT_MIN, -jnp.inf, key_to_float(lo))
        cnt_ref[...] = cnt_lo

    thr = thr_ref[...]
    has_ties = jnp.max(cnt_ref[...]) > float(topk)

    @pl.when(jnp.logical_not(has_ties))
    def _():
        def bias_tile(kt, carry):
            ks = tile_start(kt)
            sel = (sc_ref[pl.ds(ks, TK), :] >= thr) & (ks + key_in_tile < limit)
            bias_ref[pl.ds(ks, TK), :] = jnp.where(sel, 0.0, NEG)
            return carry
        lax.fori_loop(0, n_kt, bias_tile, 0)

    @pl.when(has_ties)
    def _():
        need = float(topk) - count(lambda s: s > thr)
        ri = lax.broadcasted_iota(I32, (TK, TK), 0)
        ci = lax.broadcasted_iota(I32, (TK, TK), 1)
        tri = jnp.where(ci < ri, 1.0, 0.0).astype(BF16)

        def bias_tile(kt, run):
            ks = tile_start(kt)
            s = sc_ref[pl.ds(ks, TK), :]
            eq = jnp.where(s == thr, 1.0, 0.0)
            before = jnp.dot(tri, eq.astype(BF16), preferred_element_type=F32) + run
            sel = (s > thr) | ((s == thr) & (before < need))
            sel = sel & (ks + key_in_tile < limit)
            bias_ref[pl.ds(ks, TK), :] = jnp.where(sel, 0.0, NEG)
            return run + jnp.sum(eq, axis=0, keepdims=True)
        lax.fori_loop(0, n_kt, bias_tile, jnp.zeros((1, TQ), F32))

    m_ref[...] = jnp.full(m_ref.shape, -3e38, F32)
    acc_ref[...] = jnp.zeros(acc_ref.shape, F32)

    def scores(kt, slot):
        kk = k_ref[0, pl.ds(tile_start(jnp.minimum(kt, n_kt - 1)), TK), :]
        bias = bias_ref[pl.ds(tile_start(jnp.minimum(kt, n_kt)), TK), :]
        for h in range(ATT_HEADS):
            s_ref[slot, h] = jnp.dot(kk, wq_ref[h], preferred_element_type=F32) + bias

    def softmax(slot):
        for h in range(ATT_HEADS):
            for c in range(TQ // LANES):
                ql = slice(c * LANES, (c + 1) * LANES)
                s = s_ref[slot, h, :, ql]
                m_old = m_ref[h:h + 1, ql]
                m_new = jnp.maximum(m_old, jnp.max(s, axis=0, keepdims=True))
                p_ref[slot, h, :, ql] = jnp.exp2(s - m_new).astype(BF16)
                alpha_ref[slot, h:h + 1, ql] = jnp.exp2(m_old - m_new)
                m_ref[h:h + 1, ql] = m_new

    def weighted_values(kt, slot):
        ks = tile_start(jnp.clip(kt, 0, n_kt - 1))
        for h in range(ATT_HEADS):
            g = h // ATT_REP
            pv = jnp.dot(vT_ref[0, g * V_ROWS:(g + 1) * V_ROWS, pl.ds(ks, TK)], p_ref[slot, h],
                         preferred_element_type=F32)
            acc_ref[h] = alpha_ref[slot, h:h + 1, :] * acc_ref[h] + pv

    bias_ref[pl.ds(tile_start(n_kt), TK), :] = jnp.full((TK, TQ), NEG, F32)
    scores(0, 0)
    p_ref[1] = jnp.zeros(p_ref.shape[1:], BF16)
    alpha_ref[1] = jnp.ones(alpha_ref.shape[1:], F32)

    def att_pair(j, carry):
        scores(2 * j + 1, 1)
        softmax(0)
        weighted_values(2 * j - 1, 1)
        scores(2 * j + 2, 0)
        softmax(1)
        weighted_values(2 * j, 0)
        return carry

    n_pairs = (n_kt + 1) // 2
    lax.fori_loop(0, n_pairs, att_pair, 0)
    weighted_values(2 * n_pairs - 1, 1)

    for c in range(ATT_HEADS // 2):
        pair = [acc_ref[h, :HD, :] / acc_ref[h, HD:HD + 1, :] for h in (2 * c, 2 * c + 1)]
        o_ref[0, :, c * LANES:(c + 1) * LANES] = jnp.concatenate(pair, axis=0).T.astype(BF16)


def _dsa(qT, qiT, wiT, ki, k, vT):
    B, QW, S = qT.shape
    TQ = DSA_TQ
    topk = min(TOPK_MAX, S // 4)
    assert S % TQ == 0 and TQ == DSA_TK and DSA_TK >= topk
    return pl.pallas_call(
        functools.partial(_dsa_kernel, topk=topk),
        out_shape=jax.ShapeDtypeStruct((B, S, QW), BF16),
        grid=(B, S // TQ),
        in_specs=[
            pl.BlockSpec((1, QW, TQ), lambda b, i: (b, 0, i)),
            pl.BlockSpec((1, IDX_HEADS * IDX_DIM, TQ), lambda b, i: (b, 0, i)),
            pl.BlockSpec((1, IDX_HEADS, TQ), lambda b, i: (b, 0, i)),
            pl.BlockSpec((1, S, LANES), lambda b, i: (b, 0, 0)),
            pl.BlockSpec((1, S, LANES), lambda b, i: (b, 0, 0)),
            pl.BlockSpec((1, ATT_KV_HEADS * V_ROWS, S), lambda b, i: (b, 0, 0)),
        ],
        out_specs=pl.BlockSpec((1, TQ, QW), lambda b, i: (b, i, 0)),
        scratch_shapes=[
            pltpu.VMEM((ATT_HEADS, LANES, TQ), BF16),
            pltpu.VMEM((IDX_HEADS, LANES, TQ), BF16),
            pltpu.VMEM((S, TQ), F32),
            pltpu.VMEM((32, S // 32, TQ), I32),
            pltpu.VMEM((S + DSA_TK, TQ), F32),
            pltpu.VMEM((ATT_HEADS, TQ), F32),
            pltpu.VMEM((ATT_HEADS, V_ROWS, TQ), F32),
            pltpu.VMEM((2, ATT_HEADS, TQ), F32),
            pltpu.VMEM((2, ATT_HEADS, DSA_TK, TQ), F32),
            pltpu.VMEM((2, ATT_HEADS, DSA_TK, TQ), BF16),
            pltpu.VMEM((1, TQ), F32),
            pltpu.VMEM((1, TQ), F32),
        ],
        compiler_params=_params("arbitrary", "arbitrary"),
        name="dsa",
    )(qT, qiT, wiT, ki, k, vT)


S5_SLABS = S5_WIDTH // LANES
S5_SLAB_STATES = (LANES // S5_GROUP) * S5_STATE


def _s5_param_kernel(lre_ref, lim_ref, ldt_ref, bre_ref, bim_ref,
                     lbre_ref, lbim_ref, bbre_ref, bbim_ref):
    lre = jnp.minimum(lre_ref[...], -1e-4)
    lim = lim_ref[...]
    dt = jnp.exp(ldt_ref[...])
    mag = jnp.exp(lre * dt)
    lbre = mag * jnp.cos(lim * dt)
    lbim = mag * jnp.sin(lim * dt)
    nre = lbre - 1.0
    den = lre * lre + lim * lim
    cre = (nre * lre + lbim * lim) / den
    cim = (lbim * lre - nre * lim) / den
    lbre_ref[...] = lbre
    lbim_ref[...] = lbim
    bbre_ref[...] = cre * bre_ref[...] - cim * bim_ref[...]
    bbim_ref[...] = cre * bim_ref[...] + cim * bre_ref[...]


def _s5_params(lam_re, lam_im, log_dt, b_re, b_im, c_re, c_im):
    G, P, H = S5_GROUPS, S5_STATE, S5_GROUP
    n = G * P
    flat = lambda a: a.astype(F32).reshape(1, n)
    ldt = jnp.repeat(log_dt.astype(F32), P).reshape(1, n)
    bt = lambda a: a.astype(F32).transpose(2, 0, 1).reshape(H, n)
    lbre, lbim, bbre, bbim = pl.pallas_call(
        _s5_param_kernel,
        out_shape=(jax.ShapeDtypeStruct((1, n), F32), jax.ShapeDtypeStruct((1, n), F32),
                   jax.ShapeDtypeStruct((H, n), F32), jax.ShapeDtypeStruct((H, n), F32)),
        name="s5_params",
    )(flat(lam_re), flat(lam_im), ldt, bt(b_re), bt(b_im))

    gl = LANES // H
    eye = jnp.eye(gl, dtype=F32)
    bb = jnp.stack([bbre, bbim]).reshape(2, H, S5_SLABS, gl, P)
    wb = jnp.einsum("ahjgp,gk->jghakp", bb, eye).reshape(S5_SLABS, LANES, 2 * gl * P)
    cc = jnp.stack([c_re.astype(F32), -c_im.astype(F32)]).reshape(2, S5_SLABS, gl, H, P)
    wc = jnp.einsum("ajghp,gk->jagpkh", cc, eye).reshape(S5_SLABS, 2 * gl * P, LANES)
    lam = lambda a: jnp.broadcast_to(a.reshape(S5_SLABS, 1, gl * P), (S5_SLABS, SUBLANES, gl * P))
    return lam(lbre), lam(lbim), wb.astype(BF16), wc.astype(BF16)


def _s5_kernel(u_ref, lbre_ref, lbim_ref, wb_ref, wc_ref, d_ref, wglu_ref, bglu_ref, o_ref,
               state_ref, h_ref, y_ref, u_sc, *, steps, batch):
    NS = S5_SLAB_STATES
    t = pl.program_id(0)

    @pl.when(t == 0)
    def _():
        state_ref[...] = jnp.zeros(state_ref.shape, F32)

    W = S5_WIDTH
    for b in range(batch):
        for j in range(S5_SLABS):
            c0 = b * W + j * LANES
            u_sc[j, pl.ds(b, steps, stride=batch), :] = u_ref[:, c0:c0 + LANES]
    for j in range(S5_SLABS):
        h_ref[j] = jnp.dot(u_sc[j].astype(BF16), wb_ref[j], preferred_element_type=F32)

    for j in range(S5_SLABS):
        lre = lbre_ref[j]
        lim = lbim_ref[j]

        def step(s, carry):
            hre, him = carry
            r0 = pl.multiple_of(s * batch, batch)
            nre = lre * hre - lim * him + h_ref[j, pl.ds(r0, batch), :NS]
            nim = lre * him + lim * hre + h_ref[j, pl.ds(r0, batch), NS:]
            h_ref[j, pl.ds(r0, batch), :NS] = nre
            h_ref[j, pl.ds(r0, batch), NS:] = nim
            return nre, nim

        hre, him = lax.fori_loop(0, steps, step, (state_ref[j, :, :NS], state_ref[j, :, NS:]),
                                 unroll=4)
        state_ref[j, :, :NS] = hre
        state_ref[j, :, NS:] = him

    ys = [jnp.dot(h_ref[j].astype(BF16), wc_ref[j], preferred_element_type=F32)
          for j in range(S5_SLABS)]
    u = jnp.concatenate([u_sc[j] for j in range(S5_SLABS)], axis=1)
    y = jnp.concatenate(ys, axis=1) + d_ref[...] * u
    y = jax.nn.gelu(y)
    gate = jnp.dot(y.astype(BF16), wglu_ref[...], preferred_element_type=F32) + bglu_ref[...]
    y = y * jax.nn.sigmoid(gate)
    for j in range(S5_SLABS):
        y_ref[j] = y[:, j * LANES:(j + 1) * LANES]
    for b in range(batch):
        for j in range(S5_SLABS):
            o_ref[b, :, j * LANES:(j + 1) * LANES] = (
                y_ref[j, pl.ds(b, steps, stride=batch), :].astype(BF16))


def _s5(u2, lbre, lbim, wb, wc, d_skip, w_glu, b_glu, batch, steps):
    S, W = u2.shape[0], S5_WIDTH
    NS2 = 2 * S5_SLAB_STATES
    const = lambda *shape: pl.BlockSpec(shape, lambda t: (0,) * len(shape))
    return pl.pallas_call(
        functools.partial(_s5_kernel, steps=steps, batch=batch),
        out_shape=jax.ShapeDtypeStruct((batch, S, W), BF16),
        grid=(S // steps,),
        in_specs=[
            pl.BlockSpec((steps, batch * W), lambda t: (t, 0)),
            const(S5_SLABS, SUBLANES, S5_SLAB_STATES),
            const(S5_SLABS, SUBLANES, S5_SLAB_STATES),
            const(S5_SLABS, LANES, NS2),
            const(S5_SLABS, NS2, LANES),
            const(1, W),
            const(W, W),
            const(1, W),
        ],
        out_specs=pl.BlockSpec((batch, steps, W), lambda t: (0, t, 0)),
        scratch_shapes=[
            pltpu.VMEM((S5_SLABS, batch, NS2), F32),
            pltpu.VMEM((S5_SLABS, steps * batch, NS2), F32),
            pltpu.VMEM((S5_SLABS, steps * batch, LANES), F32),
            pltpu.VMEM((S5_SLABS, steps * batch, LANES), F32),
        ],
        compiler_params=_params("arbitrary"),
        name="s5",
    )(u2, lbre, lbim, wb, wc, d_skip.astype(F32).reshape(1, W), w_glu.astype(BF16),
      b_glu.astype(F32).reshape(1, W))


def _layer_norm(z, g, b):
    mu = jnp.mean(z, axis=-1, keepdims=True)
    zc = z - mu
    var = jnp.mean(zc * zc, axis=-1, keepdims=True)
    return zc * lax.rsqrt(var + LN_EPS) * g + b


def _merge_kernel(x_ref, a_ref, yb_ref, qm_ref, mem_ref, wkv_ref, wg_ref, bg_ref,
                  wpa_ref, wpb_ref, wpc_ref, wout_ref, g_ref, b_ref, o_ref, kv_ref, *, alpha):
    D = x_ref.shape[2]
    HW = MEM_HEADS * MEM_HEAD_DIM

    @pl.when(pl.program_id(1) == 0)
    def _():
        kv_ref[...] = jnp.dot(mem_ref[0].astype(BF16), wkv_ref[...],
                              preferred_element_type=F32).astype(BF16)

    x = x_ref[0]
    xb = x.astype(BF16)
    qm = qm_ref[0]
    heads = []
    for h in range(MEM_HEADS):
        sl = slice(h * MEM_HEAD_DIM, (h + 1) * MEM_HEAD_DIM)
        s = lax.dot_general(qm[:, sl], kv_ref[:, sl], (((1,), (1,)), ((), ())),
                            preferred_element_type=F32)
        p = jnp.exp(s - jnp.max(s, axis=1, keepdims=True))
        p = p / jnp.sum(p, axis=1, keepdims=True)
        heads.append(jnp.dot(p.astype(BF16), kv_ref[:, HW + h * MEM_HEAD_DIM:HW + (h + 1) * MEM_HEAD_DIM],
                             preferred_element_type=F32))
    c_in = jnp.concatenate(heads, axis=1).astype(BF16)

    y_a = jnp.dot(a_ref[0], wpa_ref[...], preferred_element_type=F32)
    y_b = jnp.dot(yb_ref[0], wpb_ref[...], preferred_element_type=F32)
    y_c = jnp.dot(c_in, wpc_ref[...], preferred_element_type=F32)
    merged = None
    for k, yk in enumerate((y_a, y_b, y_c)):
        gk = jax.nn.sigmoid(jnp.dot(xb, wg_ref[:, k * D:(k + 1) * D], preferred_element_type=F32)
                            + bg_ref[:, k * D:(k + 1) * D])
        merged = gk * yk if merged is None else merged + gk * yk
    mix = jnp.dot(merged.astype(BF16), wout_ref[...], preferred_element_type=F32)
    o_ref[0] = _layer_norm(alpha * x + mix, g_ref[...], b_ref[...])


def _merge(x, att, yb, qm, mem, w_mem_kv, w_gate, b_gate, w_proj_a, w_proj_b, w_proj_c, w_out,
           ln_g, ln_b, alpha, tile):
    B, S, D = x.shape
    n_mem = mem.shape[1]
    HW = MEM_HEADS * MEM_HEAD_DIM
    const = lambda *shape: pl.BlockSpec(shape, lambda b, t: (0,) * len(shape))
    row = lambda width: pl.BlockSpec((1, tile, width), lambda b, t: (b, t, 0))
    return pl.pallas_call(
        functools.partial(_merge_kernel, alpha=alpha),
        out_shape=jax.ShapeDtypeStruct((B, S, D), F32),
        grid=(B, S // tile),
        in_specs=[
            row(D), row(att.shape[2]), row(yb.shape[2]), row(HW),
            pl.BlockSpec((1, n_mem, D), lambda b, t: (b, 0, 0)),
            const(D, 2 * HW), const(D, N_BRANCH * D), const(1, N_BRANCH * D),
            const(att.shape[2], D), const(yb.shape[2], D), const(HW, D), const(D, D),
            const(1, D), const(1, D),
        ],
        out_specs=row(D),
        scratch_shapes=[pltpu.VMEM((n_mem, 2 * HW), BF16)],
        compiler_params=_params("parallel", "arbitrary"),
        name="merge",
    )(x, att, yb, qm, mem, w_mem_kv.astype(BF16), w_gate.astype(BF16),
      b_gate.astype(F32).reshape(1, -1), w_proj_a.astype(BF16), w_proj_b.astype(BF16),
      w_proj_c.astype(BF16), w_out.astype(BF16), ln_g.astype(F32).reshape(1, D),
      ln_b.astype(F32).reshape(1, D))


FFN_CHUNK = 256


def _ffn_kernel(h_ref, wup_ref, cw_ref, cb_ref, wdn_ref, g_ref, b_ref, o_ref,
                act_ref, tail_ref, *, alpha, ffn_dim):
    T = h_ref.shape[1]
    FC = FFN_CHUNK
    HALO = SUBLANES

    @pl.when(pl.program_id(1) == 0)
    def _():
        tail_ref[...] = jnp.zeros(tail_ref.shape, F32)

    h = h_ref[0]
    hb = h.astype(BF16)
    halo_row = lax.broadcasted_iota(I32, (HALO, FC), 0)

    def conv_cols(c0):
        up = jnp.dot(hb, wup_ref[:, c0:c0 + FC], preferred_element_type=F32)
        tail = tail_ref[:, c0:c0 + FC]
        tail_ref[:, c0:c0 + FC] = up[T - HALO:]
        out = cb_ref[:, c0:c0 + FC] + cw_ref[CONV_WIDTH - 1:CONV_WIDTH, c0:c0 + FC] * up
        for d in range(1, CONV_WIDTH):
            rolled = pltpu.roll(up, d, 0)
            top = jnp.where(halo_row < d, pltpu.roll(tail, d, 0), rolled[:HALO])
            delayed = jnp.concatenate([top, rolled[HALO:]], axis=0)
            out = out + cw_ref[CONV_WIDTH - 1 - d:CONV_WIDTH - d, c0:c0 + FC] * delayed
        return out

    for c in range(ffn_dim // FC):
        gate = conv_cols(c * FC)
        up = conv_cols(ffn_dim + c * FC)
        act_ref[:, c * FC:(c + 1) * FC] = (gate * jax.nn.sigmoid(gate) * up).astype(BF16)
    f = jnp.dot(act_ref[...], wdn_ref[...], preferred_element_type=F32)
    o_ref[0] = _layer_norm(alpha * h + f, g_ref[...], b_ref[...])


def _ffn(h, w_up, conv_w, conv_b, w_down, ln_g, ln_b, alpha, tile):
    B, S, D = h.shape
    F2 = w_up.shape[1]
    const = lambda *shape: pl.BlockSpec(shape, lambda b, t: (0,) * len(shape))
    row = pl.BlockSpec((1, tile, D), lambda b, t: (b, t, 0))
    return pl.pallas_call(
        functools.partial(_ffn_kernel, alpha=alpha, ffn_dim=F2 // 2),
        out_shape=jax.ShapeDtypeStruct((B, S, D), F32),
        grid=(B, S // tile),
        in_specs=[row, const(D, F2), const(CONV_WIDTH, F2), const(1, F2), const(F2 // 2, D),
                  const(1, D), const(1, D)],
        out_specs=row,
        scratch_shapes=[
            pltpu.VMEM((tile, F2 // 2), BF16),
            pltpu.VMEM((SUBLANES, F2), F32),
        ],
        compiler_params=_params("parallel", "arbitrary"),
        name="ffn",
    )(h, w_up.astype(BF16), conv_w.astype(F32), conv_b.astype(F32).reshape(1, F2),
      w_down.astype(BF16), ln_g.astype(F32).reshape(1, D), ln_b.astype(F32).reshape(1, D))


PROJ_TILE = 512
S5_STEPS = 64
MERGE_TILE = 512
FFN_TILE = 512


def kernel(x, mem, positions, w_in, w_gate, b_gate, s5_lam_re, s5_lam_im, s5_log_dt, s5_b_re, s5_b_im, s5_c_re, s5_c_im, s5_d, w_glu, b_glu, w_mem_kv, w_proj_a, w_proj_b, w_proj_c, w_out, ln1_g, ln1_b, w_up, conv_w, conv_b, w_down, ln2_g, ln2_b):
    B, S, D = x.shape
    depth = w_in.shape[0]
    alpha = (2.0 * depth) ** 0.25
    h = x
    for l in range(depth):
        qT, k, vT, qiT, ki, wiT, u, qm = _proj(h, positions, w_in[l], min(PROJ_TILE, S))
        att = _dsa(qT, qiT, wiT, ki, k, vT)
        lbre, lbim, wb, wc = _s5_params(s5_lam_re[l], s5_lam_im[l], s5_log_dt[l], s5_b_re[l],
                                        s5_b_im[l], s5_c_re[l], s5_c_im[l])
        yb = _s5(u, lbre, lbim, wb, wc, s5_d[l], w_glu[l], b_glu[l], B, min(S5_STEPS, S))
        h = _merge(h, att, yb, qm, mem, w_mem_kv[l], w_gate[l], b_gate[l], w_proj_a[l],
                   w_proj_b[l], w_proj_c[l], w_out[l], ln1_g[l], ln1_b[l], alpha, min(MERGE_TILE, S))
        h = _ffn(h, w_up[l], conv_w[l], conv_b[l], w_down[l], ln2_g[l], ln2_b[l], alpha,
                 min(FFN_TILE, S))
    return h
```

```python
import functools
import math

import jax
import jax.numpy as jnp
from jax import lax
from jax.experimental import pallas as pl
from jax.experimental.pallas import tpu as pltpu

F32 = jnp.float32
BF16 = jnp.bfloat16
I32 = jnp.int32

CHUNK = 64
ATT_HEADS = 8
ATT_KV_HEADS = 2
ATT_HEAD_DIM = 64
ATT_REP = ATT_HEADS // ATT_KV_HEADS
IDX_HEADS = 4
IDX_DIM = 64
TOPK_MAX = 256
ROPE_THETA = 10000.0
S5_WIDTH = 512
S5_GROUP = 16
S5_GROUPS = S5_WIDTH // S5_GROUP
S5_STATE = 64
MEM_HEADS = 4
MEM_HEAD_DIM = 128
CONV_WIDTH = 3
N_BRANCH = 3
LN_EPS = 1e-5
NEG = -1e30
INT_MIN = -(2 ** 31)
LOG2E = math.log2(math.e)

LANES = 128
SUBLANES = 8
VMEM_LIMIT = 56 * 1024 * 1024

_Q0 = 0
_K0 = _Q0 + ATT_HEADS * ATT_HEAD_DIM
_V0 = _K0 + ATT_KV_HEADS * ATT_HEAD_DIM
_QI0 = _V0 + ATT_KV_HEADS * ATT_HEAD_DIM
_KI0 = _QI0 + IDX_HEADS * IDX_DIM
_WI0 = _KI0 + IDX_DIM
_U0 = _WI0 + IDX_HEADS
_QM0 = _U0 + S5_WIDTH
_END = _QM0 + MEM_HEADS * MEM_HEAD_DIM
_PAD = LANES - IDX_DIM - IDX_HEADS
_PU0 = _KI0 + LANES
_PQM0 = _PU0 + S5_WIDTH
_PEND = _PQM0 + MEM_HEADS * MEM_HEAD_DIM
V_ROWS = ATT_HEAD_DIM + 16


def _params(*sem):
    return pltpu.CompilerParams(dimension_semantics=sem, vmem_limit_bytes=VMEM_LIMIT)


def _proj_kernel(x_ref, pos_ref, invf_ref, w_ref,
                 qT_ref, k_ref, vT_ref, qiT_ref, ki_ref, wiT_ref, u_ref, qm_ref):
    T = x_ref.shape[1]
    y = jnp.dot(x_ref[0].astype(BF16), w_ref[...], preferred_element_type=F32)

    ang = pos_ref[0].astype(F32) * invf_ref[...]
    cos = jnp.cos(ang)
    sin = jnp.sin(ang)
    lane = lax.broadcasted_iota(I32, (T, LANES), 1)
    first = (lane % ATT_HEAD_DIM) < (ATT_HEAD_DIM // 2)
    sin = jnp.where(first, -sin, sin)

    def rope(z):
        partner = jnp.where(first, pltpu.roll(z, LANES - 32, 1), pltpu.roll(z, 32, 1))
        return z * cos + partner * sin

    att_scale = ATT_HEAD_DIM ** -0.5 * LOG2E
    for c in range(ATT_HEADS // 2):
        z = rope(y[:, _Q0 + c * LANES:_Q0 + (c + 1) * LANES]) * att_scale
        qT_ref[0, c * LANES:(c + 1) * LANES, :] = z.T.astype(BF16)
    k_ref[0] = rope(y[:, _K0:_K0 + LANES]).astype(BF16)
    vT = y[:, _V0:_V0 + LANES].T.astype(BF16)
    for g in range(ATT_KV_HEADS):
        vT_ref[0, g * V_ROWS:g * V_ROWS + ATT_HEAD_DIM, :] = vT[g * ATT_HEAD_DIM:(g + 1) * ATT_HEAD_DIM]
        vT_ref[0, g * V_ROWS + ATT_HEAD_DIM:(g + 1) * V_ROWS, :] = jnp.ones((V_ROWS - ATT_HEAD_DIM, T), BF16)
    idx_scale = IDX_DIM ** -0.5
    for c in range(IDX_HEADS // 2):
        z = rope(y[:, _QI0 + c * LANES:_QI0 + (c + 1) * LANES]) * idx_scale
        qiT_ref[0, c * LANES:(c + 1) * LANES, :] = z.T.astype(BF16)
    kw = y[:, _KI0:_KI0 + LANES]
    ki_ref[0] = jnp.where(lane < IDX_DIM, rope(kw), 0.0).astype(BF16)
    wiT_ref[0] = kw.T[IDX_DIM:IDX_DIM + IDX_HEADS] * (IDX_HEADS ** -0.5)
    u_ref[...] = y[:, _PU0:_PU0 + S5_WIDTH]
    qm_ref[0] = (y[:, _PQM0:_PEND] * (MEM_HEAD_DIM ** -0.5)).astype(BF16)


def _proj(x, positions, w_in, tile):
    B, S, D = x.shape
    nt = S // tile
    w_pad = jnp.concatenate(
        [w_in[:, :_U0], jnp.zeros((D, _PAD), w_in.dtype), w_in[:, _U0:]], axis=1).astype(BF16)
    half = ATT_HEAD_DIM // 2
    inv_freq = ROPE_THETA ** (-jnp.arange(half, dtype=F32) / half)
    invf = jnp.tile(inv_freq, LANES // half)[None, :]
    pos3 = positions.reshape(B, S, 1)
    QW = ATT_HEADS * ATT_HEAD_DIM
    IW = IDX_HEADS * IDX_DIM
    out_shape = (
        jax.ShapeDtypeStruct((B, QW, S), BF16),
        jax.ShapeDtypeStruct((B, S, LANES), BF16),
        jax.ShapeDtypeStruct((B, ATT_KV_HEADS * V_ROWS, S), BF16),
        jax.ShapeDtypeStruct((B, IW, S), BF16),
        jax.ShapeDtypeStruct((B, S, LANES), BF16),
        jax.ShapeDtypeStruct((B, IDX_HEADS, S), F32),
        jax.ShapeDtypeStruct((S, B * S5_WIDTH), F32),
        jax.ShapeDtypeStruct((B, S, MEM_HEADS * MEM_HEAD_DIM), BF16),
    )
    rows = lambda width: pl.BlockSpec((1, tile, width), lambda b, t: (b, t, 0))
    cols = lambda height: pl.BlockSpec((1, height, tile), lambda b, t: (b, 0, t))
    out_specs = (
        cols(QW), rows(LANES), cols(ATT_KV_HEADS * V_ROWS), cols(IW), rows(LANES), cols(IDX_HEADS),
        pl.BlockSpec((tile, S5_WIDTH), lambda b, t: (t, b)),
        rows(MEM_HEADS * MEM_HEAD_DIM),
    )
    return pl.pallas_call(
        _proj_kernel,
        out_shape=out_shape,
        grid=(B, nt),
        in_specs=[
            pl.BlockSpec((1, tile, D), lambda b, t: (b, t, 0)),
            pl.BlockSpec((1, tile, 1), lambda b, t: (b, t, 0)),
            pl.BlockSpec((1, LANES), lambda b, t: (0, 0)),
            pl.BlockSpec((D, _PEND), lambda b, t: (0, 0)),
        ],
        out_specs=out_specs,
        compiler_params=_params("parallel", "parallel"),
        name="proj",
    )(x, pos3, invf, w_pad)


DSA_TQ = 256
DSA_TK = 256
PLANE_KEYS = 32 * SUBLANES
FCOUNT_ROWS = 4 * SUBLANES


def _bit_transpose32(words):
    a = list(words)
    j, mask = 16, 0x0000FFFF
    while j:
        k = 0
        while k < 32:
            t = (a[k] ^ lax.shift_right_logical(a[k + j], jnp.int32(j))) & jnp.int32(mask)
            a[k] = a[k] ^ t
            a[k + j] = a[k + j] ^ lax.shift_left(t, jnp.int32(j))
            k = (k + j + 1) & ~j
        j >>= 1
        mask = (mask ^ (mask << j)) & 0xFFFFFFFF
    return a


def _dsa_kernel(qT_ref, qiT_ref, wiT_ref, ki_ref, k_ref, vT_ref, o_ref,
                wq_ref, wqi_ref, sc_ref, planes_ref, bias_ref, m_ref, acc_ref, alpha_ref,
                s_ref, p_ref, thr_ref, cnt_ref,
                *, topk):
    TQ, TK = DSA_TQ, DSA_TK
    HD = ATT_HEAD_DIM
    i = pl.program_id(1)
    n_kt = (i * TQ + TQ + TK - 1) // TK
    key_in_tile = lax.broadcasted_iota(I32, (TK, TQ), 0)
    qpos = i * TQ + lax.broadcasted_iota(I32, (1, TQ), 1)
    limit = (qpos // CHUNK + 1) * CHUNK
    wi = wiT_ref[0]

    def tile_start(kt):
        return pl.multiple_of(kt * TK, TK)

    zeros = jnp.zeros((HD, TQ), BF16)
    for h in range(ATT_HEADS):
        qh = qT_ref[0, h * HD:(h + 1) * HD, :]
        g = h // ATT_REP
        wq_ref[h] = jnp.concatenate([zeros] * g + [qh] + [zeros] * (ATT_KV_HEADS - 1 - g), axis=0)
    for h in range(IDX_HEADS):
        wqi_ref[h] = jnp.concatenate([qiT_ref[0, h * IDX_DIM:(h + 1) * IDX_DIM, :], zeros], axis=0)

    @pl.when((pl.program_id(0) == 0) & (i == 0))
    def _():
        planes_ref[...] = jnp.zeros(planes_ref.shape, I32)

    def score_tile(kt):
        ks = tile_start(kt)
        kk = ki_ref[0, pl.ds(ks, TK), :]
        acc = jnp.zeros((TK, TQ), F32)
        for h in range(IDX_HEADS):
            logit = jnp.dot(kk, wqi_ref[h], preferred_element_type=F32)
            acc = acc + wi[h:h + 1, :] * jnp.maximum(logit, 0.0)
        sc_ref[pl.ds(ks, TK), :] = jnp.where(ks + key_in_tile < limit, acc, NEG)

    def score_pair(j, carry):
        score_tile(2 * j)
        score_tile(2 * j + 1)
        return carry

    lax.fori_loop(0, n_kt // 2, score_pair, 0)

    @pl.when(n_kt % 2 == 1)
    def _():
        score_tile(n_kt - 1)

    def plane_tile(kt, carry):
        bits = lax.bitcast_convert_type(sc_ref[pl.ds(tile_start(kt), TK), :], I32)
        ukey = jnp.where(bits < 0, -bits, bits ^ jnp.int32(INT_MIN))
        for g in range(TK // PLANE_KEYS):
            row0 = pl.multiple_of(kt * (TK // 32) + g * SUBLANES, SUBLANES)
            for c in range(TQ // LANES):
                rows = ukey[g * PLANE_KEYS:(g + 1) * PLANE_KEYS, c * LANES:(c + 1) * LANES]
                words = _bit_transpose32([rows[j * SUBLANES:(j + 1) * SUBLANES] for j in range(32)])
                for b in range(32):
                    planes_ref[b, pl.ds(row0, SUBLANES), c * LANES:(c + 1) * LANES] = words[31 - b]
        return carry

    lax.fori_loop(0, n_kt, plane_tile, 0)

    n_words = planes_ref.shape[1]
    word_row = lax.broadcasted_iota(I32, (n_words, TQ), 0)
    cand0 = jnp.where(word_row < n_kt * (TK // 32), jnp.int32(-1), jnp.int32(0))

    def popcount_rows(w):
        return jnp.sum(lax.population_count(w), axis=0, keepdims=True)

    def bit_body(it, carry):
        cand, need, ukey_thr = carry
        b = 31 - it
        ones = cand & planes_ref[b]
        c1 = popcount_rows(ones)
        ok = c1 >= need
        cand = jnp.where(ok, ones, cand ^ ones)
        need = jnp.where(ok, need, need - c1)
        ukey_thr = ukey_thr | jnp.left_shift(ok.astype(I32), b)
        return cand, need, ukey_thr

    cand, need, ukey_thr = lax.fori_loop(
        0, 32, bit_body, (cand0, jnp.full((1, TQ), topk, I32), jnp.zeros((1, TQ), I32)))
    thr_key = ukey_thr ^ jnp.int32(INT_MIN)
    cnt_key = (topk - need) + popcount_rows(cand)

    def key_to_float(key):
        bits = jnp.where(key < 0, jnp.int32(INT_MIN) - key, key)
        return lax.bitcast_convert_type(bits, F32)

    def count(pred):
        def body(kt, cnt):
            m = jnp.where(pred(sc_ref[pl.ds(tile_start(kt), TK), :]), 1.0, 0.0)
            return cnt + m.reshape(TK // FCOUNT_ROWS, FCOUNT_ROWS, TQ).sum(axis=0)
        cnt = lax.fori_loop(0, n_kt, body, jnp.zeros((FCOUNT_ROWS, TQ), F32))
        return jnp.sum(cnt, axis=0, keepdims=True)

    thr_ref[...] = jnp.where(thr_key == INT_MIN, -jnp.inf, key_to_float(thr_key))
    cnt_ref[...] = cnt_key.astype(F32)
    recount = count(lambda s: s >= thr_ref[...])
    disagree = jnp.max(jnp.abs(recount - cnt_ref[...])) > 0.0

    @pl.when(disagree)
    def _():
        def body(it, carry):
            lo, cnt_lo = carry
            cand = lo + jnp.left_shift(jnp.int32(1), 31 - it)
            cand_f = key_to_float(cand)
            c = count(lambda s: s >= cand_f)
            ok = c >= float(topk)
            return jnp.where(ok, cand, lo), jnp.where(ok, c, cnt_lo)
        lo, cnt_lo = lax.fori_loop(0, 32, body, (jnp.full((1, TQ), INT_MIN, I32),
                                                 jnp.full((1, TQ), n_kt * TK, I32).astype(F32)))
        thr_ref[...] = jnp.where(lo == INT_MIN, -jnp.inf, key_to_float(lo))
        cnt_ref[...] = cnt_lo

    thr = thr_ref[...]
    has_ties = jnp.max(cnt_ref[...]) > float(topk)

    @pl.when(jnp.logical_not(has_ties))
    def _():
        def bias_tile(kt, carry):
            ks = tile_start(kt)
            sel = (sc_ref[pl.ds(ks, TK), :] >= thr) & (ks + key_in_tile < limit)
            bias_ref[pl.ds(ks, TK), :] = jnp.where(sel, 0.0, NEG)
            return carry
        lax.fori_loop(0, n_kt, bias_tile, 0)

    @pl.when(has_ties)
    def _():
        need = float(topk) - count(lambda s: s > thr)
        ri = lax.broadcasted_iota(I32, (TK, TK), 0)
        ci = lax.broadcasted_iota(I32, (TK, TK), 1)
        tri = jnp.where(ci < ri, 1.0, 0.0).astype(BF16)

        def bias_tile(kt, run):
            ks = tile_start(kt)
            s = sc_ref[pl.ds(ks, TK), :]
            eq = jnp.where(s == thr, 1.0, 0.0)
            before = jnp.dot(tri, eq.astype(BF16), preferred_element_type=F32) + run
            sel = (s > thr) | ((s == thr) & (before < need))
            sel = sel & (ks + key_in_tile < limit)
            bias_ref[pl.ds(ks, TK), :] = jnp.where(sel, 0.0, NEG)
            return run + jnp.sum(eq, axis=0, keepdims=True)
        lax.fori_loop(0, n_kt, bias_tile, jnp.zeros((1, TQ), F32))

    m_ref[...] = jnp.full(m_ref.shape, -3e38, F32)
    acc_ref[...] = jnp.zeros(acc_ref.shape, F32)

    def scores(kt, slot):
        kk = k_ref[0, pl.ds(tile_start(jnp.minimum(kt, n_kt - 1)), TK), :]
        bias = bias_ref[pl.ds(tile_start(jnp.minimum(kt, n_kt)), TK), :]
        for h in range(ATT_HEADS):
            s_ref[slot, h] = jnp.dot(kk, wq_ref[h], preferred_element_type=F32) + bias

    def softmax(slot):
        for h in range(ATT_HEADS):
            for c in range(TQ // LANES):
                ql = slice(c * LANES, (c + 1) * LANES)
                s = s_ref[slot, h, :, ql]
                m_old = m_ref[h:h + 1, ql]
                m_new = jnp.maximum(m_old, jnp.max(s, axis=0, keepdims=True))
                p_ref[slot, h, :, ql] = jnp.exp2(s - m_new).astype(BF16)
                alpha_ref[slot, h:h + 1, ql] = jnp.exp2(m_old - m_new)
                m_ref[h:h + 1, ql] = m_new

    def weighted_values(kt, slot):
        ks = tile_start(jnp.clip(kt, 0, n_kt - 1))
        for h in range(ATT_HEADS):
            g = h // ATT_REP
            pv = jnp.dot(vT_ref[0, g * V_ROWS:(g + 1) * V_ROWS, pl.ds(ks, TK)], p_ref[slot, h],
                         preferred_element_type=F32)
            acc_ref[h] = alpha_ref[slot, h:h + 1, :] * acc_ref[h] + pv

    bias_ref[pl.ds(tile_start(n_kt), TK), :] = jnp.full((TK, TQ), NEG, F32)
    scores(0, 0)
    p_ref[1] = jnp.zeros(p_ref.shape[1:], BF16)
    alpha_ref[1] = jnp.ones(alpha_ref.shape[1:], F32)

    def att_pair(j, carry):
        scores(2 * j + 1, 1)
        softmax(0)
        weighted_values(2 * j - 1, 1)
        scores(2 * j + 2, 0)
        softmax(1)
        weighted_values(2 * j, 0)
        return carry

    n_pairs = n_kt // 2
    lax.fori_loop(0, n_pairs, att_pair, 0)
    weighted_values(2 * n_pairs - 1, 1)

    @pl.when(n_kt % 2 == 1)
    def _():
        softmax(0)
        weighted_values(n_kt - 1, 0)

    for c in range(ATT_HEADS // 2):
        pair = [acc_ref[h, :HD, :] / acc_ref[h, HD:HD + 1, :] for h in (2 * c, 2 * c + 1)]
        o_ref[0, :, c * LANES:(c + 1) * LANES] = jnp.concatenate(pair, axis=0).T.astype(BF16)


def _dsa(qT, qiT, wiT, ki, k, vT):
    B, QW, S = qT.shape
    TQ = DSA_TQ
    topk = min(TOPK_MAX, S // 4)
    assert S % TQ == 0 and TQ == DSA_TK and DSA_TK >= topk
    return pl.pallas_call(
        functools.partial(_dsa_kernel, topk=topk),
        out_shape=jax.ShapeDtypeStruct((B, S, QW), BF16),
        grid=(B, S // TQ),
        in_specs=[
            pl.BlockSpec((1, QW, TQ), lambda b, i: (b, 0, i)),
            pl.BlockSpec((1, IDX_HEADS * IDX_DIM, TQ), lambda b, i: (b, 0, i)),
            pl.BlockSpec((1, IDX_HEADS, TQ), lambda b, i: (b, 0, i)),
            pl.BlockSpec((1, S, LANES), lambda b, i: (b, 0, 0)),
            pl.BlockSpec((1, S, LANES), lambda b, i: (b, 0, 0)),
            pl.BlockSpec((1, ATT_KV_HEADS * V_ROWS, S), lambda b, i: (b, 0, 0)),
        ],
        out_specs=pl.BlockSpec((1, TQ, QW), lambda b, i: (b, i, 0)),
        scratch_shapes=[
            pltpu.VMEM((ATT_HEADS, LANES, TQ), BF16),
            pltpu.VMEM((IDX_HEADS, LANES, TQ), BF16),
            pltpu.VMEM((S, TQ), F32),
            pltpu.VMEM((32, S // 32, TQ), I32),
            pltpu.VMEM((S + DSA_TK, TQ), F32),
            pltpu.VMEM((ATT_HEADS, TQ), F32),
            pltpu.VMEM((ATT_HEADS, V_ROWS, TQ), F32),
            pltpu.VMEM((2, ATT_HEADS, TQ), F32),
            pltpu.VMEM((2, ATT_HEADS, DSA_TK, TQ), F32),
            pltpu.VMEM((2, ATT_HEADS, DSA_TK, TQ), BF16),
            pltpu.VMEM((1, TQ), F32),
            pltpu.VMEM((1, TQ), F32),
        ],
        compiler_params=_params("arbitrary", "arbitrary"),
        name="dsa",
    )(qT, qiT, wiT, ki, k, vT)


S5_SLABS = S5_WIDTH // LANES
S5_SLAB_STATES = (LANES // S5_GROUP) * S5_STATE


def _s5_param_kernel(lre_ref, lim_ref, ldt_ref, bre_ref, bim_ref,
                     lbre_ref, lbim_ref, bbre_ref, bbim_ref):
    lre = jnp.minimum(lre_ref[...], -1e-4)
    lim = lim_ref[...]
    dt = jnp.exp(ldt_ref[...])
    mag = jnp.exp(lre * dt)
    lbre = mag * jnp.cos(lim * dt)
    lbim = mag * jnp.sin(lim * dt)
    nre = lbre - 1.0
    den = lre * lre + lim * lim
    cre = (nre * lre + lbim * lim) / den
    cim = (lbim * lre - nre * lim) / den
    lbre_ref[...] = lbre
    lbim_ref[...] = lbim
    bbre_ref[...] = cre * bre_ref[...] - cim * bim_ref[...]
    bbim_ref[...] = cre * bim_ref[...] + cim * bre_ref[...]


def _s5_params(lam_re, lam_im, log_dt, b_re, b_im, c_re, c_im):
    G, P, H = S5_GROUPS, S5_STATE, S5_GROUP
    n = G * P
    flat = lambda a: a.astype(F32).reshape(1, n)
    ldt = jnp.repeat(log_dt.astype(F32), P).reshape(1, n)
    bt = lambda a: a.astype(F32).transpose(2, 0, 1).reshape(H, n)
    lbre, lbim, bbre, bbim = pl.pallas_call(
        _s5_param_kernel,
        out_shape=(jax.ShapeDtypeStruct((1, n), F32), jax.ShapeDtypeStruct((1, n), F32),
                   jax.ShapeDtypeStruct((H, n), F32), jax.ShapeDtypeStruct((H, n), F32)),
        name="s5_params",
    )(flat(lam_re), flat(lam_im), ldt, bt(b_re), bt(b_im))

    gl = LANES // H
    eye = jnp.eye(gl, dtype=F32)
    bb = jnp.stack([bbre, bbim]).reshape(2, H, S5_SLABS, gl, P)
    wb = jnp.einsum("ahjgp,gk->jghakp", bb, eye).reshape(S5_SLABS, LANES, 2 * gl * P)
    cc = jnp.stack([c_re.astype(F32), -c_im.astype(F32)]).reshape(2, S5_SLABS, gl, H, P)
    wc = jnp.einsum("ajghp,gk->jagpkh", cc, eye).reshape(S5_SLABS, 2 * gl * P, LANES)
    lam = lambda a: jnp.broadcast_to(a.reshape(S5_SLABS, 1, gl * P), (S5_SLABS, SUBLANES, gl * P))
    return lam(lbre), lam(lbim), wb.astype(BF16), wc.astype(BF16)


def _s5_kernel(u_ref, lbre_ref, lbim_ref, wb_ref, wc_ref, d_ref, wglu_ref, bglu_ref, o_ref,
               state_ref, h_ref, y_ref, u_sc, hb_ref, *, steps, batch):
    NS = S5_SLAB_STATES
    t = pl.program_id(0)

    @pl.when(t == 0)
    def _():
        state_ref[...] = jnp.zeros(state_ref.shape, F32)

    W = S5_WIDTH
    for b in range(batch):
        for j in range(S5_SLABS):
            c0 = b * W + j * LANES
            u_sc[j, pl.ds(b, steps, stride=batch), :] = u_ref[:, c0:c0 + LANES]
    for j in range(S5_SLABS):
        h_ref[j] = jnp.dot(u_sc[j].astype(BF16), wb_ref[j], preferred_element_type=F32)

    for j in range(S5_SLABS):
        lre = lbre_ref[j]
        lim = lbim_ref[j]

        def step2(s2, carry):
            hre, him = carry
            r0 = pl.multiple_of(s2 * (2 * batch), 2 * batch)
            bre = h_ref[j, pl.ds(r0, 2 * batch), :NS]
            bim = h_ref[j, pl.ds(r0, 2 * batch), NS:]
            are = lre * hre - lim * him + bre[:batch]
            aim = lre * him + lim * hre + bim[:batch]
            nre = lre * are - lim * aim + bre[batch:]
            nim = lre * aim + lim * are + bim[batch:]
            hb_ref[j, pl.ds(r0, 2 * batch), :NS] = jnp.concatenate([are, nre], axis=0).astype(BF16)
            hb_ref[j, pl.ds(r0, 2 * batch), NS:] = jnp.concatenate([aim, nim], axis=0).astype(BF16)
            return nre, nim

        hre, him = lax.fori_loop(0, steps // 2, step2, (state_ref[j, :, :NS], state_ref[j, :, NS:]),
                                 unroll=2)
        state_ref[j, :, :NS] = hre
        state_ref[j, :, NS:] = him

    ys = [jnp.dot(hb_ref[j], wc_ref[j], preferred_element_type=F32) for j in range(S5_SLABS)]
    u = jnp.concatenate([u_sc[j] for j in range(S5_SLABS)], axis=1)
    y = jnp.concatenate(ys, axis=1) + d_ref[...] * u
    y = jax.nn.gelu(y)
    gate = jnp.dot(y.astype(BF16), wglu_ref[...], preferred_element_type=F32) + bglu_ref[...]
    y = y * jax.nn.sigmoid(gate)
    for j in range(S5_SLABS):
        y_ref[j] = y[:, j * LANES:(j + 1) * LANES]
    for b in range(batch):
        for j in range(S5_SLABS):
            o_ref[b, :, j * LANES:(j + 1) * LANES] = (
                y_ref[j, pl.ds(b, steps, stride=batch), :].astype(BF16))


def _s5(u2, lbre, lbim, wb, wc, d_skip, w_glu, b_glu, batch, steps):
    S, W = u2.shape[0], S5_WIDTH
    NS2 = 2 * S5_SLAB_STATES
    const = lambda *shape: pl.BlockSpec(shape, lambda t: (0,) * len(shape))
    return pl.pallas_call(
        functools.partial(_s5_kernel, steps=steps, batch=batch),
        out_shape=jax.ShapeDtypeStruct((batch, S, W), BF16),
        grid=(S // steps,),
        in_specs=[
            pl.BlockSpec((steps, batch * W), lambda t: (t, 0)),
            const(S5_SLABS, SUBLANES, S5_SLAB_STATES),
            const(S5_SLABS, SUBLANES, S5_SLAB_STATES),
            const(S5_SLABS, LANES, NS2),
            const(S5_SLABS, NS2, LANES),
            const(1, W),
            const(W, W),
            const(1, W),
        ],
        out_specs=pl.BlockSpec((batch, steps, W), lambda t: (0, t, 0)),
        scratch_shapes=[
            pltpu.VMEM((S5_SLABS, batch, NS2), F32),
            pltpu.VMEM((S5_SLABS, steps * batch, NS2), F32),
            pltpu.VMEM((S5_SLABS, steps * batch, LANES), F32),
            pltpu.VMEM((S5_SLABS, steps * batch, LANES), F32),
            pltpu.VMEM((S5_SLABS, steps * batch, NS2), BF16),
        ],
        compiler_params=_params("arbitrary"),
        name="s5",
    )(u2, lbre, lbim, wb, wc, d_skip.astype(F32).reshape(1, W), w_glu.astype(BF16),
      b_glu.astype(F32).reshape(1, W))


def _layer_norm(z, g, b):
    mu = jnp.mean(z, axis=-1, keepdims=True)
    zc = z - mu
    var = jnp.mean(zc * zc, axis=-1, keepdims=True)
    return zc * lax.rsqrt(var + LN_EPS) * g + b


def _merge_kernel(x_ref, a_ref, yb_ref, qm_ref, mem_ref, wkv_ref, wg_ref, bg_ref,
                  wpa_ref, wpb_ref, wpc_ref, wout_ref, g_ref, b_ref, o_ref, kv_ref, *, alpha):
    D = x_ref.shape[2]
    HW = MEM_HEADS * MEM_HEAD_DIM

    @pl.when(pl.program_id(1) == 0)
    def _():
        kv_ref[...] = jnp.dot(mem_ref[0].astype(BF16), wkv_ref[...],
                              preferred_element_type=F32).astype(BF16)

    x = x_ref[0]
    xb = x.astype(BF16)
    qm = qm_ref[0]
    heads = []
    for h in range(MEM_HEADS):
        sl = slice(h * MEM_HEAD_DIM, (h + 1) * MEM_HEAD_DIM)
        s = lax.dot_general(qm[:, sl], kv_ref[:, sl], (((1,), (1,)), ((), ())),
                            preferred_element_type=F32)
        p = jnp.exp(s - jnp.max(s, axis=1, keepdims=True))
        p = p / jnp.sum(p, axis=1, keepdims=True)
        heads.append(jnp.dot(p.astype(BF16), kv_ref[:, HW + h * MEM_HEAD_DIM:HW + (h + 1) * MEM_HEAD_DIM],
                             preferred_element_type=F32))
    c_in = jnp.concatenate(heads, axis=1).astype(BF16)

    y_a = jnp.dot(a_ref[0], wpa_ref[...], preferred_element_type=F32)
    y_b = jnp.dot(yb_ref[0], wpb_ref[...], preferred_element_type=F32)
    y_c = jnp.dot(c_in, wpc_ref[...], preferred_element_type=F32)
    merged = None
    for k, yk in enumerate((y_a, y_b, y_c)):
        gk = jax.nn.sigmoid(jnp.dot(xb, wg_ref[:, k * D:(k + 1) * D], preferred_element_type=F32)
                            + bg_ref[:, k * D:(k + 1) * D])
        merged = gk * yk if merged is None else merged + gk * yk
    mix = jnp.dot(merged.astype(BF16), wout_ref[...], preferred_element_type=F32)
    o_ref[0] = _layer_norm(alpha * x + mix, g_ref[...], b_ref[...])


def _merge(x, att, yb, qm, mem, w_mem_kv, w_gate, b_gate, w_proj_a, w_proj_b, w_proj_c, w_out,
           ln_g, ln_b, alpha, tile):
    B, S, D = x.shape
    n_mem = mem.shape[1]
    HW = MEM_HEADS * MEM_HEAD_DIM
    const = lambda *shape: pl.BlockSpec(shape, lambda b, t: (0,) * len(shape))
    row = lambda width: pl.BlockSpec((1, tile, width), lambda b, t: (b, t, 0))
    return pl.pallas_call(
        functools.partial(_merge_kernel, alpha=alpha),
        out_shape=jax.ShapeDtypeStruct((B, S, D), F32),
        grid=(B, S // tile),
        in_specs=[
            row(D), row(att.shape[2]), row(yb.shape[2]), row(HW),
            pl.BlockSpec((1, n_mem, D), lambda b, t: (b, 0, 0)),
            const(D, 2 * HW), const(D, N_BRANCH * D), const(1, N_BRANCH * D),
            const(att.shape[2], D), const(yb.shape[2], D), const(HW, D), const(D, D),
            const(1, D), const(1, D),
        ],
        out_specs=row(D),
        scratch_shapes=[pltpu.VMEM((n_mem, 2 * HW), BF16)],
        compiler_params=_params("parallel", "arbitrary"),
        name="merge",
    )(x, att, yb, qm, mem, w_mem_kv.astype(BF16), w_gate.astype(BF16),
      b_gate.astype(F32).reshape(1, -1), w_proj_a.astype(BF16), w_proj_b.astype(BF16),
      w_proj_c.astype(BF16), w_out.astype(BF16), ln_g.astype(F32).reshape(1, D),
      ln_b.astype(F32).reshape(1, D))


FFN_CHUNK = 256


def _ffn_kernel(h_ref, wup_ref, cw_ref, cb_ref, wdn_ref, g_ref, b_ref, o_ref,
                act_ref, tail_ref, *, alpha, ffn_dim):
    T = h_ref.shape[1]
    FC = FFN_CHUNK
    HALO = SUBLANES

    @pl.when(pl.program_id(1) == 0)
    def _():
        tail_ref[...] = jnp.zeros(tail_ref.shape, F32)

    h = h_ref[0]
    hb = h.astype(BF16)
    halo_row = lax.broadcasted_iota(I32, (HALO, FC), 0)

    def conv_cols(c0):
        up = jnp.dot(hb, wup_ref[:, c0:c0 + FC], preferred_element_type=F32)
        tail = tail_ref[:, c0:c0 + FC]
        tail_ref[:, c0:c0 + FC] = up[T - HALO:]
        out = cb_ref[:, c0:c0 + FC] + cw_ref[CONV_WIDTH - 1:CONV_WIDTH, c0:c0 + FC] * up
        for d in range(1, CONV_WIDTH):
            rolled = pltpu.roll(up, d, 0)
            top = jnp.where(halo_row < d, pltpu.roll(tail, d, 0), rolled[:HALO])
            delayed = jnp.concatenate([top, rolled[HALO:]], axis=0)
            out = out + cw_ref[CONV_WIDTH - 1 - d:CONV_WIDTH - d, c0:c0 + FC] * delayed
        return out

    for c in range(ffn_dim // FC):
        gate = conv_cols(c * FC)
        up = conv_cols(ffn_dim + c * FC)
        act_ref[:, c * FC:(c + 1) * FC] = (gate * jax.nn.sigmoid(gate) * up).astype(BF16)
    f = jnp.dot(act_ref[...], wdn_ref[...], preferred_element_type=F32)
    o_ref[0] = _layer_norm(alpha * h + f, g_ref[...], b_ref[...])


def _ffn(h, w_up, conv_w, conv_b, w_down, ln_g, ln_b, alpha, tile):
    B, S, D = h.shape
    F2 = w_up.shape[1]
    const = lambda *shape: pl.BlockSpec(shape, lambda b, t: (0,) * len(shape))
    row = pl.BlockSpec((1, tile, D), lambda b, t: (b, t, 0))
    return pl.pallas_call(
        functools.partial(_ffn_kernel, alpha=alpha, ffn_dim=F2 // 2),
        out_shape=jax.ShapeDtypeStruct((B, S, D), F32),
        grid=(B, S // tile),
        in_specs=[row, const(D, F2), const(CONV_WIDTH, F2), const(1, F2), const(F2 // 2, D),
                  const(1, D), const(1, D)],
        out_specs=row,
        scratch_shapes=[
            pltpu.VMEM((tile, F2 // 2), BF16),
            pltpu.VMEM((SUBLANES, F2), F32),
        ],
        compiler_params=_params("parallel", "arbitrary"),
        name="ffn",
    )(h, w_up.astype(BF16), conv_w.astype(F32), conv_b.astype(F32).reshape(1, F2),
      w_down.astype(BF16), ln_g.astype(F32).reshape(1, D), ln_b.astype(F32).reshape(1, D))


PROJ_TILE = 512
S5_STEPS = 64
MERGE_TILE = 512
FFN_TILE = 512


def kernel(x, mem, positions, w_in, w_gate, b_gate, s5_lam_re, s5_lam_im, s5_log_dt, s5_b_re, s5_b_im, s5_c_re, s5_c_im, s5_d, w_glu, b_glu, w_mem_kv, w_proj_a, w_proj_b, w_proj_c, w_out, ln1_g, ln1_b, w_up, conv_w, conv_b, w_down, ln2_g, ln2_b):
    B, S, D = x.shape
    depth = w_in.shape[0]
    alpha = (2.0 * depth) ** 0.25
    h = x
    for l in range(depth):
        qT, k, vT, qiT, ki, wiT, u, qm = _proj(h, positions, w_in[l], min(PROJ_TILE, S))
        att = _dsa(qT, qiT, wiT, ki, k, vT)
        lbre, lbim, wb, wc = _s5_params(s5_lam_re[l], s5_lam_im[l], s5_log_dt[l], s5_b_re[l],
                                        s5_b_im[l], s5_c_re[l], s5_c_im[l])
        yb = _s5(u, lbre, lbim, wb, wc, s5_d[l], w_glu[l], b_glu[l], B, min(S5_STEPS, S))
        h = _merge(h, att, yb, qm, mem, w_mem_kv[l], w_gate[l], b_gate[l], w_proj_a[l],
                   w_proj_b[l], w_proj_c[l], w_out[l], ln1_g[l], ln1_b[l], alpha, min(MERGE_TILE, S))
        h = _ffn(h, w_up[l], conv_w[l], conv_b[l], w_down[l], ln2_g[l], ln2_b[l], alpha,
                 min(FFN_TILE, S))
    return h
```

```python
import functools
import math

import jax
import jax.numpy as jnp
from jax import lax
from jax.experimental import pallas as pl
from jax.experimental.pallas import tpu as pltpu

F32 = jnp.float32
BF16 = jnp.bfloat16
I32 = jnp.int32

CHUNK = 64
ATT_HEADS = 8
ATT_KV_HEADS = 2
ATT_HEAD_DIM = 64
ATT_REP = ATT_HEADS // ATT_KV_HEADS
IDX_HEADS = 4
IDX_DIM = 64
TOPK_MAX = 256
ROPE_THETA = 10000.0
S5_WIDTH = 512
S5_GROUP = 16
S5_GROUPS = S5_WIDTH // S5_GROUP
S5_STATE = 64
MEM_HEADS = 4
MEM_HEAD_DIM = 128
CONV_WIDTH = 3
N_BRANCH = 3
LN_EPS = 1e-5
NEG = -1e30
INT_MIN = -(2 ** 31)
LOG2E = math.log2(math.e)

LANES = 128
SUBLANES = 8
VMEM_LIMIT = 56 * 1024 * 1024

_Q0 = 0
_K0 = _Q0 + ATT_HEADS * ATT_HEAD_DIM
_V0 = _K0 + ATT_KV_HEADS * ATT_HEAD_DIM
_QI0 = _V0 + ATT_KV_HEADS * ATT_HEAD_DIM
_KI0 = _QI0 + IDX_HEADS * IDX_DIM
_WI0 = _KI0 + IDX_DIM
_U0 = _WI0 + IDX_HEADS
_QM0 = _U0 + S5_WIDTH
_END = _QM0 + MEM_HEADS * MEM_HEAD_DIM
_PAD = LANES - IDX_DIM - IDX_HEADS
_PU0 = _KI0 + LANES
_PQM0 = _PU0 + S5_WIDTH
_PEND = _PQM0 + MEM_HEADS * MEM_HEAD_DIM
V_ROWS = ATT_HEAD_DIM + 16


def _params(*sem):
    return pltpu.CompilerParams(dimension_semantics=sem, vmem_limit_bytes=VMEM_LIMIT)


def _proj_kernel(x_ref, pos_ref, invf_ref, w_ref,
                 qT_ref, k_ref, vT_ref, qiT_ref, ki_ref, wiT_ref, u_ref, qm_ref):
    T = x_ref.shape[1]
    y = jnp.dot(x_ref[0].astype(BF16), w_ref[...], preferred_element_type=F32)

    ang = pos_ref[0].astype(F32) * invf_ref[...]
    cos = jnp.cos(ang)
    sin = jnp.sin(ang)
    lane = lax.broadcasted_iota(I32, (T, LANES), 1)
    first = (lane % ATT_HEAD_DIM) < (ATT_HEAD_DIM // 2)
    sin = jnp.where(first, -sin, sin)

    def rope(z):
        partner = jnp.where(first, pltpu.roll(z, LANES - 32, 1), pltpu.roll(z, 32, 1))
        return z * cos + partner * sin

    att_scale = ATT_HEAD_DIM ** -0.5 * LOG2E
    for c in range(ATT_HEADS // 2):
        z = rope(y[:, _Q0 + c * LANES:_Q0 + (c + 1) * LANES]) * att_scale
        qT_ref[0, c * LANES:(c + 1) * LANES, :] = z.T.astype(BF16)
    k_ref[0] = rope(y[:, _K0:_K0 + LANES]).astype(BF16)
    vT = y[:, _V0:_V0 + LANES].T.astype(BF16)
    for g in range(ATT_KV_HEADS):
        vT_ref[0, g * V_ROWS:g * V_ROWS + ATT_HEAD_DIM, :] = vT[g * ATT_HEAD_DIM:(g + 1) * ATT_HEAD_DIM]
        vT_ref[0, g * V_ROWS + ATT_HEAD_DIM:(g + 1) * V_ROWS, :] = jnp.ones((V_ROWS - ATT_HEAD_DIM, T), BF16)
    idx_scale = IDX_DIM ** -0.5
    for c in range(IDX_HEADS // 2):
        z = rope(y[:, _QI0 + c * LANES:_QI0 + (c + 1) * LANES]) * idx_scale
        qiT_ref[0, c * LANES:(c + 1) * LANES, :] = z.T.astype(BF16)
    kw = y[:, _KI0:_KI0 + LANES]
    ki_ref[0] = jnp.where(lane < IDX_DIM, rope(kw), 0.0).astype(BF16)
    wiT_ref[0] = kw.T[IDX_DIM:IDX_DIM + IDX_HEADS] * (IDX_HEADS ** -0.5)
    u_ref[...] = y[:, _PU0:_PU0 + S5_WIDTH]
    qm_ref[0] = (y[:, _PQM0:_PEND] * (MEM_HEAD_DIM ** -0.5)).astype(BF16)


def _proj(x, positions, w_in, tile):
    B, S, D = x.shape
    nt = S // tile
    w_pad = jnp.concatenate(
        [w_in[:, :_U0], jnp.zeros((D, _PAD), w_in.dtype), w_in[:, _U0:]], axis=1).astype(BF16)
    half = ATT_HEAD_DIM // 2
    inv_freq = ROPE_THETA ** (-jnp.arange(half, dtype=F32) / half)
    invf = jnp.tile(inv_freq, LANES // half)[None, :]
    pos3 = positions.reshape(B, S, 1)
    QW = ATT_HEADS * ATT_HEAD_DIM
    IW = IDX_HEADS * IDX_DIM
    out_shape = (
        jax.ShapeDtypeStruct((B, QW, S), BF16),
        jax.ShapeDtypeStruct((B, S, LANES), BF16),
        jax.ShapeDtypeStruct((B, ATT_KV_HEADS * V_ROWS, S), BF16),
        jax.ShapeDtypeStruct((B, IW, S), BF16),
        jax.ShapeDtypeStruct((B, S, LANES), BF16),
        jax.ShapeDtypeStruct((B, IDX_HEADS, S), F32),
        jax.ShapeDtypeStruct((S, B * S5_WIDTH), F32),
        jax.ShapeDtypeStruct((B, S, MEM_HEADS * MEM_HEAD_DIM), BF16),
    )
    rows = lambda width: pl.BlockSpec((1, tile, width), lambda b, t: (b, t, 0))
    cols = lambda height: pl.BlockSpec((1, height, tile), lambda b, t: (b, 0, t))
    out_specs = (
        cols(QW), rows(LANES), cols(ATT_KV_HEADS * V_ROWS), cols(IW), rows(LANES), cols(IDX_HEADS),
        pl.BlockSpec((tile, S5_WIDTH), lambda b, t: (t, b)),
        rows(MEM_HEADS * MEM_HEAD_DIM),
    )
    return pl.pallas_call(
        _proj_kernel,
        out_shape=out_shape,
        grid=(B, nt),
        in_specs=[
            pl.BlockSpec((1, tile, D), lambda b, t: (b, t, 0)),
            pl.BlockSpec((1, tile, 1), lambda b, t: (b, t, 0)),
            pl.BlockSpec((1, LANES), lambda b, t: (0, 0)),
            pl.BlockSpec((D, _PEND), lambda b, t: (0, 0)),
        ],
        out_specs=out_specs,
        compiler_params=_params("parallel", "parallel"),
        name="proj",
    )(x, pos3, invf, w_pad)


DSA_TQ = 256
DSA_TK = 256
PLANE_KEYS = 32 * SUBLANES
FCOUNT_ROWS = 4 * SUBLANES


def _bit_transpose32(words):
    a = list(words)
    j, mask = 16, 0x0000FFFF
    while j:
        k = 0
        while k < 32:
            t = (a[k] ^ lax.shift_right_logical(a[k + j], jnp.int32(j))) & jnp.int32(mask)
            a[k] = a[k] ^ t
            a[k + j] = a[k + j] ^ lax.shift_left(t, jnp.int32(j))
            k = (k + j + 1) & ~j
        j >>= 1
        mask = (mask ^ (mask << j)) & 0xFFFFFFFF
    return a


def _dsa_kernel(qT_ref, qiT_ref, wiT_ref, ki_ref, k_ref, vT_ref, o_ref,
                wq_ref, wqi_ref, sc_ref, planes_ref, bias_ref, m_ref, acc_ref, alpha_ref,
                s_ref, p_ref, thr_ref, cnt_ref,
                *, topk):
    TQ, TK = DSA_TQ, DSA_TK
    HD = ATT_HEAD_DIM
    i = pl.program_id(1)
    n_kt = (i * TQ + TQ + TK - 1) // TK
    key_in_tile = lax.broadcasted_iota(I32, (TK, TQ), 0)
    qpos = i * TQ + lax.broadcasted_iota(I32, (1, TQ), 1)
    limit = (qpos // CHUNK + 1) * CHUNK
    wi = wiT_ref[0]

    def tile_start(kt):
        return pl.multiple_of(kt * TK, TK)

    zeros = jnp.zeros((HD, TQ), BF16)
    for h in range(ATT_HEADS):
        qh = qT_ref[0, h * HD:(h + 1) * HD, :]
        g = h // ATT_REP
        wq_ref[h] = jnp.concatenate([zeros] * g + [qh] + [zeros] * (ATT_KV_HEADS - 1 - g), axis=0)
    for h in range(IDX_HEADS):
        wqi_ref[h] = jnp.concatenate([qiT_ref[0, h * IDX_DIM:(h + 1) * IDX_DIM, :], zeros], axis=0)

    @pl.when((pl.program_id(0) == 0) & (i == 0))
    def _():
        planes_ref[...] = jnp.zeros(planes_ref.shape, I32)

    def score_tile(kt):
        ks = tile_start(kt)
        kk = ki_ref[0, pl.ds(ks, TK), :]
        acc = jnp.zeros((TK, TQ), F32)
        for h in range(IDX_HEADS):
            logit = jnp.dot(kk, wqi_ref[h], preferred_element_type=F32)
            acc = acc + wi[h:h + 1, :] * jnp.maximum(logit, 0.0)
        sc_ref[pl.ds(ks, TK), :] = jnp.where(ks + key_in_tile < limit, acc, NEG)

    def score_pair(j, carry):
        score_tile(2 * j)
        score_tile(2 * j + 1)
        return carry

    lax.fori_loop(0, n_kt // 2, score_pair, 0)

    @pl.when(n_kt % 2 == 1)
    def _():
        score_tile(n_kt - 1)

    def plane_tile(kt, carry):
        bits = lax.bitcast_convert_type(sc_ref[pl.ds(tile_start(kt), TK), :], I32)
        ukey = jnp.where(bits < 0, -bits, bits ^ jnp.int32(INT_MIN))
        for g in range(TK // PLANE_KEYS):
            row0 = pl.multiple_of(kt * (TK // 32) + g * SUBLANES, SUBLANES)
            for c in range(TQ // LANES):
                rows = ukey[g * PLANE_KEYS:(g + 1) * PLANE_KEYS, c * LANES:(c + 1) * LANES]
                words = _bit_transpose32([rows[j * SUBLANES:(j + 1) * SUBLANES] for j in range(32)])
                for b in range(32):
                    planes_ref[b, pl.ds(row0, SUBLANES), c * LANES:(c + 1) * LANES] = words[31 - b]
        return carry

    lax.fori_loop(0, n_kt, plane_tile, 0)

    def key_to_float(key):
        bits = jnp.where(key < 0, jnp.int32(INT_MIN) - key, key)
        return lax.bitcast_convert_type(bits, F32)

    def popcount_rows(w):
        return jnp.sum(lax.population_count(w), axis=0, keepdims=True)

    def radix_select(n_words):
        word_row = lax.broadcasted_iota(I32, (n_words, TQ), 0)
        cand0 = jnp.where(word_row < n_kt * (TK // 32), jnp.int32(-1), jnp.int32(0))

        def bit_body(it, carry):
            cand, need, ukey_thr = carry
            b = 31 - it
            ones = cand & planes_ref[b, :n_words, :]
            c1 = popcount_rows(ones)
            ok = c1 >= need
            cand = jnp.where(ok, ones, cand ^ ones)
            need = jnp.where(ok, need, need - c1)
            ukey_thr = ukey_thr | jnp.left_shift(ok.astype(I32), b)
            return cand, need, ukey_thr

        cand, need, ukey_thr = lax.fori_loop(
            0, 32, bit_body, (cand0, jnp.full((1, TQ), topk, I32), jnp.zeros((1, TQ), I32)))
        thr_key = ukey_thr ^ jnp.int32(INT_MIN)
        cnt_key = (topk - need) + popcount_rows(cand)
        thr_ref[...] = jnp.where(thr_key == INT_MIN, -jnp.inf, key_to_float(thr_key))
        cnt_ref[...] = cnt_key.astype(F32)

    all_words = planes_ref.shape[1]
    half_words = all_words // 2
    few_keys = n_kt * (TK // 32) <= half_words

    @pl.when(few_keys)
    def _():
        radix_select(half_words)

    @pl.when(jnp.logical_not(few_keys))
    def _():
        radix_select(all_words)

    def count(pred):
        def body(kt, cnt):
            m = jnp.where(pred(sc_ref[pl.ds(tile_start(kt), TK), :]), 1.0, 0.0)
            return cnt + m.reshape(TK // FCOUNT_ROWS, FCOUNT_ROWS, TQ).sum(axis=0)
        cnt = lax.fori_loop(0, n_kt, body, jnp.zeros((FCOUNT_ROWS, TQ), F32))
        return jnp.sum(cnt, axis=0, keepdims=True)

    def plain_bias():
        thr = thr_ref[...]

        def bias_tile(kt, cnt):
            ks = tile_start(kt)
            above = sc_ref[pl.ds(ks, TK), :] >= thr
            bias_ref[pl.ds(ks, TK), :] = jnp.where(above & (ks + key_in_tile < limit), 0.0, NEG)
            m = jnp.where(above, 1.0, 0.0)
            return cnt + m.reshape(TK // FCOUNT_ROWS, FCOUNT_ROWS, TQ).sum(axis=0)
        cnt = lax.fori_loop(0, n_kt, bias_tile, jnp.zeros((FCOUNT_ROWS, TQ), F32))
        return jnp.sum(cnt, axis=0, keepdims=True)

    disagree = jnp.max(jnp.abs(plain_bias() - cnt_ref[...])) > 0.0

    @pl.when(disagree)
    def _():
        def body(it, carry):
            lo, cnt_lo = carry
            cand = lo + jnp.left_shift(jnp.int32(1), 31 - it)
            cand_f = key_to_float(cand)
            c = count(lambda s: s >= cand_f)
            ok = c >= float(topk)
            return jnp.where(ok, cand, lo), jnp.where(ok, c, cnt_lo)
        lo, cnt_lo = lax.fori_loop(0, 32, body, (jnp.full((1, TQ), INT_MIN, I32),
                                                 jnp.full((1, TQ), n_kt * TK, I32).astype(F32)))
        thr_ref[...] = jnp.where(lo == INT_MIN, -jnp.inf, key_to_float(lo))
        cnt_ref[...] = cnt_lo

    thr = thr_ref[...]
    has_ties = jnp.max(cnt_ref[...]) > float(topk)

    @pl.when(disagree & jnp.logical_not(has_ties))
    def _():
        plain_bias()

    @pl.when(has_ties)
    def _():
        need = float(topk) - count(lambda s: s > thr)
        ri = lax.broadcasted_iota(I32, (TK, TK), 0)
        ci = lax.broadcasted_iota(I32, (TK, TK), 1)
        tri = jnp.where(ci < ri, 1.0, 0.0).astype(BF16)

        def bias_tile(kt, run):
            ks = tile_start(kt)
            s = sc_ref[pl.ds(ks, TK), :]
            eq = jnp.where(s == thr, 1.0, 0.0)
            before = jnp.dot(tri, eq.astype(BF16), preferred_element_type=F32) + run
            sel = (s > thr) | ((s == thr) & (before < need))
            sel = sel & (ks + key_in_tile < limit)
            bias_ref[pl.ds(ks, TK), :] = jnp.where(sel, 0.0, NEG)
            return run + jnp.sum(eq, axis=0, keepdims=True)
        lax.fori_loop(0, n_kt, bias_tile, jnp.zeros((1, TQ), F32))

    m_ref[...] = jnp.full(m_ref.shape, -3e38, F32)
    acc_ref[...] = jnp.zeros(acc_ref.shape, F32)

    def scores(kt, slot):
        kk = k_ref[0, pl.ds(tile_start(jnp.minimum(kt, n_kt - 1)), TK), :]
        bias = bias_ref[pl.ds(tile_start(jnp.minimum(kt, n_kt)), TK), :]
        for h in range(ATT_HEADS):
            s_ref[slot, h] = jnp.dot(kk, wq_ref[h], preferred_element_type=F32) + bias

    def softmax(slot):
        for h in range(ATT_HEADS):
            for c in range(TQ // LANES):
                ql = slice(c * LANES, (c + 1) * LANES)
                s = s_ref[slot, h, :, ql]
                m_old = m_ref[h:h + 1, ql]
                m_new = jnp.maximum(m_old, jnp.max(s, axis=0, keepdims=True))
                p_ref[slot, h, :, ql] = jnp.exp2(s - m_new).astype(BF16)
                alpha_ref[slot, h:h + 1, ql] = jnp.exp2(m_old - m_new)
                m_ref[h:h + 1, ql] = m_new

    def weighted_values(kt, slot):
        ks = tile_start(jnp.clip(kt, 0, n_kt - 1))
        for h in range(ATT_HEADS):
            g = h // ATT_REP
            pv = jnp.dot(vT_ref[0, g * V_ROWS:(g + 1) * V_ROWS, pl.ds(ks, TK)], p_ref[slot, h],
                         preferred_element_type=F32)
            acc_ref[h] = alpha_ref[slot, h:h + 1, :] * acc_ref[h] + pv

    bias_ref[pl.ds(tile_start(n_kt), TK), :] = jnp.full((TK, TQ), NEG, F32)
    scores(0, 0)
    p_ref[1] = jnp.zeros(p_ref.shape[1:], BF16)
    alpha_ref[1] = jnp.ones(alpha_ref.shape[1:], F32)

    def att_pair(j, carry):
        scores(2 * j + 1, 1)
        softmax(0)
        weighted_values(2 * j - 1, 1)
        scores(2 * j + 2, 0)
        softmax(1)
        weighted_values(2 * j, 0)
        return carry

    n_pairs = n_kt // 2
    lax.fori_loop(0, n_pairs, att_pair, 0)
    weighted_values(2 * n_pairs - 1, 1)

    @pl.when(n_kt % 2 == 1)
    def _():
        softmax(0)
        weighted_values(n_kt - 1, 0)

    for c in range(ATT_HEADS // 2):
        pair = [acc_ref[h, :HD, :] / acc_ref[h, HD:HD + 1, :] for h in (2 * c, 2 * c + 1)]
        o_ref[0, :, c * LANES:(c + 1) * LANES] = jnp.concatenate(pair, axis=0).T.astype(BF16)


def _dsa(qT, qiT, wiT, ki, k, vT):
    B, QW, S = qT.shape
    TQ = DSA_TQ
    topk = min(TOPK_MAX, S // 4)
    assert S % TQ == 0 and TQ == DSA_TK and DSA_TK >= topk
    return pl.pallas_call(
        functools.partial(_dsa_kernel, topk=topk),
        out_shape=jax.ShapeDtypeStruct((B, S, QW), BF16),
        grid=(B, S // TQ),
        in_specs=[
            pl.BlockSpec((1, QW, TQ), lambda b, i: (b, 0, i)),
            pl.BlockSpec((1, IDX_HEADS * IDX_DIM, TQ), lambda b, i: (b, 0, i)),
            pl.BlockSpec((1, IDX_HEADS, TQ), lambda b, i: (b, 0, i)),
            pl.BlockSpec((1, S, LANES), lambda b, i: (b, 0, 0)),
            pl.BlockSpec((1, S, LANES), lambda b, i: (b, 0, 0)),
            pl.BlockSpec((1, ATT_KV_HEADS * V_ROWS, S), lambda b, i: (b, 0, 0)),
        ],
        out_specs=pl.BlockSpec((1, TQ, QW), lambda b, i: (b, i, 0)),
        scratch_shapes=[
            pltpu.VMEM((ATT_HEADS, LANES, TQ), BF16),
            pltpu.VMEM((IDX_HEADS, LANES, TQ), BF16),
            pltpu.VMEM((S, TQ), F32),
            pltpu.VMEM((32, S // 32, TQ), I32),
            pltpu.VMEM((S + DSA_TK, TQ), F32),
            pltpu.VMEM((ATT_HEADS, TQ), F32),
            pltpu.VMEM((ATT_HEADS, V_ROWS, TQ), F32),
            pltpu.VMEM((2, ATT_HEADS, TQ), F32),
            pltpu.VMEM((2, ATT_HEADS, DSA_TK, TQ), F32),
            pltpu.VMEM((2, ATT_HEADS, DSA_TK, TQ), BF16),
            pltpu.VMEM((1, TQ), F32),
            pltpu.VMEM((1, TQ), F32),
        ],
        compiler_params=_params("arbitrary", "arbitrary"),
        name="dsa",
    )(qT, qiT, wiT, ki, k, vT)


S5_SLABS = S5_WIDTH // LANES
S5_SLAB_STATES = (LANES // S5_GROUP) * S5_STATE


def _s5_param_kernel(lre_ref, lim_ref, ldt_ref, bre_ref, bim_ref,
                     lbre_ref, lbim_ref, bbre_ref, bbim_ref):
    lre = jnp.minimum(lre_ref[...], -1e-4)
    lim = lim_ref[...]
    dt = jnp.exp(ldt_ref[...])
    mag = jnp.exp(lre * dt)
    lbre = mag * jnp.cos(lim * dt)
    lbim = mag * jnp.sin(lim * dt)
    nre = lbre - 1.0
    den = lre * lre + lim * lim
    cre = (nre * lre + lbim * lim) / den
    cim = (lbim * lre - nre * lim) / den
    lbre_ref[...] = lbre
    lbim_ref[...] = lbim
    bbre_ref[...] = cre * bre_ref[...] - cim * bim_ref[...]
    bbim_ref[...] = cre * bim_ref[...] + cim * bre_ref[...]


def _s5_params(lam_re, lam_im, log_dt, b_re, b_im, c_re, c_im):
    G, P, H = S5_GROUPS, S5_STATE, S5_GROUP
    n = G * P
    flat = lambda a: a.astype(F32).reshape(1, n)
    ldt = jnp.repeat(log_dt.astype(F32), P).reshape(1, n)
    bt = lambda a: a.astype(F32).transpose(2, 0, 1).reshape(H, n)
    lbre, lbim, bbre, bbim = pl.pallas_call(
        _s5_param_kernel,
        out_shape=(jax.ShapeDtypeStruct((1, n), F32), jax.ShapeDtypeStruct((1, n), F32),
                   jax.ShapeDtypeStruct((H, n), F32), jax.ShapeDtypeStruct((H, n), F32)),
        name="s5_params",
    )(flat(lam_re), flat(lam_im), ldt, bt(b_re), bt(b_im))

    gl = LANES // H
    eye = jnp.eye(gl, dtype=F32)
    bb = jnp.stack([bbre, bbim]).reshape(2, H, S5_SLABS, gl, P)
    wb = jnp.einsum("ahjgp,gk->jghakp", bb, eye).reshape(S5_SLABS, LANES, 2 * gl * P)
    cc = jnp.stack([c_re.astype(F32), -c_im.astype(F32)]).reshape(2, S5_SLABS, gl, H, P)
    wc = jnp.einsum("ajghp,gk->jagpkh", cc, eye).reshape(S5_SLABS, 2 * gl * P, LANES)
    lam = lambda a: jnp.broadcast_to(a.reshape(S5_SLABS, 1, gl * P), (S5_SLABS, SUBLANES, gl * P))
    return lam(lbre), lam(lbim), wb.astype(BF16), wc.astype(BF16)


def _s5_kernel(u_ref, lbre_ref, lbim_ref, wb_ref, wc_ref, d_ref, wglu_ref, bglu_ref, o_ref,
               state_ref, h_ref, y_ref, u_sc, hb_ref, *, steps, batch):
    NS = S5_SLAB_STATES
    t = pl.program_id(0)

    @pl.when(t == 0)
    def _():
        state_ref[...] = jnp.zeros(state_ref.shape, F32)

    W = S5_WIDTH
    for b in range(batch):
        for j in range(S5_SLABS):
            c0 = b * W + j * LANES
            u_sc[j, pl.ds(b, steps, stride=batch), :] = u_ref[:, c0:c0 + LANES]
    for j in range(S5_SLABS):
        h_ref[j] = jnp.dot(u_sc[j].astype(BF16), wb_ref[j], preferred_element_type=F32)

    for j in range(S5_SLABS):
        lre = lbre_ref[j]
        lim = lbim_ref[j]

        def step2(s2, carry):
            hre, him = carry
            r0 = pl.multiple_of(s2 * (2 * batch), 2 * batch)
            bre = h_ref[j, pl.ds(r0, 2 * batch), :NS]
            bim = h_ref[j, pl.ds(r0, 2 * batch), NS:]
            are = lre * hre - lim * him + bre[:batch]
            aim = lre * him + lim * hre + bim[:batch]
            nre = lre * are - lim * aim + bre[batch:]
            nim = lre * aim + lim * are + bim[batch:]
            hb_ref[j, pl.ds(r0, 2 * batch), :NS] = jnp.concatenate([are, nre], axis=0).astype(BF16)
            hb_ref[j, pl.ds(r0, 2 * batch), NS:] = jnp.concatenate([aim, nim], axis=0).astype(BF16)
            return nre, nim

        hre, him = lax.fori_loop(0, steps // 2, step2, (state_ref[j, :, :NS], state_ref[j, :, NS:]),
                                 unroll=2)
        state_ref[j, :, :NS] = hre
        state_ref[j, :, NS:] = him

    ys = [jnp.dot(hb_ref[j], wc_ref[j], preferred_element_type=F32) for j in range(S5_SLABS)]
    u = jnp.concatenate([u_sc[j] for j in range(S5_SLABS)], axis=1)
    y = jnp.concatenate(ys, axis=1) + d_ref[...] * u
    y = jax.nn.gelu(y)
    gate = jnp.dot(y.astype(BF16), wglu_ref[...], preferred_element_type=F32) + bglu_ref[...]
    y = y * jax.nn.sigmoid(gate)
    for j in range(S5_SLABS):
        y_ref[j] = y[:, j * LANES:(j + 1) * LANES]
    for b in range(batch):
        for j in range(S5_SLABS):
            o_ref[b, :, j * LANES:(j + 1) * LANES] = (
                y_ref[j, pl.ds(b, steps, stride=batch), :].astype(BF16))


def _s5(u2, lbre, lbim, wb, wc, d_skip, w_glu, b_glu, batch, steps):
    S, W = u2.shape[0], S5_WIDTH
    NS2 = 2 * S5_SLAB_STATES
    const = lambda *shape: pl.BlockSpec(shape, lambda t: (0,) * len(shape))
    return pl.pallas_call(
        functools.partial(_s5_kernel, steps=steps, batch=batch),
        out_shape=jax.ShapeDtypeStruct((batch, S, W), BF16),
        grid=(S // steps,),
        in_specs=[
            pl.BlockSpec((steps, batch * W), lambda t: (t, 0)),
            const(S5_SLABS, SUBLANES, S5_SLAB_STATES),
            const(S5_SLABS, SUBLANES, S5_SLAB_STATES),
            const(S5_SLABS, LANES, NS2),
            const(S5_SLABS, NS2, LANES),
            const(1, W),
            const(W, W),
            const(1, W),
        ],
        out_specs=pl.BlockSpec((batch, steps, W), lambda t: (0, t, 0)),
        scratch_shapes=[
            pltpu.VMEM((S5_SLABS, batch, NS2), F32),
            pltpu.VMEM((S5_SLABS, steps * batch, NS2), F32),
            pltpu.VMEM((S5_SLABS, steps * batch, LANES), F32),
            pltpu.VMEM((S5_SLABS, steps * batch, LANES), F32),
            pltpu.VMEM((S5_SLABS, steps * batch, NS2), BF16),
        ],
        compiler_params=_params("arbitrary"),
        name="s5",
    )(u2, lbre, lbim, wb, wc, d_skip.astype(F32).reshape(1, W), w_glu.astype(BF16),
      b_glu.astype(F32).reshape(1, W))


def _layer_norm(z, g, b):
    mu = jnp.mean(z, axis=-1, keepdims=True)
    zc = z - mu
    var = jnp.mean(zc * zc, axis=-1, keepdims=True)
    return zc * lax.rsqrt(var + LN_EPS) * g + b


def _merge_kernel(x_ref, a_ref, yb_ref, qm_ref, mem_ref, wkv_ref, wg_ref, bg_ref,
                  wpa_ref, wpb_ref, wpc_ref, wout_ref, g_ref, b_ref, o_ref, kv_ref, *, alpha):
    D = x_ref.shape[2]
    HW = MEM_HEADS * MEM_HEAD_DIM

    @pl.when(pl.program_id(1) == 0)
    def _():
        kv_ref[...] = jnp.dot(mem_ref[0].astype(BF16), wkv_ref[...],
                              preferred_element_type=F32).astype(BF16)

    x = x_ref[0]
    xb = x.astype(BF16)
    qm = qm_ref[0]
    heads = []
    for h in range(MEM_HEADS):
        sl = slice(h * MEM_HEAD_DIM, (h + 1) * MEM_HEAD_DIM)
        s = lax.dot_general(qm[:, sl], kv_ref[:, sl], (((1,), (1,)), ((), ())),
                            preferred_element_type=F32)
        p = jnp.exp(s - jnp.max(s, axis=1, keepdims=True))
        p = p / jnp.sum(p, axis=1, keepdims=True)
        heads.append(jnp.dot(p.astype(BF16), kv_ref[:, HW + h * MEM_HEAD_DIM:HW + (h + 1) * MEM_HEAD_DIM],
                             preferred_element_type=F32))
    c_in = jnp.concatenate(heads, axis=1).astype(BF16)

    y_a = jnp.dot(a_ref[0], wpa_ref[...], preferred_element_type=F32)
    y_b = jnp.dot(yb_ref[0], wpb_ref[...], preferred_element_type=F32)
    y_c = jnp.dot(c_in, wpc_ref[...], preferred_element_type=F32)
    merged = None
    for k, yk in enumerate((y_a, y_b, y_c)):
        gk = jax.nn.sigmoid(jnp.dot(xb, wg_ref[:, k * D:(k + 1) * D], preferred_element_type=F32)
                            + bg_ref[:, k * D:(k + 1) * D])
        merged = gk * yk if merged is None else merged + gk * yk
    mix = jnp.dot(merged.astype(BF16), wout_ref[...], preferred_element_type=F32)
    o_ref[0] = _layer_norm(alpha * x + mix, g_ref[...], b_ref[...])


def _merge(x, att, yb, qm, mem, w_mem_kv, w_gate, b_gate, w_proj_a, w_proj_b, w_proj_c, w_out,
           ln_g, ln_b, alpha, tile):
    B, S, D = x.shape
    n_mem = mem.shape[1]
    HW = MEM_HEADS * MEM_HEAD_DIM
    const = lambda *shape: pl.BlockSpec(shape, lambda b, t: (0,) * len(shape))
    row = lambda width: pl.BlockSpec((1, tile, width), lambda b, t: (b, t, 0))
    return pl.pallas_call(
        functools.partial(_merge_kernel, alpha=alpha),
        out_shape=jax.ShapeDtypeStruct((B, S, D), F32),
        grid=(B, S // tile),
        in_specs=[
            row(D), row(att.shape[2]), row(yb.shape[2]), row(HW),
            pl.BlockSpec((1, n_mem, D), lambda b, t: (b, 0, 0)),
            const(D, 2 * HW), const(D, N_BRANCH * D), const(1, N_BRANCH * D),
            const(att.shape[2], D), const(yb.shape[2], D), const(HW, D), const(D, D),
            const(1, D), const(1, D),
        ],
        out_specs=row(D),
        scratch_shapes=[pltpu.VMEM((n_mem, 2 * HW), BF16)],
        compiler_params=_params("parallel", "arbitrary"),
        name="merge",
    )(x, att, yb, qm, mem, w_mem_kv.astype(BF16), w_gate.astype(BF16),
      b_gate.astype(F32).reshape(1, -1), w_proj_a.astype(BF16), w_proj_b.astype(BF16),
      w_proj_c.astype(BF16), w_out.astype(BF16), ln_g.astype(F32).reshape(1, D),
      ln_b.astype(F32).reshape(1, D))


FFN_CHUNK = 256


def _ffn_kernel(h_ref, wup_ref, cw_ref, cb_ref, wdn_ref, g_ref, b_ref, o_ref,
                act_ref, tail_ref, *, alpha, ffn_dim):
    T = h_ref.shape[1]
    FC = FFN_CHUNK
    HALO = SUBLANES

    @pl.when(pl.program_id(1) == 0)
    def _():
        tail_ref[...] = jnp.zeros(tail_ref.shape, F32)

    h = h_ref[0]
    hb = h.astype(BF16)
    halo_row = lax.broadcasted_iota(I32, (HALO, FC), 0)

    def conv_cols(c0):
        up = jnp.dot(hb, wup_ref[:, c0:c0 + FC], preferred_element_type=F32)
        tail = tail_ref[:, c0:c0 + FC]
        tail_ref[:, c0:c0 + FC] = up[T - HALO:]
        out = cb_ref[:, c0:c0 + FC] + cw_ref[CONV_WIDTH - 1:CONV_WIDTH, c0:c0 + FC] * up
        for d in range(1, CONV_WIDTH):
            rolled = pltpu.roll(up, d, 0)
            top = jnp.where(halo_row < d, pltpu.roll(tail, d, 0), rolled[:HALO])
            delayed = jnp.concatenate([top, rolled[HALO:]], axis=0)
            out = out + cw_ref[CONV_WIDTH - 1 - d:CONV_WIDTH - d, c0:c0 + FC] * delayed
        return out

    for c in range(ffn_dim // FC):
        gate = conv_cols(c * FC)
        up = conv_cols(ffn_dim + c * FC)
        act_ref[:, c * FC:(c + 1) * FC] = (gate * jax.nn.sigmoid(gate) * up).astype(BF16)
    f = jnp.dot(act_ref[...], wdn_ref[...], preferred_element_type=F32)
    o_ref[0] = _layer_norm(alpha * h + f, g_ref[...], b_ref[...])


def _ffn(h, w_up, conv_w, conv_b, w_down, ln_g, ln_b, alpha, tile):
    B, S, D = h.shape
    F2 = w_up.shape[1]
    const = lambda *shape: pl.BlockSpec(shape, lambda b, t: (0,) * len(shape))
    row = pl.BlockSpec((1, tile, D), lambda b, t: (b, t, 0))
    return pl.pallas_call(
        functools.partial(_ffn_kernel, alpha=alpha, ffn_dim=F2 // 2),
        out_shape=jax.ShapeDtypeStruct((B, S, D), F32),
        grid=(B, S // tile),
        in_specs=[row, const(D, F2), const(CONV_WIDTH, F2), const(1, F2), const(F2 // 2, D),
                  const(1, D), const(1, D)],
        out_specs=row,
        scratch_shapes=[
            pltpu.VMEM((tile, F2 // 2), BF16),
            pltpu.VMEM((SUBLANES, F2), F32),
        ],
        compiler_params=_params("parallel", "arbitrary"),
        name="ffn",
    )(h, w_up.astype(BF16), conv_w.astype(F32), conv_b.astype(F32).reshape(1, F2),
      w_down.astype(BF16), ln_g.astype(F32).reshape(1, D), ln_b.astype(F32).reshape(1, D))


PROJ_TILE = 512
S5_STEPS = 64
MERGE_TILE = 512
FFN_TILE = 512


def kernel(x, mem, positions, w_in, w_gate, b_gate, s5_lam_re, s5_lam_im, s5_log_dt, s5_b_re, s5_b_im, s5_c_re, s5_c_im, s5_d, w_glu, b_glu, w_mem_kv, w_proj_a, w_proj_b, w_proj_c, w_out, ln1_g, ln1_b, w_up, conv_w, conv_b, w_down, ln2_g, ln2_b):
    B, S, D = x.shape
    depth = w_in.shape[0]
    alpha = (2.0 * depth) ** 0.25
    h = x
    for l in range(depth):
        qT, k, vT, qiT, ki, wiT, u, qm = _proj(h, positions, w_in[l], min(PROJ_TILE, S))
        att = _dsa(qT, qiT, wiT, ki, k, vT)
        lbre, lbim, wb, wc = _s5_params(s5_lam_re[l], s5_lam_im[l], s5_log_dt[l], s5_b_re[l],
                                        s5_b_im[l], s5_c_re[l], s5_c_im[l])
        yb = _s5(u, lbre, lbim, wb, wc, s5_d[l], w_glu[l], b_glu[l], B, min(S5_STEPS, S))
        h = _merge(h, att, yb, qm, mem, w_mem_kv[l], w_gate[l], b_gate[l], w_proj_a[l],
                   w_proj_b[l], w_proj_c[l], w_out[l], ln1_g[l], ln1_b[l], alpha, min(MERGE_TILE, S))
        h = _ffn(h, w_up[l], conv_w[l], conv_b[l], w_down[l], ln2_g[l], ln2_b[l], alpha,
                 min(FFN_TILE, S))
    return h
```

```python
import functools
import math

import jax
import jax.numpy as jnp
from jax import lax
from jax.experimental import pallas as pl
from jax.experimental.pallas import tpu as pltpu

F32 = jnp.float32
BF16 = jnp.bfloat16
I32 = jnp.int32

CHUNK = 64
ATT_HEADS = 8
ATT_KV_HEADS = 2
ATT_HEAD_DIM = 64
ATT_REP = ATT_HEADS // ATT_KV_HEADS
IDX_HEADS = 4
IDX_DIM = 64
TOPK_MAX = 256
ROPE_THETA = 10000.0
S5_WIDTH = 512
S5_GROUP = 16
S5_GROUPS = S5_WIDTH // S5_GROUP
S5_STATE = 64
MEM_HEADS = 4
MEM_HEAD_DIM = 128
CONV_WIDTH = 3
N_BRANCH = 3
LN_EPS = 1e-5
NEG = -1e30
INT_MIN = -(2 ** 31)
LOG2E = math.log2(math.e)

LANES = 128
SUBLANES = 8
VMEM_LIMIT = 56 * 1024 * 1024

_Q0 = 0
_K0 = _Q0 + ATT_HEADS * ATT_HEAD_DIM
_V0 = _K0 + ATT_KV_HEADS * ATT_HEAD_DIM
_QI0 = _V0 + ATT_KV_HEADS * ATT_HEAD_DIM
_KI0 = _QI0 + IDX_HEADS * IDX_DIM
_WI0 = _KI0 + IDX_DIM
_U0 = _WI0 + IDX_HEADS
_QM0 = _U0 + S5_WIDTH
_END = _QM0 + MEM_HEADS * MEM_HEAD_DIM
_PAD = LANES - IDX_DIM - IDX_HEADS
_PU0 = _KI0 + LANES
_PQM0 = _PU0 + S5_WIDTH
_PEND = _PQM0 + MEM_HEADS * MEM_HEAD_DIM
V_ROWS = ATT_HEAD_DIM + 16


def _params(*sem):
    return pltpu.CompilerParams(dimension_semantics=sem, vmem_limit_bytes=VMEM_LIMIT)


def _proj_kernel(x_ref, pos_ref, invf_ref, w_ref,
                 qT_ref, k_ref, vT_ref, qiT_ref, ki_ref, wiT_ref, u_ref, qm_ref):
    T = x_ref.shape[1]
    y = jnp.dot(x_ref[0].astype(BF16), w_ref[...], preferred_element_type=F32)

    ang = pos_ref[0].astype(F32) * invf_ref[...]
    cos = jnp.cos(ang)
    sin = jnp.sin(ang)
    lane = lax.broadcasted_iota(I32, (T, LANES), 1)
    first = (lane % ATT_HEAD_DIM) < (ATT_HEAD_DIM // 2)
    sin = jnp.where(first, -sin, sin)

    def rope(z):
        partner = jnp.where(first, pltpu.roll(z, LANES - 32, 1), pltpu.roll(z, 32, 1))
        return z * cos + partner * sin

    att_scale = ATT_HEAD_DIM ** -0.5 * LOG2E
    for c in range(ATT_HEADS // 2):
        z = rope(y[:, _Q0 + c * LANES:_Q0 + (c + 1) * LANES]) * att_scale
        qT_ref[0, c * LANES:(c + 1) * LANES, :] = z.T.astype(BF16)
    k_ref[0] = rope(y[:, _K0:_K0 + LANES]).astype(BF16)
    vT = y[:, _V0:_V0 + LANES].T.astype(BF16)
    for g in range(ATT_KV_HEADS):
        vT_ref[0, g * V_ROWS:g * V_ROWS + ATT_HEAD_DIM, :] = vT[g * ATT_HEAD_DIM:(g + 1) * ATT_HEAD_DIM]
        vT_ref[0, g * V_ROWS + ATT_HEAD_DIM:(g + 1) * V_ROWS, :] = jnp.ones((V_ROWS - ATT_HEAD_DIM, T), BF16)
    idx_scale = IDX_DIM ** -0.5
    for c in range(IDX_HEADS // 2):
        z = rope(y[:, _QI0 + c * LANES:_QI0 + (c + 1) * LANES]) * idx_scale
        qiT_ref[0, c * LANES:(c + 1) * LANES, :] = z.T.astype(BF16)
    kw = y[:, _KI0:_KI0 + LANES]
    ki_ref[0] = jnp.where(lane < IDX_DIM, rope(kw), 0.0).astype(BF16)
    wiT_ref[0] = kw.T[IDX_DIM:IDX_DIM + IDX_HEADS] * (IDX_HEADS ** -0.5)
    u_ref[...] = y[:, _PU0:_PU0 + S5_WIDTH]
    qm_ref[0] = (y[:, _PQM0:_PEND] * (MEM_HEAD_DIM ** -0.5)).astype(BF16)


def _proj(x, positions, w_in, tile):
    B, S, D = x.shape
    nt = S // tile
    w_pad = jnp.concatenate(
        [w_in[:, :_U0], jnp.zeros((D, _PAD), w_in.dtype), w_in[:, _U0:]], axis=1).astype(BF16)
    half = ATT_HEAD_DIM // 2
    inv_freq = ROPE_THETA ** (-jnp.arange(half, dtype=F32) / half)
    invf = jnp.tile(inv_freq, LANES // half)[None, :]
    pos3 = positions.reshape(B, S, 1)
    QW = ATT_HEADS * ATT_HEAD_DIM
    IW = IDX_HEADS * IDX_DIM
    out_shape = (
        jax.ShapeDtypeStruct((B, QW, S), BF16),
        jax.ShapeDtypeStruct((B, S, LANES), BF16),
        jax.ShapeDtypeStruct((B, ATT_KV_HEADS * V_ROWS, S), BF16),
        jax.ShapeDtypeStruct((B, IW, S), BF16),
        jax.ShapeDtypeStruct((B, S, LANES), BF16),
        jax.ShapeDtypeStruct((B, IDX_HEADS, S), F32),
        jax.ShapeDtypeStruct((S, B * S5_WIDTH), F32),
        jax.ShapeDtypeStruct((B, S, MEM_HEADS * MEM_HEAD_DIM), BF16),
    )
    rows = lambda width: pl.BlockSpec((1, tile, width), lambda b, t: (b, t, 0))
    cols = lambda height: pl.BlockSpec((1, height, tile), lambda b, t: (b, 0, t))
    out_specs = (
        cols(QW), rows(LANES), cols(ATT_KV_HEADS * V_ROWS), cols(IW), rows(LANES), cols(IDX_HEADS),
        pl.BlockSpec((tile, S5_WIDTH), lambda b, t: (t, b)),
        rows(MEM_HEADS * MEM_HEAD_DIM),
    )
    return pl.pallas_call(
        _proj_kernel,
        out_shape=out_shape,
        grid=(B, nt),
        in_specs=[
            pl.BlockSpec((1, tile, D), lambda b, t: (b, t, 0)),
            pl.BlockSpec((1, tile, 1), lambda b, t: (b, t, 0)),
            pl.BlockSpec((1, LANES), lambda b, t: (0, 0)),
            pl.BlockSpec((D, _PEND), lambda b, t: (0, 0)),
        ],
        out_specs=out_specs,
        compiler_params=_params("parallel", "parallel"),
        name="proj",
    )(x, pos3, invf, w_pad)


DSA_TQ = 256
DSA_TK = 512
PLANE_KEYS = 32 * SUBLANES
FCOUNT_ROWS = 4 * SUBLANES


def _bit_transpose32(words):
    a = list(words)
    j, mask = 16, 0x0000FFFF
    while j:
        k = 0
        while k < 32:
            t = (a[k] ^ lax.shift_right_logical(a[k + j], jnp.int32(j))) & jnp.int32(mask)
            a[k] = a[k] ^ t
            a[k + j] = a[k + j] ^ lax.shift_left(t, jnp.int32(j))
            k = (k + j + 1) & ~j
        j >>= 1
        mask = (mask ^ (mask << j)) & 0xFFFFFFFF
    return a


def _dsa_kernel(qT_ref, qiT_ref, wiT_ref, ki_ref, k_ref, vT_ref, o_ref,
                wq_ref, wqi_ref, sc_ref, planes_ref, bias_ref, m_ref, acc_ref, alpha_ref,
                s_ref, p_ref, thr_ref, cnt_ref,
                *, topk):
    TQ, TK = DSA_TQ, DSA_TK
    HD = ATT_HEAD_DIM
    i = pl.program_id(1)
    n_kt = (i * TQ + TQ + TK - 1) // TK
    key_in_tile = lax.broadcasted_iota(I32, (TK, TQ), 0)
    qpos = i * TQ + lax.broadcasted_iota(I32, (1, TQ), 1)
    limit = (qpos // CHUNK + 1) * CHUNK
    wi = wiT_ref[0]

    def tile_start(kt):
        return pl.multiple_of(kt * TK, TK)

    zeros = jnp.zeros((HD, TQ), BF16)
    for h in range(ATT_HEADS):
        qh = qT_ref[0, h * HD:(h + 1) * HD, :]
        g = h // ATT_REP
        wq_ref[h] = jnp.concatenate([zeros] * g + [qh] + [zeros] * (ATT_KV_HEADS - 1 - g), axis=0)
    for h in range(IDX_HEADS):
        wqi_ref[h] = jnp.concatenate([qiT_ref[0, h * IDX_DIM:(h + 1) * IDX_DIM, :], zeros], axis=0)

    @pl.when((pl.program_id(0) == 0) & (i == 0))
    def _():
        planes_ref[...] = jnp.zeros(planes_ref.shape, I32)

    def score_tile(kt):
        ks = tile_start(kt)
        kk = ki_ref[0, pl.ds(ks, TK), :]
        acc = jnp.zeros((TK, TQ), F32)
        for h in range(IDX_HEADS):
            logit = jnp.dot(kk, wqi_ref[h], preferred_element_type=F32)
            acc = acc + wi[h:h + 1, :] * jnp.maximum(logit, 0.0)
        sc_ref[pl.ds(ks, TK), :] = jnp.where(ks + key_in_tile < limit, acc, NEG)

    def score_pair(j, carry):
        score_tile(2 * j)
        score_tile(2 * j + 1)
        return carry

    lax.fori_loop(0, n_kt // 2, score_pair, 0)

    @pl.when(n_kt % 2 == 1)
    def _():
        score_tile(n_kt - 1)

    def plane_tile(kt, carry):
        bits = lax.bitcast_convert_type(sc_ref[pl.ds(tile_start(kt), TK), :], I32)
        ukey = jnp.where(bits < 0, -bits, bits ^ jnp.int32(INT_MIN))
        for g in range(TK // PLANE_KEYS):
            row0 = pl.multiple_of(kt * (TK // 32) + g * SUBLANES, SUBLANES)
            for c in range(TQ // LANES):
                rows = ukey[g * PLANE_KEYS:(g + 1) * PLANE_KEYS, c * LANES:(c + 1) * LANES]
                words = _bit_transpose32([rows[j * SUBLANES:(j + 1) * SUBLANES] for j in range(32)])
                for b in range(32):
                    planes_ref[b, pl.ds(row0, SUBLANES), c * LANES:(c + 1) * LANES] = words[31 - b]
        return carry

    lax.fori_loop(0, n_kt, plane_tile, 0)

    def key_to_float(key):
        bits = jnp.where(key < 0, jnp.int32(INT_MIN) - key, key)
        return lax.bitcast_convert_type(bits, F32)

    def popcount_rows(w):
        return jnp.sum(lax.population_count(w), axis=0, keepdims=True)

    def radix_select(n_words):
        word_row = lax.broadcasted_iota(I32, (n_words, TQ), 0)
        cand0 = jnp.where(word_row < n_kt * (TK // 32), jnp.int32(-1), jnp.int32(0))

        def bit_body(it, carry):
            cand, need, ukey_thr = carry
            b = 31 - it
            ones = cand & planes_ref[b, :n_words, :]
            c1 = popcount_rows(ones)
            ok = c1 >= need
            cand = jnp.where(ok, ones, cand ^ ones)
            need = jnp.where(ok, need, need - c1)
            ukey_thr = ukey_thr | jnp.left_shift(ok.astype(I32), b)
            return cand, need, ukey_thr

        cand, need, ukey_thr = lax.fori_loop(
            0, 32, bit_body, (cand0, jnp.full((1, TQ), topk, I32), jnp.zeros((1, TQ), I32)))
        thr_key = ukey_thr ^ jnp.int32(INT_MIN)
        cnt_key = (topk - need) + popcount_rows(cand)
        thr_ref[...] = jnp.where(thr_key == INT_MIN, -jnp.inf, key_to_float(thr_key))
        cnt_ref[...] = cnt_key.astype(F32)

    all_words = planes_ref.shape[1]
    half_words = all_words // 2
    few_keys = n_kt * (TK // 32) <= half_words

    @pl.when(few_keys)
    def _():
        radix_select(half_words)

    @pl.when(jnp.logical_not(few_keys))
    def _():
        radix_select(all_words)

    def count(pred):
        def body(kt, cnt):
            m = jnp.where(pred(sc_ref[pl.ds(tile_start(kt), TK), :]), 1.0, 0.0)
            return cnt + m.reshape(TK // FCOUNT_ROWS, FCOUNT_ROWS, TQ).sum(axis=0)
        cnt = lax.fori_loop(0, n_kt, body, jnp.zeros((FCOUNT_ROWS, TQ), F32))
        return jnp.sum(cnt, axis=0, keepdims=True)

    def plain_bias():
        thr = thr_ref[...]

        def bias_tile(kt, cnt):
            ks = tile_start(kt)
            above = sc_ref[pl.ds(ks, TK), :] >= thr
            bias_ref[pl.ds(ks, TK), :] = jnp.where(above & (ks + key_in_tile < limit), 0.0, NEG)
            m = jnp.where(above, 1.0, 0.0)
            return cnt + m.reshape(TK // FCOUNT_ROWS, FCOUNT_ROWS, TQ).sum(axis=0)
        cnt = lax.fori_loop(0, n_kt, bias_tile, jnp.zeros((FCOUNT_ROWS, TQ), F32))
        return jnp.sum(cnt, axis=0, keepdims=True)

    disagree = jnp.max(jnp.abs(plain_bias() - cnt_ref[...])) > 0.0

    @pl.when(disagree)
    def _():
        def body(it, carry):
            lo, cnt_lo = carry
            cand = lo + jnp.left_shift(jnp.int32(1), 31 - it)
            cand_f = key_to_float(cand)
            c = count(lambda s: s >= cand_f)
            ok = c >= float(topk)
            return jnp.where(ok, cand, lo), jnp.where(ok, c, cnt_lo)
        lo, cnt_lo = lax.fori_loop(0, 32, body, (jnp.full((1, TQ), INT_MIN, I32),
                                                 jnp.full((1, TQ), n_kt * TK, I32).astype(F32)))
        thr_ref[...] = jnp.where(lo == INT_MIN, -jnp.inf, key_to_float(lo))
        cnt_ref[...] = cnt_lo

    thr = thr_ref[...]
    has_ties = jnp.max(cnt_ref[...]) > float(topk)

    @pl.when(disagree & jnp.logical_not(has_ties))
    def _():
        plain_bias()

    @pl.when(has_ties)
    def _():
        need = float(topk) - count(lambda s: s > thr)
        ri = lax.broadcasted_iota(I32, (TK, TK), 0)
        ci = lax.broadcasted_iota(I32, (TK, TK), 1)
        tri = jnp.where(ci < ri, 1.0, 0.0).astype(BF16)

        def bias_tile(kt, run):
            ks = tile_start(kt)
            s = sc_ref[pl.ds(ks, TK), :]
            eq = jnp.where(s == thr, 1.0, 0.0)
            before = jnp.dot(tri, eq.astype(BF16), preferred_element_type=F32) + run
            sel = (s > thr) | ((s == thr) & (before < need))
            sel = sel & (ks + key_in_tile < limit)
            bias_ref[pl.ds(ks, TK), :] = jnp.where(sel, 0.0, NEG)
            return run + jnp.sum(eq, axis=0, keepdims=True)
        lax.fori_loop(0, n_kt, bias_tile, jnp.zeros((1, TQ), F32))

    m_ref[...] = jnp.full(m_ref.shape, -3e38, F32)
    acc_ref[...] = jnp.zeros(acc_ref.shape, F32)

    def scores(kt, slot):
        kk = k_ref[0, pl.ds(tile_start(jnp.minimum(kt, n_kt - 1)), TK), :]
        bias = bias_ref[pl.ds(tile_start(jnp.minimum(kt, n_kt)), TK), :]
        for h in range(ATT_HEADS):
            s_ref[slot, h] = jnp.dot(kk, wq_ref[h], preferred_element_type=F32) + bias

    def softmax(slot):
        for h in range(ATT_HEADS):
            for c in range(TQ // LANES):
                ql = slice(c * LANES, (c + 1) * LANES)
                s = s_ref[slot, h, :, ql]
                m_old = m_ref[h:h + 1, ql]
                m_new = jnp.maximum(m_old, jnp.max(s, axis=0, keepdims=True))
                p_ref[slot, h, :, ql] = jnp.exp2(s - m_new).astype(BF16)
                alpha_ref[slot, h:h + 1, ql] = jnp.exp2(m_old - m_new)
                m_ref[h:h + 1, ql] = m_new

    def weighted_values(kt, slot):
        ks = tile_start(jnp.clip(kt, 0, n_kt - 1))
        for h in range(ATT_HEADS):
            g = h // ATT_REP
            pv = jnp.dot(vT_ref[0, g * V_ROWS:(g + 1) * V_ROWS, pl.ds(ks, TK)], p_ref[slot, h],
                         preferred_element_type=F32)
            acc_ref[h] = alpha_ref[slot, h:h + 1, :] * acc_ref[h] + pv

    bias_ref[pl.ds(tile_start(n_kt), TK), :] = jnp.full((TK, TQ), NEG, F32)
    scores(0, 0)
    p_ref[1] = jnp.zeros(p_ref.shape[1:], BF16)
    alpha_ref[1] = jnp.ones(alpha_ref.shape[1:], F32)

    def att_pair(j, carry):
        scores(2 * j + 1, 1)
        softmax(0)
        weighted_values(2 * j - 1, 1)
        scores(2 * j + 2, 0)
        softmax(1)
        weighted_values(2 * j, 0)
        return carry

    n_pairs = n_kt // 2
    lax.fori_loop(0, n_pairs, att_pair, 0)
    weighted_values(2 * n_pairs - 1, 1)

    @pl.when(n_kt % 2 == 1)
    def _():
        softmax(0)
        weighted_values(n_kt - 1, 0)

    for c in range(ATT_HEADS // 2):
        pair = [acc_ref[h, :HD, :] / acc_ref[h, HD:HD + 1, :] for h in (2 * c, 2 * c + 1)]
        o_ref[0, :, c * LANES:(c + 1) * LANES] = jnp.concatenate(pair, axis=0).T.astype(BF16)


def _dsa(qT, qiT, wiT, ki, k, vT):
    B, QW, S = qT.shape
    TQ = DSA_TQ
    topk = min(TOPK_MAX, S // 4)
    assert S % DSA_TK == 0 and DSA_TK % TQ == 0 and DSA_TK >= topk
    return pl.pallas_call(
        functools.partial(_dsa_kernel, topk=topk),
        out_shape=jax.ShapeDtypeStruct((B, S, QW), BF16),
        grid=(B, S // TQ),
        in_specs=[
            pl.BlockSpec((1, QW, TQ), lambda b, i: (b, 0, i)),
            pl.BlockSpec((1, IDX_HEADS * IDX_DIM, TQ), lambda b, i: (b, 0, i)),
            pl.BlockSpec((1, IDX_HEADS, TQ), lambda b, i: (b, 0, i)),
            pl.BlockSpec((1, S, LANES), lambda b, i: (b, 0, 0)),
            pl.BlockSpec((1, S, LANES), lambda b, i: (b, 0, 0)),
            pl.BlockSpec((1, ATT_KV_HEADS * V_ROWS, S), lambda b, i: (b, 0, 0)),
        ],
        out_specs=pl.BlockSpec((1, TQ, QW), lambda b, i: (b, i, 0)),
        scratch_shapes=[
            pltpu.VMEM((ATT_HEADS, LANES, TQ), BF16),
            pltpu.VMEM((IDX_HEADS, LANES, TQ), BF16),
            pltpu.VMEM((S, TQ), F32),
            pltpu.VMEM((32, S // 32, TQ), I32),
            pltpu.VMEM((S + DSA_TK, TQ), F32),
            pltpu.VMEM((ATT_HEADS, TQ), F32),
            pltpu.VMEM((ATT_HEADS, V_ROWS, TQ), F32),
            pltpu.VMEM((2, ATT_HEADS, TQ), F32),
            pltpu.VMEM((2, ATT_HEADS, DSA_TK, TQ), F32),
            pltpu.VMEM((2, ATT_HEADS, DSA_TK, TQ), BF16),
            pltpu.VMEM((1, TQ), F32),
            pltpu.VMEM((1, TQ), F32),
        ],
        compiler_params=_params("arbitrary", "arbitrary"),
        name="dsa",
    )(qT, qiT, wiT, ki, k, vT)


S5_SLABS = S5_WIDTH // LANES
S5_SLAB_STATES = (LANES // S5_GROUP) * S5_STATE


def _s5_param_kernel(lre_ref, lim_ref, ldt_ref, bre_ref, bim_ref,
                     lbre_ref, lbim_ref, bbre_ref, bbim_ref):
    lre = jnp.minimum(lre_ref[...], -1e-4)
    lim = lim_ref[...]
    dt = jnp.exp(ldt_ref[...])
    mag = jnp.exp(lre * dt)
    lbre = mag * jnp.cos(lim * dt)
    lbim = mag * jnp.sin(lim * dt)
    nre = lbre - 1.0
    den = lre * lre + lim * lim
    cre = (nre * lre + lbim * lim) / den
    cim = (lbim * lre - nre * lim) / den
    lbre_ref[...] = lbre
    lbim_ref[...] = lbim
    bbre_ref[...] = cre * bre_ref[...] - cim * bim_ref[...]
    bbim_ref[...] = cre * bim_ref[...] + cim * bre_ref[...]


def _s5_params(lam_re, lam_im, log_dt, b_re, b_im, c_re, c_im):
    G, P, H = S5_GROUPS, S5_STATE, S5_GROUP
    n = G * P
    flat = lambda a: a.astype(F32).reshape(1, n)
    ldt = jnp.repeat(log_dt.astype(F32), P).reshape(1, n)
    bt = lambda a: a.astype(F32).transpose(2, 0, 1).reshape(H, n)
    lbre, lbim, bbre, bbim = pl.pallas_call(
        _s5_param_kernel,
        out_shape=(jax.ShapeDtypeStruct((1, n), F32), jax.ShapeDtypeStruct((1, n), F32),
                   jax.ShapeDtypeStruct((H, n), F32), jax.ShapeDtypeStruct((H, n), F32)),
        name="s5_params",
    )(flat(lam_re), flat(lam_im), ldt, bt(b_re), bt(b_im))

    gl = LANES // H
    eye = jnp.eye(gl, dtype=F32)
    bb = jnp.stack([bbre, bbim]).reshape(2, H, S5_SLABS, gl, P)
    wb = jnp.einsum("ahjgp,gk->jghakp", bb, eye).reshape(S5_SLABS, LANES, 2 * gl * P)
    cc = jnp.stack([c_re.astype(F32), -c_im.astype(F32)]).reshape(2, S5_SLABS, gl, H, P)
    wc = jnp.einsum("ajghp,gk->jagpkh", cc, eye).reshape(S5_SLABS, 2 * gl * P, LANES)
    lam = lambda a: jnp.broadcast_to(a.reshape(S5_SLABS, 1, gl * P), (S5_SLABS, SUBLANES, gl * P))
    return lam(lbre), lam(lbim), wb.astype(BF16), wc.astype(BF16)


def _s5_kernel(u_ref, lbre_ref, lbim_ref, wb_ref, wc_ref, d_ref, wglu_ref, bglu_ref, o_ref,
               state_ref, h_ref, y_ref, u_sc, hb_ref, *, steps, batch):
    NS = S5_SLAB_STATES
    t = pl.program_id(0)

    @pl.when(t == 0)
    def _():
        state_ref[...] = jnp.zeros(state_ref.shape, F32)

    W = S5_WIDTH
    for b in range(batch):
        for j in range(S5_SLABS):
            c0 = b * W + j * LANES
            u_sc[j, pl.ds(b, steps, stride=batch), :] = u_ref[:, c0:c0 + LANES]
    for j in range(S5_SLABS):
        h_ref[j] = jnp.dot(u_sc[j].astype(BF16), wb_ref[j], preferred_element_type=F32)

    for j in range(S5_SLABS):
        lre = lbre_ref[j]
        lim = lbim_ref[j]

        def step2(s2, carry):
            hre, him = carry
            r0 = pl.multiple_of(s2 * (2 * batch), 2 * batch)
            bre = h_ref[j, pl.ds(r0, 2 * batch), :NS]
            bim = h_ref[j, pl.ds(r0, 2 * batch), NS:]
            are = lre * hre - lim * him + bre[:batch]
            aim = lre * him + lim * hre + bim[:batch]
            nre = lre * are - lim * aim + bre[batch:]
            nim = lre * aim + lim * are + bim[batch:]
            hb_ref[j, pl.ds(r0, 2 * batch), :NS] = jnp.concatenate([are, nre], axis=0).astype(BF16)
            hb_ref[j, pl.ds(r0, 2 * batch), NS:] = jnp.concatenate([aim, nim], axis=0).astype(BF16)
            return nre, nim

        hre, him = lax.fori_loop(0, steps // 2, step2, (state_ref[j, :, :NS], state_ref[j, :, NS:]),
                                 unroll=2)
        state_ref[j, :, :NS] = hre
        state_ref[j, :, NS:] = him

    ys = [jnp.dot(hb_ref[j], wc_ref[j], preferred_element_type=F32) for j in range(S5_SLABS)]
    u = jnp.concatenate([u_sc[j] for j in range(S5_SLABS)], axis=1)
    y = jnp.concatenate(ys, axis=1) + d_ref[...] * u
    y = jax.nn.gelu(y)
    gate = jnp.dot(y.astype(BF16), wglu_ref[...], preferred_element_type=F32) + bglu_ref[...]
    y = y * jax.nn.sigmoid(gate)
    for j in range(S5_SLABS):
        y_ref[j] = y[:, j * LANES:(j + 1) * LANES]
    for b in range(batch):
        for j in range(S5_SLABS):
            o_ref[b, :, j * LANES:(j + 1) * LANES] = (
                y_ref[j, pl.ds(b, steps, stride=batch), :].astype(BF16))


def _s5(u2, lbre, lbim, wb, wc, d_skip, w_glu, b_glu, batch, steps):
    S, W = u2.shape[0], S5_WIDTH
    NS2 = 2 * S5_SLAB_STATES
    const = lambda *shape: pl.BlockSpec(shape, lambda t: (0,) * len(shape))
    return pl.pallas_call(
        functools.partial(_s5_kernel, steps=steps, batch=batch),
        out_shape=jax.ShapeDtypeStruct((batch, S, W), BF16),
        grid=(S // steps,),
        in_specs=[
            pl.BlockSpec((steps, batch * W), lambda t: (t, 0)),
            const(S5_SLABS, SUBLANES, S5_SLAB_STATES),
            const(S5_SLABS, SUBLANES, S5_SLAB_STATES),
            const(S5_SLABS, LANES, NS2),
            const(S5_SLABS, NS2, LANES),
            const(1, W),
            const(W, W),
            const(1, W),
        ],
        out_specs=pl.BlockSpec((batch, steps, W), lambda t: (0, t, 0)),
        scratch_shapes=[
            pltpu.VMEM((S5_SLABS, batch, NS2), F32),
            pltpu.VMEM((S5_SLABS, steps * batch, NS2), F32),
            pltpu.VMEM((S5_SLABS, steps * batch, LANES), F32),
            pltpu.VMEM((S5_SLABS, steps * batch, LANES), F32),
            pltpu.VMEM((S5_SLABS, steps * batch, NS2), BF16),
        ],
        compiler_params=_params("arbitrary"),
        name="s5",
    )(u2, lbre, lbim, wb, wc, d_skip.astype(F32).reshape(1, W), w_glu.astype(BF16),
      b_glu.astype(F32).reshape(1, W))


def _layer_norm(z, g, b):
    mu = jnp.mean(z, axis=-1, keepdims=True)
    zc = z - mu
    var = jnp.mean(zc * zc, axis=-1, keepdims=True)
    return zc * lax.rsqrt(var + LN_EPS) * g + b


def _merge_kernel(x_ref, a_ref, yb_ref, qm_ref, mem_ref, wkv_ref, wg_ref, bg_ref,
                  wpa_ref, wpb_ref, wpc_ref, wout_ref, g_ref, b_ref, o_ref, kv_ref, *, alpha):
    D = x_ref.shape[2]
    HW = MEM_HEADS * MEM_HEAD_DIM

    @pl.when(pl.program_id(1) == 0)
    def _():
        kv_ref[...] = jnp.dot(mem_ref[0].astype(BF16), wkv_ref[...],
                              preferred_element_type=F32).astype(BF16)

    x = x_ref[0]
    xb = x.astype(BF16)
    qm = qm_ref[0]
    heads = []
    for h in range(MEM_HEADS):
        sl = slice(h * MEM_HEAD_DIM, (h + 1) * MEM_HEAD_DIM)
        s = lax.dot_general(qm[:, sl], kv_ref[:, sl], (((1,), (1,)), ((), ())),
                            preferred_element_type=F32)
        p = jnp.exp(s - jnp.max(s, axis=1, keepdims=True))
        p = p / jnp.sum(p, axis=1, keepdims=True)
        heads.append(jnp.dot(p.astype(BF16), kv_ref[:, HW + h * MEM_HEAD_DIM:HW + (h + 1) * MEM_HEAD_DIM],
                             preferred_element_type=F32))
    c_in = jnp.concatenate(heads, axis=1).astype(BF16)

    y_a = jnp.dot(a_ref[0], wpa_ref[...], preferred_element_type=F32)
    y_b = jnp.dot(yb_ref[0], wpb_ref[...], preferred_element_type=F32)
    y_c = jnp.dot(c_in, wpc_ref[...], preferred_element_type=F32)
    merged = None
    for k, yk in enumerate((y_a, y_b, y_c)):
        gk = jax.nn.sigmoid(jnp.dot(xb, wg_ref[:, k * D:(k + 1) * D], preferred_element_type=F32)
                            + bg_ref[:, k * D:(k + 1) * D])
        merged = gk * yk if merged is None else merged + gk * yk
    mix = jnp.dot(merged.astype(BF16), wout_ref[...], preferred_element_type=F32)
    o_ref[0] = _layer_norm(alpha * x + mix, g_ref[...], b_ref[...])


def _merge(x, att, yb, qm, mem, w_mem_kv, w_gate, b_gate, w_proj_a, w_proj_b, w_proj_c, w_out,
           ln_g, ln_b, alpha, tile):
    B, S, D = x.shape
    n_mem = mem.shape[1]
    HW = MEM_HEADS * MEM_HEAD_DIM
    const = lambda *shape: pl.BlockSpec(shape, lambda b, t: (0,) * len(shape))
    row = lambda width: pl.BlockSpec((1, tile, width), lambda b, t: (b, t, 0))
    return pl.pallas_call(
        functools.partial(_merge_kernel, alpha=alpha),
        out_shape=jax.ShapeDtypeStruct((B, S, D), F32),
        grid=(B, S // tile),
        in_specs=[
            row(D), row(att.shape[2]), row(yb.shape[2]), row(HW),
            pl.BlockSpec((1, n_mem, D), lambda b, t: (b, 0, 0)),
            const(D, 2 * HW), const(D, N_BRANCH * D), const(1, N_BRANCH * D),
            const(att.shape[2], D), const(yb.shape[2], D), const(HW, D), const(D, D),
            const(1, D), const(1, D),
        ],
        out_specs=row(D),
        scratch_shapes=[pltpu.VMEM((n_mem, 2 * HW), BF16)],
        compiler_params=_params("parallel", "arbitrary"),
        name="merge",
    )(x, att, yb, qm, mem, w_mem_kv.astype(BF16), w_gate.astype(BF16),
      b_gate.astype(F32).reshape(1, -1), w_proj_a.astype(BF16), w_proj_b.astype(BF16),
      w_proj_c.astype(BF16), w_out.astype(BF16), ln_g.astype(F32).reshape(1, D),
      ln_b.astype(F32).reshape(1, D))


FFN_CHUNK = 256


def _ffn_kernel(h_ref, wup_ref, cw_ref, cb_ref, wdn_ref, g_ref, b_ref, o_ref,
                act_ref, tail_ref, *, alpha, ffn_dim):
    T = h_ref.shape[1]
    FC = FFN_CHUNK
    HALO = SUBLANES

    @pl.when(pl.program_id(1) == 0)
    def _():
        tail_ref[...] = jnp.zeros(tail_ref.shape, F32)

    h = h_ref[0]
    hb = h.astype(BF16)
    halo_row = lax.broadcasted_iota(I32, (HALO, FC), 0)

    def conv_cols(c0):
        up = jnp.dot(hb, wup_ref[:, c0:c0 + FC], preferred_element_type=F32)
        tail = tail_ref[:, c0:c0 + FC]
        tail_ref[:, c0:c0 + FC] = up[T - HALO:]
        out = cb_ref[:, c0:c0 + FC] + cw_ref[CONV_WIDTH - 1:CONV_WIDTH, c0:c0 + FC] * up
        for d in range(1, CONV_WIDTH):
            rolled = pltpu.roll(up, d, 0)
            top = jnp.where(halo_row < d, pltpu.roll(tail, d, 0), rolled[:HALO])
            delayed = jnp.concatenate([top, rolled[HALO:]], axis=0)
            out = out + cw_ref[CONV_WIDTH - 1 - d:CONV_WIDTH - d, c0:c0 + FC] * delayed
        return out

    for c in range(ffn_dim // FC):
        gate = conv_cols(c * FC)
        up = conv_cols(ffn_dim + c * FC)
        act_ref[:, c * FC:(c + 1) * FC] = (gate * jax.nn.sigmoid(gate) * up).astype(BF16)
    f = jnp.dot(act_ref[...], wdn_ref[...], preferred_element_type=F32)
    o_ref[0] = _layer_norm(alpha * h + f, g_ref[...], b_ref[...])


def _ffn(h, w_up, conv_w, conv_b, w_down, ln_g, ln_b, alpha, tile):
    B, S, D = h.shape
    F2 = w_up.shape[1]
    const = lambda *shape: pl.BlockSpec(shape, lambda b, t: (0,) * len(shape))
    row = pl.BlockSpec((1, tile, D), lambda b, t: (b, t, 0))
    return pl.pallas_call(
        functools.partial(_ffn_kernel, alpha=alpha, ffn_dim=F2 // 2),
        out_shape=jax.ShapeDtypeStruct((B, S, D), F32),
        grid=(B, S // tile),
        in_specs=[row, const(D, F2), const(CONV_WIDTH, F2), const(1, F2), const(F2 // 2, D),
                  const(1, D), const(1, D)],
        out_specs=row,
        scratch_shapes=[
            pltpu.VMEM((tile, F2 // 2), BF16),
            pltpu.VMEM((SUBLANES, F2), F32),
        ],
        compiler_params=_params("parallel", "arbitrary"),
        name="ffn",
    )(h, w_up.astype(BF16), conv_w.astype(F32), conv_b.astype(F32).reshape(1, F2),
      w_down.astype(BF16), ln_g.astype(F32).reshape(1, D), ln_b.astype(F32).reshape(1, D))


PROJ_TILE = 512
S5_STEPS = 64
MERGE_TILE = 512
FFN_TILE = 512


def kernel(x, mem, positions, w_in, w_gate, b_gate, s5_lam_re, s5_lam_im, s5_log_dt, s5_b_re, s5_b_im, s5_c_re, s5_c_im, s5_d, w_glu, b_glu, w_mem_kv, w_proj_a, w_proj_b, w_proj_c, w_out, ln1_g, ln1_b, w_up, conv_w, conv_b, w_down, ln2_g, ln2_b):
    B, S, D = x.shape
    depth = w_in.shape[0]
    alpha = (2.0 * depth) ** 0.25
    h = x
    for l in range(depth):
        qT, k, vT, qiT, ki, wiT, u, qm = _proj(h, positions, w_in[l], min(PROJ_TILE, S))
        att = _dsa(qT, qiT, wiT, ki, k, vT)
        lbre, lbim, wb, wc = _s5_params(s5_lam_re[l], s5_lam_im[l], s5_log_dt[l], s5_b_re[l],
                                        s5_b_im[l], s5_c_re[l], s5_c_im[l])
        yb = _s5(u, lbre, lbim, wb, wc, s5_d[l], w_glu[l], b_glu[l], B, min(S5_STEPS, S))
        h = _merge(h, att, yb, qm, mem, w_mem_kv[l], w_gate[l], b_gate[l], w_proj_a[l],
                   w_proj_b[l], w_proj_c[l], w_out[l], ln1_g[l], ln1_b[l], alpha, min(MERGE_TILE, S))
        h = _ffn(h, w_up[l], conv_w[l], conv_b[l], w_down[l], ln2_g[l], ln2_b[l], alpha,
                 min(FFN_TILE, S))
    return h
```

```python
import functools
import math

import jax
import jax.numpy as jnp
from jax import lax
from jax.experimental import pallas as pl
from jax.experimental.pallas import tpu as pltpu

F32 = jnp.float32
BF16 = jnp.bfloat16
I32 = jnp.int32

CHUNK = 64
ATT_HEADS = 8
ATT_KV_HEADS = 2
ATT_HEAD_DIM = 64
ATT_REP = ATT_HEADS // ATT_KV_HEADS
IDX_HEADS = 4
IDX_DIM = 64
TOPK_MAX = 256
ROPE_THETA = 10000.0
S5_WIDTH = 512
S5_GROUP = 16
S5_GROUPS = S5_WIDTH // S5_GROUP
S5_STATE = 64
MEM_HEADS = 4
MEM_HEAD_DIM = 128
CONV_WIDTH = 3
N_BRANCH = 3
LN_EPS = 1e-5
NEG = -1e30
INT_MIN = -(2 ** 31)
LOG2E = math.log2(math.e)

LANES = 128
SUBLANES = 8
VMEM_LIMIT = 56 * 1024 * 1024

_Q0 = 0
_K0 = _Q0 + ATT_HEADS * ATT_HEAD_DIM
_V0 = _K0 + ATT_KV_HEADS * ATT_HEAD_DIM
_QI0 = _V0 + ATT_KV_HEADS * ATT_HEAD_DIM
_KI0 = _QI0 + IDX_HEADS * IDX_DIM
_WI0 = _KI0 + IDX_DIM
_U0 = _WI0 + IDX_HEADS
_QM0 = _U0 + S5_WIDTH
_END = _QM0 + MEM_HEADS * MEM_HEAD_DIM
_PAD = LANES - IDX_DIM - IDX_HEADS
_PU0 = _KI0 + LANES
_PQM0 = _PU0 + S5_WIDTH
_PEND = _PQM0 + MEM_HEADS * MEM_HEAD_DIM
V_ROWS = ATT_HEAD_DIM + 16


def _params(*sem):
    return pltpu.CompilerParams(dimension_semantics=sem, vmem_limit_bytes=VMEM_LIMIT)


def _proj_kernel(x_ref, pos_ref, invf_ref, w_ref,
                 qT_ref, k_ref, vT_ref, qiT_ref, ki_ref, wiT_ref, u_ref, qm_ref):
    T = x_ref.shape[1]
    y = jnp.dot(x_ref[0].astype(BF16), w_ref[...], preferred_element_type=F32)

    ang = pos_ref[0].astype(F32) * invf_ref[...]
    cos = jnp.cos(ang)
    sin = jnp.sin(ang)
    lane = lax.broadcasted_iota(I32, (T, LANES), 1)
    first = (lane % ATT_HEAD_DIM) < (ATT_HEAD_DIM // 2)
    sin = jnp.where(first, -sin, sin)

    def rope(z):
        partner = jnp.where(first, pltpu.roll(z, LANES - 32, 1), pltpu.roll(z, 32, 1))
        return z * cos + partner * sin

    att_scale = ATT_HEAD_DIM ** -0.5 * LOG2E
    for c in range(ATT_HEADS // 2):
        z = rope(y[:, _Q0 + c * LANES:_Q0 + (c + 1) * LANES]) * att_scale
        qT_ref[0, c * LANES:(c + 1) * LANES, :] = z.T.astype(BF16)
    k_ref[0] = rope(y[:, _K0:_K0 + LANES]).astype(BF16)
    vT = y[:, _V0:_V0 + LANES].T.astype(BF16)
    for g in range(ATT_KV_HEADS):
        vT_ref[0, g * V_ROWS:g * V_ROWS + ATT_HEAD_DIM, :] = vT[g * ATT_HEAD_DIM:(g + 1) * ATT_HEAD_DIM]
        vT_ref[0, g * V_ROWS + ATT_HEAD_DIM:(g + 1) * V_ROWS, :] = jnp.ones((V_ROWS - ATT_HEAD_DIM, T), BF16)
    idx_scale = IDX_DIM ** -0.5
    for c in range(IDX_HEADS // 2):
        z = rope(y[:, _QI0 + c * LANES:_QI0 + (c + 1) * LANES]) * idx_scale
        qiT_ref[0, c * LANES:(c + 1) * LANES, :] = z.T.astype(BF16)
    kw = y[:, _KI0:_KI0 + LANES]
    ki_ref[0] = jnp.where(lane < IDX_DIM, rope(kw), 0.0).astype(BF16)
    wiT_ref[0] = kw.T[IDX_DIM:IDX_DIM + IDX_HEADS] * (IDX_HEADS ** -0.5)
    u_ref[...] = y[:, _PU0:_PU0 + S5_WIDTH]
    qm_ref[0] = (y[:, _PQM0:_PEND] * (MEM_HEAD_DIM ** -0.5)).astype(BF16)


def _proj(x, positions, w_in, tile):
    B, S, D = x.shape
    nt = S // tile
    w_pad = jnp.concatenate(
        [w_in[:, :_U0], jnp.zeros((D, _PAD), w_in.dtype), w_in[:, _U0:]], axis=1).astype(BF16)
    half = ATT_HEAD_DIM // 2
    inv_freq = ROPE_THETA ** (-jnp.arange(half, dtype=F32) / half)
    invf = jnp.tile(inv_freq, LANES // half)[None, :]
    pos3 = positions.reshape(B, S, 1)
    QW = ATT_HEADS * ATT_HEAD_DIM
    IW = IDX_HEADS * IDX_DIM
    out_shape = (
        jax.ShapeDtypeStruct((B, QW, S), BF16),
        jax.ShapeDtypeStruct((B, S, LANES), BF16),
        jax.ShapeDtypeStruct((B, ATT_KV_HEADS * V_ROWS, S), BF16),
        jax.ShapeDtypeStruct((B, IW, S), BF16),
        jax.ShapeDtypeStruct((B, S, LANES), BF16),
        jax.ShapeDtypeStruct((B, IDX_HEADS, S), F32),
        jax.ShapeDtypeStruct((S, B * S5_WIDTH), F32),
        jax.ShapeDtypeStruct((B, S, MEM_HEADS * MEM_HEAD_DIM), BF16),
    )
    rows = lambda width: pl.BlockSpec((1, tile, width), lambda b, t: (b, t, 0))
    cols = lambda height: pl.BlockSpec((1, height, tile), lambda b, t: (b, 0, t))
    out_specs = (
        cols(QW), rows(LANES), cols(ATT_KV_HEADS * V_ROWS), cols(IW), rows(LANES), cols(IDX_HEADS),
        pl.BlockSpec((tile, S5_WIDTH), lambda b, t: (t, b)),
        rows(MEM_HEADS * MEM_HEAD_DIM),
    )
    return pl.pallas_call(
        _proj_kernel,
        out_shape=out_shape,
        grid=(B, nt),
        in_specs=[
            pl.BlockSpec((1, tile, D), lambda b, t: (b, t, 0)),
            pl.BlockSpec((1, tile, 1), lambda b, t: (b, t, 0)),
            pl.BlockSpec((1, LANES), lambda b, t: (0, 0)),
            pl.BlockSpec((D, _PEND), lambda b, t: (0, 0)),
        ],
        out_specs=out_specs,
        compiler_params=_params("parallel", "parallel"),
        name="proj",
    )(x, pos3, invf, w_pad)


DSA_TQ = 256
DSA_TK = 256
PLANE_KEYS = 32 * SUBLANES
FCOUNT_ROWS = 4 * SUBLANES


def _bit_transpose32(words):
    a = list(words)
    j, mask = 16, 0x0000FFFF
    while j:
        k = 0
        while k < 32:
            t = (a[k] ^ lax.shift_right_logical(a[k + j], jnp.int32(j))) & jnp.int32(mask)
            a[k] = a[k] ^ t
            a[k + j] = a[k + j] ^ lax.shift_left(t, jnp.int32(j))
            k = (k + j + 1) & ~j
        j >>= 1
        mask = (mask ^ (mask << j)) & 0xFFFFFFFF
    return a


def _dsa_kernel(qT_ref, qiT_ref, wiT_ref, ki_ref, k_ref, vT_ref, o_ref,
                wq_ref, wqi_ref, sc_ref, planes_ref, bias_ref, m_ref, acc_ref, alpha_ref,
                s_ref, p_ref, thr_ref, cnt_ref,
                *, topk):
    TQ, TK = DSA_TQ, DSA_TK
    HD = ATT_HEAD_DIM
    i = pl.program_id(1)
    n_kt = (i * TQ + TQ + TK - 1) // TK
    key_in_tile = lax.broadcasted_iota(I32, (TK, TQ), 0)
    qpos = i * TQ + lax.broadcasted_iota(I32, (1, TQ), 1)
    limit = (qpos // CHUNK + 1) * CHUNK
    wi = wiT_ref[0]

    def tile_start(kt):
        return pl.multiple_of(kt * TK, TK)

    zeros = jnp.zeros((HD, TQ), BF16)
    for h in range(ATT_HEADS):
        qh = qT_ref[0, h * HD:(h + 1) * HD, :]
        g = h // ATT_REP
        wq_ref[h] = jnp.concatenate([zeros] * g + [qh] + [zeros] * (ATT_KV_HEADS - 1 - g), axis=0)
    for h in range(IDX_HEADS):
        wqi_ref[h] = jnp.concatenate([qiT_ref[0, h * IDX_DIM:(h + 1) * IDX_DIM, :], zeros], axis=0)

    @pl.when((pl.program_id(0) == 0) & (i == 0))
    def _():
        planes_ref[...] = jnp.zeros(planes_ref.shape, I32)

    def score_tile(kt):
        ks = tile_start(kt)
        kk = ki_ref[0, pl.ds(ks, TK), :]
        acc = jnp.zeros((TK, TQ), F32)
        for h in range(IDX_HEADS):
            logit = jnp.dot(kk, wqi_ref[h], preferred_element_type=F32)
            acc = acc + wi[h:h + 1, :] * jnp.maximum(logit, 0.0)
        sc_ref[pl.ds(ks, TK), :] = jnp.where(ks + key_in_tile < limit, acc, NEG)

    def score_pair(j, carry):
        score_tile(2 * j)
        score_tile(2 * j + 1)
        return carry

    lax.fori_loop(0, n_kt // 2, score_pair, 0)

    @pl.when(n_kt % 2 == 1)
    def _():
        score_tile(n_kt - 1)

    def plane_tile(kt):
        bits = lax.bitcast_convert_type(sc_ref[pl.ds(tile_start(kt), TK), :], I32)
        ukey = jnp.where(bits < 0, -bits, bits ^ jnp.int32(INT_MIN))
        for g in range(TK // PLANE_KEYS):
            row0 = pl.multiple_of(kt * (TK // 32) + g * SUBLANES, SUBLANES)
            for c in range(TQ // LANES):
                rows = ukey[g * PLANE_KEYS:(g + 1) * PLANE_KEYS, c * LANES:(c + 1) * LANES]
                words = _bit_transpose32([rows[j * SUBLANES:(j + 1) * SUBLANES] for j in range(32)])
                for b in range(32):
                    planes_ref[b, pl.ds(row0, SUBLANES), c * LANES:(c + 1) * LANES] = words[31 - b]

    def plane_pair(j, carry):
        plane_tile(2 * j)
        plane_tile(2 * j + 1)
        return carry

    lax.fori_loop(0, n_kt // 2, plane_pair, 0)

    @pl.when(n_kt % 2 == 1)
    def _():
        plane_tile(n_kt - 1)

    def key_to_float(key):
        bits = jnp.where(key < 0, jnp.int32(INT_MIN) - key, key)
        return lax.bitcast_convert_type(bits, F32)

    def popcount_rows(w):
        return jnp.sum(lax.population_count(w), axis=0, keepdims=True)

    def radix_select(n_words):
        word_row = lax.broadcasted_iota(I32, (n_words, TQ), 0)
        cand0 = jnp.where(word_row < n_kt * (TK // 32), jnp.int32(-1), jnp.int32(0))

        def bit_body(it, carry):
            cand, need, ukey_thr = carry
            b = 31 - it
            ones = cand & planes_ref[b, :n_words, :]
            c1 = popcount_rows(ones)
            ok = c1 >= need
            cand = jnp.where(ok, ones, cand ^ ones)
            need = jnp.where(ok, need, need - c1)
            ukey_thr = ukey_thr | jnp.left_shift(ok.astype(I32), b)
            return cand, need, ukey_thr

        cand, need, ukey_thr = lax.fori_loop(
            0, 32, bit_body, (cand0, jnp.full((1, TQ), topk, I32), jnp.zeros((1, TQ), I32)),
            unroll=4)
        thr_key = ukey_thr ^ jnp.int32(INT_MIN)
        cnt_key = (topk - need) + popcount_rows(cand)
        thr_ref[...] = jnp.where(thr_key == INT_MIN, -jnp.inf, key_to_float(thr_key))
        cnt_ref[...] = cnt_key.astype(F32)

    all_words = planes_ref.shape[1]
    half_words = all_words // 2
    few_keys = n_kt * (TK // 32) <= half_words

    @pl.when(few_keys)
    def _():
        radix_select(half_words)

    @pl.when(jnp.logical_not(few_keys))
    def _():
        radix_select(all_words)

    def count(pred):
        def body(kt, cnt):
            m = jnp.where(pred(sc_ref[pl.ds(tile_start(kt), TK), :]), 1.0, 0.0)
            return cnt + m.reshape(TK // FCOUNT_ROWS, FCOUNT_ROWS, TQ).sum(axis=0)
        cnt = lax.fori_loop(0, n_kt, body, jnp.zeros((FCOUNT_ROWS, TQ), F32))
        return jnp.sum(cnt, axis=0, keepdims=True)

    def plain_bias():
        thr = thr_ref[...]

        def bias_tile(kt, cnt):
            ks = tile_start(kt)
            above = sc_ref[pl.ds(ks, TK), :] >= thr
            bias_ref[pl.ds(ks, TK), :] = jnp.where(above & (ks + key_in_tile < limit), 0.0, NEG)
            m = jnp.where(above, 1.0, 0.0)
            return cnt + m.reshape(TK // FCOUNT_ROWS, FCOUNT_ROWS, TQ).sum(axis=0)
        cnt = lax.fori_loop(0, n_kt, bias_tile, jnp.zeros((FCOUNT_ROWS, TQ), F32))
        return jnp.sum(cnt, axis=0, keepdims=True)

    disagree = jnp.max(jnp.abs(plain_bias() - cnt_ref[...])) > 0.0

    @pl.when(disagree)
    def _():
        def body(it, carry):
            lo, cnt_lo = carry
            cand = lo + jnp.left_shift(jnp.int32(1), 31 - it)
            cand_f = key_to_float(cand)
            c = count(lambda s: s >= cand_f)
            ok = c >= float(topk)
            return jnp.where(ok, cand, lo), jnp.where(ok, c, cnt_lo)
        lo, cnt_lo = lax.fori_loop(0, 32, body, (jnp.full((1, TQ), INT_MIN, I32),
                                                 jnp.full((1, TQ), n_kt * TK, I32).astype(F32)))
        thr_ref[...] = jnp.where(lo == INT_MIN, -jnp.inf, key_to_float(lo))
        cnt_ref[...] = cnt_lo

    thr = thr_ref[...]
    has_ties = jnp.max(cnt_ref[...]) > float(topk)

    @pl.when(disagree & jnp.logical_not(has_ties))
    def _():
        plain_bias()

    @pl.when(has_ties)
    def _():
        need = float(topk) - count(lambda s: s > thr)
        ri = lax.broadcasted_iota(I32, (TK, TK), 0)
        ci = lax.broadcasted_iota(I32, (TK, TK), 1)
        tri = jnp.where(ci < ri, 1.0, 0.0).astype(BF16)

        def bias_tile(kt, run):
            ks = tile_start(kt)
            s = sc_ref[pl.ds(ks, TK), :]
            eq = jnp.where(s == thr, 1.0, 0.0)
            before = jnp.dot(tri, eq.astype(BF16), preferred_element_type=F32) + run
            sel = (s > thr) | ((s == thr) & (before < need))
            sel = sel & (ks + key_in_tile < limit)
            bias_ref[pl.ds(ks, TK), :] = jnp.where(sel, 0.0, NEG)
            return run + jnp.sum(eq, axis=0, keepdims=True)
        lax.fori_loop(0, n_kt, bias_tile, jnp.zeros((1, TQ), F32))

    m_ref[...] = jnp.full(m_ref.shape, -3e38, F32)
    acc_ref[...] = jnp.zeros(acc_ref.shape, F32)

    def scores(kt, slot):
        kk = k_ref[0, pl.ds(tile_start(jnp.minimum(kt, n_kt - 1)), TK), :]
        bias = bias_ref[pl.ds(tile_start(jnp.minimum(kt, n_kt)), TK), :]
        for h in range(ATT_HEADS):
            s_ref[slot, h] = jnp.dot(kk, wq_ref[h], preferred_element_type=F32) + bias

    def softmax(slot):
        for h in range(ATT_HEADS):
            for c in range(TQ // LANES):
                ql = slice(c * LANES, (c + 1) * LANES)
                s = s_ref[slot, h, :, ql]
                m_old = m_ref[h:h + 1, ql]
                m_new = jnp.maximum(m_old, jnp.max(s, axis=0, keepdims=True))
                p_ref[slot, h, :, ql] = jnp.exp2(s - m_new).astype(BF16)
                alpha_ref[slot, h:h + 1, ql] = jnp.exp2(m_old - m_new)
                m_ref[h:h + 1, ql] = m_new

    def weighted_values(kt, slot):
        ks = tile_start(jnp.clip(kt, 0, n_kt - 1))
        for h in range(ATT_HEADS):
            g = h // ATT_REP
            pv = jnp.dot(vT_ref[0, g * V_ROWS:(g + 1) * V_ROWS, pl.ds(ks, TK)], p_ref[slot, h],
                         preferred_element_type=F32)
            acc_ref[h] = alpha_ref[slot, h:h + 1, :] * acc_ref[h] + pv

    bias_ref[pl.ds(tile_start(n_kt), TK), :] = jnp.full((TK, TQ), NEG, F32)
    scores(0, 0)
    p_ref[1] = jnp.zeros(p_ref.shape[1:], BF16)
    alpha_ref[1] = jnp.ones(alpha_ref.shape[1:], F32)

    def att_pair(j, carry):
        scores(2 * j + 1, 1)
        softmax(0)
        weighted_values(2 * j - 1, 1)
        scores(2 * j + 2, 0)
        softmax(1)
        weighted_values(2 * j, 0)
        return carry

    n_pairs = n_kt // 2
    lax.fori_loop(0, n_pairs, att_pair, 0)
    weighted_values(2 * n_pairs - 1, 1)

    @pl.when(n_kt % 2 == 1)
    def _():
        softmax(0)
        weighted_values(n_kt - 1, 0)

    for c in range(ATT_HEADS // 2):
        pair = [acc_ref[h, :HD, :] / acc_ref[h, HD:HD + 1, :] for h in (2 * c, 2 * c + 1)]
        o_ref[0, :, c * LANES:(c + 1) * LANES] = jnp.concatenate(pair, axis=0).T.astype(BF16)


def _dsa(qT, qiT, wiT, ki, k, vT):
    B, QW, S = qT.shape
    TQ = DSA_TQ
    topk = min(TOPK_MAX, S // 4)
    assert S % TQ == 0 and TQ == DSA_TK and DSA_TK >= topk
    return pl.pallas_call(
        functools.partial(_dsa_kernel, topk=topk),
        out_shape=jax.ShapeDtypeStruct((B, S, QW), BF16),
        grid=(B, S // TQ),
        in_specs=[
            pl.BlockSpec((1, QW, TQ), lambda b, i: (b, 0, i)),
            pl.BlockSpec((1, IDX_HEADS * IDX_DIM, TQ), lambda b, i: (b, 0, i)),
            pl.BlockSpec((1, IDX_HEADS, TQ), lambda b, i: (b, 0, i)),
            pl.BlockSpec((1, S, LANES), lambda b, i: (b, 0, 0)),
            pl.BlockSpec((1, S, LANES), lambda b, i: (b, 0, 0)),
            pl.BlockSpec((1, ATT_KV_HEADS * V_ROWS, S), lambda b, i: (b, 0, 0)),
        ],
        out_specs=pl.BlockSpec((1, TQ, QW), lambda b, i: (b, i, 0)),
        scratch_shapes=[
            pltpu.VMEM((ATT_HEADS, LANES, TQ), BF16),
            pltpu.VMEM((IDX_HEADS, LANES, TQ), BF16),
            pltpu.VMEM((S, TQ), F32),
            pltpu.VMEM((32, S // 32, TQ), I32),
            pltpu.VMEM((S + DSA_TK, TQ), F32),
            pltpu.VMEM((ATT_HEADS, TQ), F32),
            pltpu.VMEM((ATT_HEADS, V_ROWS, TQ), F32),
            pltpu.VMEM((2, ATT_HEADS, TQ), F32),
            pltpu.VMEM((2, ATT_HEADS, DSA_TK, TQ), F32),
            pltpu.VMEM((2, ATT_HEADS, DSA_TK, TQ), BF16),
            pltpu.VMEM((1, TQ), F32),
            pltpu.VMEM((1, TQ), F32),
        ],
        compiler_params=_params("arbitrary", "arbitrary"),
        name="dsa",
    )(qT, qiT, wiT, ki, k, vT)


S5_SLABS = S5_WIDTH // LANES
S5_SLAB_STATES = (LANES // S5_GROUP) * S5_STATE


def _s5_param_kernel(lre_ref, lim_ref, ldt_ref, bre_ref, bim_ref,
                     lbre_ref, lbim_ref, bbre_ref, bbim_ref):
    lre = jnp.minimum(lre_ref[...], -1e-4)
    lim = lim_ref[...]
    dt = jnp.exp(ldt_ref[...])
    mag = jnp.exp(lre * dt)
    lbre = mag * jnp.cos(lim * dt)
    lbim = mag * jnp.sin(lim * dt)
    nre = lbre - 1.0
    den = lre * lre + lim * lim
    cre = (nre * lre + lbim * lim) / den
    cim = (lbim * lre - nre * lim) / den
    lbre_ref[...] = lbre
    lbim_ref[...] = lbim
    bbre_ref[...] = cre * bre_ref[...] - cim * bim_ref[...]
    bbim_ref[...] = cre * bim_ref[...] + cim * bre_ref[...]


def _s5_params(lam_re, lam_im, log_dt, b_re, b_im, c_re, c_im):
    G, P, H = S5_GROUPS, S5_STATE, S5_GROUP
    n = G * P
    flat = lambda a: a.astype(F32).reshape(1, n)
    ldt = jnp.repeat(log_dt.astype(F32), P).reshape(1, n)
    bt = lambda a: a.astype(F32).transpose(2, 0, 1).reshape(H, n)
    lbre, lbim, bbre, bbim = pl.pallas_call(
        _s5_param_kernel,
        out_shape=(jax.ShapeDtypeStruct((1, n), F32), jax.ShapeDtypeStruct((1, n), F32),
                   jax.ShapeDtypeStruct((H, n), F32), jax.ShapeDtypeStruct((H, n), F32)),
        name="s5_params",
    )(flat(lam_re), flat(lam_im), ldt, bt(b_re), bt(b_im))

    gl = LANES // H
    eye = jnp.eye(gl, dtype=F32)
    bb = jnp.stack([bbre, bbim]).reshape(2, H, S5_SLABS, gl, P)
    wb = jnp.einsum("ahjgp,gk->jghakp", bb, eye).reshape(S5_SLABS, LANES, 2 * gl * P)
    cc = jnp.stack([c_re.astype(F32), -c_im.astype(F32)]).reshape(2, S5_SLABS, gl, H, P)
    wc = jnp.einsum("ajghp,gk->jagpkh", cc, eye).reshape(S5_SLABS, 2 * gl * P, LANES)
    lam = lambda a: jnp.broadcast_to(a.reshape(S5_SLABS, 1, gl * P), (S5_SLABS, SUBLANES, gl * P))
    return lam(lbre), lam(lbim), wb.astype(BF16), wc.astype(BF16)


def _s5_kernel(u_ref, lbre_ref, lbim_ref, wb_ref, wc_ref, d_ref, wglu_ref, bglu_ref, o_ref,
               state_ref, h_ref, y_ref, u_sc, hb_ref, *, steps, batch):
    NS = S5_SLAB_STATES
    t = pl.program_id(0)

    @pl.when(t == 0)
    def _():
        state_ref[...] = jnp.zeros(state_ref.shape, F32)

    W = S5_WIDTH
    for b in range(batch):
        for j in range(S5_SLABS):
            c0 = b * W + j * LANES
            u_sc[j, pl.ds(b, steps, stride=batch), :] = u_ref[:, c0:c0 + LANES]
    for j in range(S5_SLABS):
        h_ref[j] = jnp.dot(u_sc[j].astype(BF16), wb_ref[j], preferred_element_type=F32)

    for j in range(S5_SLABS):
        lre = lbre_ref[j]
        lim = lbim_ref[j]

        def step2(s2, carry):
            hre, him = carry
            r0 = pl.multiple_of(s2 * (2 * batch), 2 * batch)
            bre = h_ref[j, pl.ds(r0, 2 * batch), :NS]
            bim = h_ref[j, pl.ds(r0, 2 * batch), NS:]
            are = lre * hre - lim * him + bre[:batch]
            aim = lre * him + lim * hre + bim[:batch]
            nre = lre * are - lim * aim + bre[batch:]
            nim = lre * aim + lim * are + bim[batch:]
            hb_ref[j, pl.ds(r0, 2 * batch), :NS] = jnp.concatenate([are, nre], axis=0).astype(BF16)
            hb_ref[j, pl.ds(r0, 2 * batch), NS:] = jnp.concatenate([aim, nim], axis=0).astype(BF16)
            return nre, nim

        hre, him = lax.fori_loop(0, steps // 2, step2, (state_ref[j, :, :NS], state_ref[j, :, NS:]),
                                 unroll=2)
        state_ref[j, :, :NS] = hre
        state_ref[j, :, NS:] = him

    ys = [jnp.dot(hb_ref[j], wc_ref[j], preferred_element_type=F32) for j in range(S5_SLABS)]
    u = jnp.concatenate([u_sc[j] for j in range(S5_SLABS)], axis=1)
    y = jnp.concatenate(ys, axis=1) + d_ref[...] * u
    y = jax.nn.gelu(y)
    gate = jnp.dot(y.astype(BF16), wglu_ref[...], preferred_element_type=F32) + bglu_ref[...]
    y = y * jax.nn.sigmoid(gate)
    for j in range(S5_SLABS):
        y_ref[j] = y[:, j * LANES:(j + 1) * LANES]
    for b in range(batch):
        for j in range(S5_SLABS):
            o_ref[b, :, j * LANES:(j + 1) * LANES] = (
                y_ref[j, pl.ds(b, steps, stride=batch), :].astype(BF16))


def _s5(u2, lbre, lbim, wb, wc, d_skip, w_glu, b_glu, batch, steps):
    S, W = u2.shape[0], S5_WIDTH
    NS2 = 2 * S5_SLAB_STATES
    const = lambda *shape: pl.BlockSpec(shape, lambda t: (0,) * len(shape))
    return pl.pallas_call(
        functools.partial(_s5_kernel, steps=steps, batch=batch),
        out_shape=jax.ShapeDtypeStruct((batch, S, W), BF16),
        grid=(S // steps,),
        in_specs=[
            pl.BlockSpec((steps, batch * W), lambda t: (t, 0)),
            const(S5_SLABS, SUBLANES, S5_SLAB_STATES),
            const(S5_SLABS, SUBLANES, S5_SLAB_STATES),
            const(S5_SLABS, LANES, NS2),
            const(S5_SLABS, NS2, LANES),
            const(1, W),
            const(W, W),
            const(1, W),
        ],
        out_specs=pl.BlockSpec((batch, steps, W), lambda t: (0, t, 0)),
        scratch_shapes=[
            pltpu.VMEM((S5_SLABS, batch, NS2), F32),
            pltpu.VMEM((S5_SLABS, steps * batch, NS2), F32),
            pltpu.VMEM((S5_SLABS, steps * batch, LANES), F32),
            pltpu.VMEM((S5_SLABS, steps * batch, LANES), F32),
            pltpu.VMEM((S5_SLABS, steps * batch, NS2), BF16),
        ],
        compiler_params=_params("arbitrary"),
        name="s5",
    )(u2, lbre, lbim, wb, wc, d_skip.astype(F32).reshape(1, W), w_glu.astype(BF16),
      b_glu.astype(F32).reshape(1, W))


def _layer_norm(z, g, b):
    mu = jnp.mean(z, axis=-1, keepdims=True)
    zc = z - mu
    var = jnp.mean(zc * zc, axis=-1, keepdims=True)
    return zc * lax.rsqrt(var + LN_EPS) * g + b


def _merge_kernel(x_ref, a_ref, yb_ref, qm_ref, mem_ref, wkv_ref, wg_ref, bg_ref,
                  wpa_ref, wpb_ref, wpc_ref, wout_ref, g_ref, b_ref, o_ref, kv_ref, *, alpha):
    D = x_ref.shape[2]
    HW = MEM_HEADS * MEM_HEAD_DIM

    @pl.when(pl.program_id(1) == 0)
    def _():
        kv_ref[...] = jnp.dot(mem_ref[0].astype(BF16), wkv_ref[...],
                              preferred_element_type=F32).astype(BF16)

    x = x_ref[0]
    xb = x.astype(BF16)
    qm = qm_ref[0]
    heads = []
    for h in range(MEM_HEADS):
        sl = slice(h * MEM_HEAD_DIM, (h + 1) * MEM_HEAD_DIM)
        s = lax.dot_general(qm[:, sl], kv_ref[:, sl], (((1,), (1,)), ((), ())),
                            preferred_element_type=F32)
        p = jnp.exp(s - jnp.max(s, axis=1, keepdims=True))
        p = p / jnp.sum(p, axis=1, keepdims=True)
        heads.append(jnp.dot(p.astype(BF16), kv_ref[:, HW + h * MEM_HEAD_DIM:HW + (h + 1) * MEM_HEAD_DIM],
                             preferred_element_type=F32))
    c_in = jnp.concatenate(heads, axis=1).astype(BF16)

    y_a = jnp.dot(a_ref[0], wpa_ref[...], preferred_element_type=F32)
    y_b = jnp.dot(yb_ref[0], wpb_ref[...], preferred_element_type=F32)
    y_c = jnp.dot(c_in, wpc_ref[...], preferred_element_type=F32)
    merged = None
    for k, yk in enumerate((y_a, y_b, y_c)):
        gk = jax.nn.sigmoid(jnp.dot(xb, wg_ref[:, k * D:(k + 1) * D], preferred_element_type=F32)
                            + bg_ref[:, k * D:(k + 1) * D])
        merged = gk * yk if merged is None else merged + gk * yk
    mix = jnp.dot(merged.astype(BF16), wout_ref[...], preferred_element_type=F32)
    o_ref[0] = _layer_norm(alpha * x + mix, g_ref[...], b_ref[...])


def _merge(x, att, yb, qm, mem, w_mem_kv, w_gate, b_gate, w_proj_a, w_proj_b, w_proj_c, w_out,
           ln_g, ln_b, alpha, tile):
    B, S, D = x.shape
    n_mem = mem.shape[1]
    HW = MEM_HEADS * MEM_HEAD_DIM
    const = lambda *shape: pl.BlockSpec(shape, lambda b, t: (0,) * len(shape))
    row = lambda width: pl.BlockSpec((1, tile, width), lambda b, t: (b, t, 0))
    return pl.pallas_call(
        functools.partial(_merge_kernel, alpha=alpha),
        out_shape=jax.ShapeDtypeStruct((B, S, D), F32),
        grid=(B, S // tile),
        in_specs=[
            row(D), row(att.shape[2]), row(yb.shape[2]), row(HW),
            pl.BlockSpec((1, n_mem, D), lambda b, t: (b, 0, 0)),
            const(D, 2 * HW), const(D, N_BRANCH * D), const(1, N_BRANCH * D),
            const(att.shape[2], D), const(yb.shape[2], D), const(HW, D), const(D, D),
            const(1, D), const(1, D),
        ],
        out_specs=row(D),
        scratch_shapes=[pltpu.VMEM((n_mem, 2 * HW), BF16)],
        compiler_params=_params("parallel", "arbitrary"),
        name="merge",
    )(x, att, yb, qm, mem, w_mem_kv.astype(BF16), w_gate.astype(BF16),
      b_gate.astype(F32).reshape(1, -1), w_proj_a.astype(BF16), w_proj_b.astype(BF16),
      w_proj_c.astype(BF16), w_out.astype(BF16), ln_g.astype(F32).reshape(1, D),
      ln_b.astype(F32).reshape(1, D))


FFN_CHUNK = 256


def _ffn_kernel(h_ref, wup_ref, cw_ref, cb_ref, wdn_ref, g_ref, b_ref, o_ref,
                act_ref, tail_ref, *, alpha, ffn_dim):
    T = h_ref.shape[1]
    FC = FFN_CHUNK
    HALO = SUBLANES

    @pl.when(pl.program_id(1) == 0)
    def _():
        tail_ref[...] = jnp.zeros(tail_ref.shape, F32)

    h = h_ref[0]
    hb = h.astype(BF16)
    halo_row = lax.broadcasted_iota(I32, (HALO, FC), 0)

    def conv_cols(c0):
        up = jnp.dot(hb, wup_ref[:, c0:c0 + FC], preferred_element_type=F32)
        tail = tail_ref[:, c0:c0 + FC]
        tail_ref[:, c0:c0 + FC] = up[T - HALO:]
        out = cb_ref[:, c0:c0 + FC] + cw_ref[CONV_WIDTH - 1:CONV_WIDTH, c0:c0 + FC] * up
        for d in range(1, CONV_WIDTH):
            rolled = pltpu.roll(up, d, 0)
            top = jnp.where(halo_row < d, pltpu.roll(tail, d, 0), rolled[:HALO])
            delayed = jnp.concatenate([top, rolled[HALO:]], axis=0)
            out = out + cw_ref[CONV_WIDTH - 1 - d:CONV_WIDTH - d, c0:c0 + FC] * delayed
        return out

    for c in range(ffn_dim // FC):
        gate = conv_cols(c * FC)
        up = conv_cols(ffn_dim + c * FC)
        act_ref[:, c * FC:(c + 1) * FC] = (gate * jax.nn.sigmoid(gate) * up).astype(BF16)
    f = jnp.dot(act_ref[...], wdn_ref[...], preferred_element_type=F32)
    o_ref[0] = _layer_norm(alpha * h + f, g_ref[...], b_ref[...])


def _ffn(h, w_up, conv_w, conv_b, w_down, ln_g, ln_b, alpha, tile):
    B, S, D = h.shape
    F2 = w_up.shape[1]
    const = lambda *shape: pl.BlockSpec(shape, lambda b, t: (0,) * len(shape))
    row = pl.BlockSpec((1, tile, D), lambda b, t: (b, t, 0))
    return pl.pallas_call(
        functools.partial(_ffn_kernel, alpha=alpha, ffn_dim=F2 // 2),
        out_shape=jax.ShapeDtypeStruct((B, S, D), F32),
        grid=(B, S // tile),
        in_specs=[row, const(D, F2), const(CONV_WIDTH, F2), const(1, F2), const(F2 // 2, D),
                  const(1, D), const(1, D)],
        out_specs=row,
        scratch_shapes=[
            pltpu.VMEM((tile, F2 // 2), BF16),
            pltpu.VMEM((SUBLANES, F2), F32),
        ],
        compiler_params=_params("parallel", "arbitrary"),
        name="ffn",
    )(h, w_up.astype(BF16), conv_w.astype(F32), conv_b.astype(F32).reshape(1, F2),
      w_down.astype(BF16), ln_g.astype(F32).reshape(1, D), ln_b.astype(F32).reshape(1, D))


PROJ_TILE = 512
S5_STEPS = 64
MERGE_TILE = 512
FFN_TILE = 512


def kernel(x, mem, positions, w_in, w_gate, b_gate, s5_lam_re, s5_lam_im, s5_log_dt, s5_b_re, s5_b_im, s5_c_re, s5_c_im, s5_d, w_glu, b_glu, w_mem_kv, w_proj_a, w_proj_b, w_proj_c, w_out, ln1_g, ln1_b, w_up, conv_w, conv_b, w_down, ln2_g, ln2_b):
    B, S, D = x.shape
    depth = w_in.shape[0]
    alpha = (2.0 * depth) ** 0.25
    h = x
    for l in range(depth):
        qT, k, vT, qiT, ki, wiT, u, qm = _proj(h, positions, w_in[l], min(PROJ_TILE, S))
        att = _dsa(qT, qiT, wiT, ki, k, vT)
        lbre, lbim, wb, wc = _s5_params(s5_lam_re[l], s5_lam_im[l], s5_log_dt[l], s5_b_re[l],
                                        s5_b_im[l], s5_c_re[l], s5_c_im[l])
        yb = _s5(u, lbre, lbim, wb, wc, s5_d[l], w_glu[l], b_glu[l], B, min(S5_STEPS, S))
        h = _merge(h, att, yb, qm, mem, w_mem_kv[l], w_gate[l], b_gate[l], w_proj_a[l],
                   w_proj_b[l], w_proj_c[l], w_out[l], ln1_g[l], ln1_b[l], alpha, min(MERGE_TILE, S))
        h = _ffn(h, w_up[l], conv_w[l], conv_b[l], w_down[l], ln2_g[l], ln2_b[l], alpha,
                 min(FFN_TILE, S))
    return h
```

```python
import functools
import math

import jax
import jax.numpy as jnp
from jax import lax
from jax.experimental import pallas as pl
from jax.experimental.pallas import tpu as pltpu

F32 = jnp.float32
BF16 = jnp.bfloat16
I32 = jnp.int32

CHUNK = 64
ATT_HEADS = 8
ATT_KV_HEADS = 2
ATT_HEAD_DIM = 64
ATT_REP = ATT_HEADS // ATT_KV_HEADS
IDX_HEADS = 4
IDX_DIM = 64
TOPK_MAX = 256
ROPE_THETA = 10000.0
S5_WIDTH = 512
S5_GROUP = 16
S5_GROUPS = S5_WIDTH // S5_GROUP
S5_STATE = 64
MEM_HEADS = 4
MEM_HEAD_DIM = 128
CONV_WIDTH = 3
N_BRANCH = 3
LN_EPS = 1e-5
NEG = -1e30
INT_MIN = -(2 ** 31)
LOG2E = math.log2(math.e)

LANES = 128
SUBLANES = 8
VMEM_LIMIT = 56 * 1024 * 1024

_Q0 = 0
_K0 = _Q0 + ATT_HEADS * ATT_HEAD_DIM
_V0 = _K0 + ATT_KV_HEADS * ATT_HEAD_DIM
_QI0 = _V0 + ATT_KV_HEADS * ATT_HEAD_DIM
_KI0 = _QI0 + IDX_HEADS * IDX_DIM
_WI0 = _KI0 + IDX_DIM
_U0 = _WI0 + IDX_HEADS
_QM0 = _U0 + S5_WIDTH
_END = _QM0 + MEM_HEADS * MEM_HEAD_DIM
_PAD = LANES - IDX_DIM - IDX_HEADS
_PU0 = _KI0 + LANES
_PQM0 = _PU0 + S5_WIDTH
_PEND = _PQM0 + MEM_HEADS * MEM_HEAD_DIM
V_ROWS = ATT_HEAD_DIM + 16


def _params(*sem):
    return pltpu.CompilerParams(dimension_semantics=sem, vmem_limit_bytes=VMEM_LIMIT)


def _proj_kernel(x_ref, pos_ref, invf_ref, w_ref,
                 qT_ref, k_ref, vT_ref, qiT_ref, ki_ref, wiT_ref, u_ref, qm_ref):
    T = x_ref.shape[1]
    y = jnp.dot(x_ref[0].astype(BF16), w_ref[...], preferred_element_type=F32)

    ang = pos_ref[0].astype(F32) * invf_ref[...]
    cos = jnp.cos(ang)
    sin = jnp.sin(ang)
    lane = lax.broadcasted_iota(I32, (T, LANES), 1)
    first = (lane % ATT_HEAD_DIM) < (ATT_HEAD_DIM // 2)
    sin = jnp.where(first, -sin, sin)

    def rope(z):
        partner = jnp.where(first, pltpu.roll(z, LANES - 32, 1), pltpu.roll(z, 32, 1))
        return z * cos + partner * sin

    att_scale = ATT_HEAD_DIM ** -0.5 * LOG2E
    for c in range(ATT_HEADS // 2):
        z = rope(y[:, _Q0 + c * LANES:_Q0 + (c + 1) * LANES]) * att_scale
        qT_ref[0, c * LANES:(c + 1) * LANES, :] = z.T.astype(BF16)
    k_ref[0] = rope(y[:, _K0:_K0 + LANES]).astype(BF16)
    vT = y[:, _V0:_V0 + LANES].T.astype(BF16)
    for g in range(ATT_KV_HEADS):
        vT_ref[0, g * V_ROWS:g * V_ROWS + ATT_HEAD_DIM, :] = vT[g * ATT_HEAD_DIM:(g + 1) * ATT_HEAD_DIM]
        vT_ref[0, g * V_ROWS + ATT_HEAD_DIM:(g + 1) * V_ROWS, :] = jnp.ones((V_ROWS - ATT_HEAD_DIM, T), BF16)
    idx_scale = IDX_DIM ** -0.5
    for c in range(IDX_HEADS // 2):
        z = rope(y[:, _QI0 + c * LANES:_QI0 + (c + 1) * LANES]) * idx_scale
        qiT_ref[0, c * LANES:(c + 1) * LANES, :] = z.T.astype(BF16)
    kw = y[:, _KI0:_KI0 + LANES]
    ki_ref[0] = jnp.where(lane < IDX_DIM, rope(kw), 0.0).astype(BF16)
    wiT_ref[0] = kw.T[IDX_DIM:IDX_DIM + IDX_HEADS] * (IDX_HEADS ** -0.5)
    u_ref[...] = y[:, _PU0:_PU0 + S5_WIDTH]
    qm_ref[0] = (y[:, _PQM0:_PEND] * (MEM_HEAD_DIM ** -0.5)).astype(BF16)


def _proj(x, positions, w_in, tile):
    B, S, D = x.shape
    nt = S // tile
    w_pad = jnp.concatenate(
        [w_in[:, :_U0], jnp.zeros((D, _PAD), w_in.dtype), w_in[:, _U0:]], axis=1).astype(BF16)
    half = ATT_HEAD_DIM // 2
    inv_freq = ROPE_THETA ** (-jnp.arange(half, dtype=F32) / half)
    invf = jnp.tile(inv_freq, LANES // half)[None, :]
    pos3 = positions.reshape(B, S, 1)
    QW = ATT_HEADS * ATT_HEAD_DIM
    IW = IDX_HEADS * IDX_DIM
    out_shape = (
        jax.ShapeDtypeStruct((B, QW, S), BF16),
        jax.ShapeDtypeStruct((B, S, LANES), BF16),
        jax.ShapeDtypeStruct((B, ATT_KV_HEADS * V_ROWS, S), BF16),
        jax.ShapeDtypeStruct((B, IW, S), BF16),
        jax.ShapeDtypeStruct((B, S, LANES), BF16),
        jax.ShapeDtypeStruct((B, IDX_HEADS, S), F32),
        jax.ShapeDtypeStruct((S, B * S5_WIDTH), F32),
        jax.ShapeDtypeStruct((B, S, MEM_HEADS * MEM_HEAD_DIM), BF16),
    )
    rows = lambda width: pl.BlockSpec((1, tile, width), lambda b, t: (b, t, 0))
    cols = lambda height: pl.BlockSpec((1, height, tile), lambda b, t: (b, 0, t))
    out_specs = (
        cols(QW), rows(LANES), cols(ATT_KV_HEADS * V_ROWS), cols(IW), rows(LANES), cols(IDX_HEADS),
        pl.BlockSpec((tile, S5_WIDTH), lambda b, t: (t, b)),
        rows(MEM_HEADS * MEM_HEAD_DIM),
    )
    return pl.pallas_call(
        _proj_kernel,
        out_shape=out_shape,
        grid=(B, nt),
        in_specs=[
            pl.BlockSpec((1, tile, D), lambda b, t: (b, t, 0)),
            pl.BlockSpec((1, tile, 1), lambda b, t: (b, t, 0)),
            pl.BlockSpec((1, LANES), lambda b, t: (0, 0)),
            pl.BlockSpec((D, _PEND), lambda b, t: (0, 0)),
        ],
        out_specs=out_specs,
        compiler_params=_params("parallel", "parallel"),
        name="proj",
    )(x, pos3, invf, w_pad)


DSA_TQ = 256
DSA_TK = 256
PLANE_KEYS = 32 * SUBLANES
FCOUNT_ROWS = 4 * SUBLANES


def _bit_transpose32(words):
    a = list(words)
    j, mask = 16, 0x0000FFFF
    while j:
        k = 0
        while k < 32:
            t = (a[k] ^ lax.shift_right_logical(a[k + j], jnp.int32(j))) & jnp.int32(mask)
            a[k] = a[k] ^ t
            a[k + j] = a[k + j] ^ lax.shift_left(t, jnp.int32(j))
            k = (k + j + 1) & ~j
        j >>= 1
        mask = (mask ^ (mask << j)) & 0xFFFFFFFF
    return a


def _dsa_kernel(qT_ref, qiT_ref, wiT_ref, ki_ref, k_ref, vT_ref, o_ref,
                wq_ref, wqi_ref, sc_ref, planes_ref, bias_ref, m_ref, acc_ref, alpha_ref,
                s_ref, p_ref, thr_ref, cnt_ref,
                *, topk):
    TQ, TK = DSA_TQ, DSA_TK
    HD = ATT_HEAD_DIM
    i = pl.program_id(1)
    n_kt = (i * TQ + TQ + TK - 1) // TK
    key_in_tile = lax.broadcasted_iota(I32, (TK, TQ), 0)
    qpos = i * TQ + lax.broadcasted_iota(I32, (1, TQ), 1)
    limit = (qpos // CHUNK + 1) * CHUNK
    wi = wiT_ref[0]

    def tile_start(kt):
        return pl.multiple_of(kt * TK, TK)

    zeros = jnp.zeros((HD, TQ), BF16)
    for h in range(ATT_HEADS):
        qh = qT_ref[0, h * HD:(h + 1) * HD, :]
        g = h // ATT_REP
        wq_ref[h] = jnp.concatenate([zeros] * g + [qh] + [zeros] * (ATT_KV_HEADS - 1 - g), axis=0)
    for h in range(IDX_HEADS):
        wqi_ref[h] = jnp.concatenate([qiT_ref[0, h * IDX_DIM:(h + 1) * IDX_DIM, :], zeros], axis=0)

    @pl.when((pl.program_id(0) == 0) & (i == 0))
    def _():
        planes_ref[...] = jnp.zeros(planes_ref.shape, I32)

    def score_tile(kt):
        ks = tile_start(kt)
        kk = ki_ref[0, pl.ds(ks, TK), :]
        acc = jnp.zeros((TK, TQ), F32)
        for h in range(IDX_HEADS):
            logit = jnp.dot(kk, wqi_ref[h], preferred_element_type=F32)
            acc = acc + wi[h:h + 1, :] * jnp.maximum(logit, 0.0)
        sc_ref[pl.ds(ks, TK), :] = jnp.where(ks + key_in_tile < limit, acc, NEG)

    def score_pair(j, carry):
        score_tile(2 * j)
        score_tile(2 * j + 1)
        return carry

    lax.fori_loop(0, n_kt // 2, score_pair, 0)

    @pl.when(n_kt % 2 == 1)
    def _():
        score_tile(n_kt - 1)

    def plane_tile(kt):
        bits = lax.bitcast_convert_type(sc_ref[pl.ds(tile_start(kt), TK), :], I32)
        ukey = jnp.where(bits < 0, -bits, bits ^ jnp.int32(INT_MIN))
        for g in range(TK // PLANE_KEYS):
            row0 = pl.multiple_of(kt * (TK // 32) + g * SUBLANES, SUBLANES)
            for c in range(TQ // LANES):
                rows = ukey[g * PLANE_KEYS:(g + 1) * PLANE_KEYS, c * LANES:(c + 1) * LANES]
                words = _bit_transpose32([rows[j * SUBLANES:(j + 1) * SUBLANES] for j in range(32)])
                for b in range(32):
                    planes_ref[b, pl.ds(row0, SUBLANES), c * LANES:(c + 1) * LANES] = words[31 - b]

    def plane_pair(j, carry):
        plane_tile(2 * j)
        plane_tile(2 * j + 1)
        return carry

    lax.fori_loop(0, n_kt // 2, plane_pair, 0)

    @pl.when(n_kt % 2 == 1)
    def _():
        plane_tile(n_kt - 1)

    def key_to_float(key):
        bits = jnp.where(key < 0, jnp.int32(INT_MIN) - key, key)
        return lax.bitcast_convert_type(bits, F32)

    def popcount_rows(w):
        return jnp.sum(lax.population_count(w), axis=0, keepdims=True)

    def radix_select(n_words):
        word_row = lax.broadcasted_iota(I32, (n_words, TQ), 0)
        cand0 = jnp.where(word_row < n_kt * (TK // 32), jnp.int32(-1), jnp.int32(0))

        def bit_body(it, carry):
            cand, need, ukey_thr = carry
            b = 31 - it
            ones = cand & planes_ref[b, :n_words, :]
            c1 = popcount_rows(ones)
            ok = c1 >= need
            cand = jnp.where(ok, ones, cand ^ ones)
            need = jnp.where(ok, need, need - c1)
            ukey_thr = ukey_thr | jnp.left_shift(ok.astype(I32), b)
            return cand, need, ukey_thr

        cand, need, ukey_thr = lax.fori_loop(
            0, 32, bit_body, (cand0, jnp.full((1, TQ), topk, I32), jnp.zeros((1, TQ), I32)),
            unroll=8)
        thr_key = ukey_thr ^ jnp.int32(INT_MIN)
        cnt_key = (topk - need) + popcount_rows(cand)
        thr_ref[...] = jnp.where(thr_key == INT_MIN, -jnp.inf, key_to_float(thr_key))
        cnt_ref[...] = cnt_key.astype(F32)

    all_words = planes_ref.shape[1]
    half_words = all_words // 2
    few_keys = n_kt * (TK // 32) <= half_words

    @pl.when(few_keys)
    def _():
        radix_select(half_words)

    @pl.when(jnp.logical_not(few_keys))
    def _():
        radix_select(all_words)

    def count(pred):
        def body(kt, cnt):
            m = jnp.where(pred(sc_ref[pl.ds(tile_start(kt), TK), :]), 1.0, 0.0)
            return cnt + m.reshape(TK // FCOUNT_ROWS, FCOUNT_ROWS, TQ).sum(axis=0)
        cnt = lax.fori_loop(0, n_kt, body, jnp.zeros((FCOUNT_ROWS, TQ), F32))
        return jnp.sum(cnt, axis=0, keepdims=True)

    def plain_bias():
        thr = thr_ref[...]

        def bias_tile(kt, cnt):
            ks = tile_start(kt)
            above = sc_ref[pl.ds(ks, TK), :] >= thr
            bias_ref[pl.ds(ks, TK), :] = jnp.where(above & (ks + key_in_tile < limit), 0.0, NEG)
            m = jnp.where(above, 1.0, 0.0)
            return cnt + m.reshape(TK // FCOUNT_ROWS, FCOUNT_ROWS, TQ).sum(axis=0)
        cnt = lax.fori_loop(0, n_kt, bias_tile, jnp.zeros((FCOUNT_ROWS, TQ), F32))
        return jnp.sum(cnt, axis=0, keepdims=True)

    disagree = jnp.max(jnp.abs(plain_bias() - cnt_ref[...])) > 0.0

    @pl.when(disagree)
    def _():
        def body(it, carry):
            lo, cnt_lo = carry
            cand = lo + jnp.left_shift(jnp.int32(1), 31 - it)
            cand_f = key_to_float(cand)
            c = count(lambda s: s >= cand_f)
            ok = c >= float(topk)
            return jnp.where(ok, cand, lo), jnp.where(ok, c, cnt_lo)
        lo, cnt_lo = lax.fori_loop(0, 32, body, (jnp.full((1, TQ), INT_MIN, I32),
                                                 jnp.full((1, TQ), n_kt * TK, I32).astype(F32)))
        thr_ref[...] = jnp.where(lo == INT_MIN, -jnp.inf, key_to_float(lo))
        cnt_ref[...] = cnt_lo

    thr = thr_ref[...]
    has_ties = jnp.max(cnt_ref[...]) > float(topk)

    @pl.when(disagree & jnp.logical_not(has_ties))
    def _():
        plain_bias()

    @pl.when(has_ties)
    def _():
        need = float(topk) - count(lambda s: s > thr)
        ri = lax.broadcasted_iota(I32, (TK, TK), 0)
        ci = lax.broadcasted_iota(I32, (TK, TK), 1)
        tri = jnp.where(ci < ri, 1.0, 0.0).astype(BF16)

        def bias_tile(kt, run):
            ks = tile_start(kt)
            s = sc_ref[pl.ds(ks, TK), :]
            eq = jnp.where(s == thr, 1.0, 0.0)
            before = jnp.dot(tri, eq.astype(BF16), preferred_element_type=F32) + run
            sel = (s > thr) | ((s == thr) & (before < need))
            sel = sel & (ks + key_in_tile < limit)
            bias_ref[pl.ds(ks, TK), :] = jnp.where(sel, 0.0, NEG)
            return run + jnp.sum(eq, axis=0, keepdims=True)
        lax.fori_loop(0, n_kt, bias_tile, jnp.zeros((1, TQ), F32))

    m_ref[...] = jnp.full(m_ref.shape, -3e38, F32)
    acc_ref[...] = jnp.zeros(acc_ref.shape, F32)

    def scores(kt, slot):
        kk = k_ref[0, pl.ds(tile_start(jnp.minimum(kt, n_kt - 1)), TK), :]
        bias = bias_ref[pl.ds(tile_start(jnp.minimum(kt, n_kt)), TK), :]
        for h in range(ATT_HEADS):
            s_ref[slot, h] = jnp.dot(kk, wq_ref[h], preferred_element_type=F32) + bias

    def softmax(slot):
        for h in range(ATT_HEADS):
            for c in range(TQ // LANES):
                ql = slice(c * LANES, (c + 1) * LANES)
                s = s_ref[slot, h, :, ql]
                m_old = m_ref[h:h + 1, ql]
                m_new = jnp.maximum(m_old, jnp.max(s, axis=0, keepdims=True))
                p_ref[slot, h, :, ql] = jnp.exp2(s - m_new).astype(BF16)
                alpha_ref[slot, h:h + 1, ql] = jnp.exp2(m_old - m_new)
                m_ref[h:h + 1, ql] = m_new

    def weighted_values(kt, slot):
        ks = tile_start(jnp.clip(kt, 0, n_kt - 1))
        for h in range(ATT_HEADS):
            g = h // ATT_REP
            pv = jnp.dot(vT_ref[0, g * V_ROWS:(g + 1) * V_ROWS, pl.ds(ks, TK)], p_ref[slot, h],
                         preferred_element_type=F32)
            acc_ref[h] = alpha_ref[slot, h:h + 1, :] * acc_ref[h] + pv

    bias_ref[pl.ds(tile_start(n_kt), TK), :] = jnp.full((TK, TQ), NEG, F32)
    scores(0, 0)
    p_ref[1] = jnp.zeros(p_ref.shape[1:], BF16)
    alpha_ref[1] = jnp.ones(alpha_ref.shape[1:], F32)

    def att_pair(j, carry):
        scores(2 * j + 1, 1)
        softmax(0)
        weighted_values(2 * j - 1, 1)
        scores(2 * j + 2, 0)
        softmax(1)
        weighted_values(2 * j, 0)
        return carry

    n_pairs = n_kt // 2
    lax.fori_loop(0, n_pairs, att_pair, 0)
    weighted_values(2 * n_pairs - 1, 1)

    @pl.when(n_kt % 2 == 1)
    def _():
        softmax(0)
        weighted_values(n_kt - 1, 0)

    for c in range(ATT_HEADS // 2):
        pair = [acc_ref[h, :HD, :] / acc_ref[h, HD:HD + 1, :] for h in (2 * c, 2 * c + 1)]
        o_ref[0, :, c * LANES:(c + 1) * LANES] = jnp.concatenate(pair, axis=0).T.astype(BF16)


def _dsa(qT, qiT, wiT, ki, k, vT):
    B, QW, S = qT.shape
    TQ = DSA_TQ
    topk = min(TOPK_MAX, S // 4)
    assert S % TQ == 0 and TQ == DSA_TK and DSA_TK >= topk
    return pl.pallas_call(
        functools.partial(_dsa_kernel, topk=topk),
        out_shape=jax.ShapeDtypeStruct((B, S, QW), BF16),
        grid=(B, S // TQ),
        in_specs=[
            pl.BlockSpec((1, QW, TQ), lambda b, i: (b, 0, i)),
            pl.BlockSpec((1, IDX_HEADS * IDX_DIM, TQ), lambda b, i: (b, 0, i)),
            pl.BlockSpec((1, IDX_HEADS, TQ), lambda b, i: (b, 0, i)),
            pl.BlockSpec((1, S, LANES), lambda b, i: (b, 0, 0)),
            pl.BlockSpec((1, S, LANES), lambda b, i: (b, 0, 0)),
            pl.BlockSpec((1, ATT_KV_HEADS * V_ROWS, S), lambda b, i: (b, 0, 0)),
        ],
        out_specs=pl.BlockSpec((1, TQ, QW), lambda b, i: (b, i, 0)),
        scratch_shapes=[
            pltpu.VMEM((ATT_HEADS, LANES, TQ), BF16),
            pltpu.VMEM((IDX_HEADS, LANES, TQ), BF16),
            pltpu.VMEM((S, TQ), F32),
            pltpu.VMEM((32, S // 32, TQ), I32),
            pltpu.VMEM((S + DSA_TK, TQ), F32),
            pltpu.VMEM((ATT_HEADS, TQ), F32),
            pltpu.VMEM((ATT_HEADS, V_ROWS, TQ), F32),
            pltpu.VMEM((2, ATT_HEADS, TQ), F32),
            pltpu.VMEM((2, ATT_HEADS, DSA_TK, TQ), F32),
            pltpu.VMEM((2, ATT_HEADS, DSA_TK, TQ), BF16),
            pltpu.VMEM((1, TQ), F32),
            pltpu.VMEM((1, TQ), F32),
        ],
        compiler_params=_params("arbitrary", "arbitrary"),
        name="dsa",
    )(qT, qiT, wiT, ki, k, vT)


S5_SLABS = S5_WIDTH // LANES
S5_SLAB_STATES = (LANES // S5_GROUP) * S5_STATE


def _s5_param_kernel(lre_ref, lim_ref, ldt_ref, bre_ref, bim_ref,
                     lbre_ref, lbim_ref, bbre_ref, bbim_ref):
    lre = jnp.minimum(lre_ref[...], -1e-4)
    lim = lim_ref[...]
    dt = jnp.exp(ldt_ref[...])
    mag = jnp.exp(lre * dt)
    lbre = mag * jnp.cos(lim * dt)
    lbim = mag * jnp.sin(lim * dt)
    nre = lbre - 1.0
    den = lre * lre + lim * lim
    cre = (nre * lre + lbim * lim) / den
    cim = (lbim * lre - nre * lim) / den
    lbre_ref[...] = lbre
    lbim_ref[...] = lbim
    bbre_ref[...] = cre * bre_ref[...] - cim * bim_ref[...]
    bbim_ref[...] = cre * bim_ref[...] + cim * bre_ref[...]


def _s5_params(lam_re, lam_im, log_dt, b_re, b_im, c_re, c_im):
    G, P, H = S5_GROUPS, S5_STATE, S5_GROUP
    n = G * P
    flat = lambda a: a.astype(F32).reshape(1, n)
    ldt = jnp.repeat(log_dt.astype(F32), P).reshape(1, n)
    bt = lambda a: a.astype(F32).transpose(2, 0, 1).reshape(H, n)
    lbre, lbim, bbre, bbim = pl.pallas_call(
        _s5_param_kernel,
        out_shape=(jax.ShapeDtypeStruct((1, n), F32), jax.ShapeDtypeStruct((1, n), F32),
                   jax.ShapeDtypeStruct((H, n), F32), jax.ShapeDtypeStruct((H, n), F32)),
        name="s5_params",
    )(flat(lam_re), flat(lam_im), ldt, bt(b_re), bt(b_im))

    gl = LANES // H
    eye = jnp.eye(gl, dtype=F32)
    bb = jnp.stack([bbre, bbim]).reshape(2, H, S5_SLABS, gl, P)
    wb = jnp.einsum("ahjgp,gk->jghakp", bb, eye).reshape(S5_SLABS, LANES, 2 * gl * P)
    cc = jnp.stack([c_re.astype(F32), -c_im.astype(F32)]).reshape(2, S5_SLABS, gl, H, P)
    wc = jnp.einsum("ajghp,gk->jagpkh", cc, eye).reshape(S5_SLABS, 2 * gl * P, LANES)
    lam = lambda a: jnp.broadcast_to(a.reshape(S5_SLABS, 1, gl * P), (S5_SLABS, SUBLANES, gl * P))
    return lam(lbre), lam(lbim), wb.astype(BF16), wc.astype(BF16)


def _s5_kernel(u_ref, lbre_ref, lbim_ref, wb_ref, wc_ref, d_ref, wglu_ref, bglu_ref, o_ref,
               state_ref, h_ref, y_ref, u_sc, hb_ref, *, steps, batch):
    NS = S5_SLAB_STATES
    t = pl.program_id(0)

    @pl.when(t == 0)
    def _():
        state_ref[...] = jnp.zeros(state_ref.shape, F32)

    W = S5_WIDTH
    for b in range(batch):
        for j in range(S5_SLABS):
            c0 = b * W + j * LANES
            u_sc[j, pl.ds(b, steps, stride=batch), :] = u_ref[:, c0:c0 + LANES]
    for j in range(S5_SLABS):
        h_ref[j] = jnp.dot(u_sc[j].astype(BF16), wb_ref[j], preferred_element_type=F32)

    for j in range(S5_SLABS):
        lre = lbre_ref[j]
        lim = lbim_ref[j]

        def step2(s2, carry):
            hre, him = carry
            r0 = pl.multiple_of(s2 * (2 * batch), 2 * batch)
            bre = h_ref[j, pl.ds(r0, 2 * batch), :NS]
            bim = h_ref[j, pl.ds(r0, 2 * batch), NS:]
            are = lre * hre - lim * him + bre[:batch]
            aim = lre * him + lim * hre + bim[:batch]
            nre = lre * are - lim * aim + bre[batch:]
            nim = lre * aim + lim * are + bim[batch:]
            hb_ref[j, pl.ds(r0, 2 * batch), :NS] = jnp.concatenate([are, nre], axis=0).astype(BF16)
            hb_ref[j, pl.ds(r0, 2 * batch), NS:] = jnp.concatenate([aim, nim], axis=0).astype(BF16)
            return nre, nim

        hre, him = lax.fori_loop(0, steps // 2, step2, (state_ref[j, :, :NS], state_ref[j, :, NS:]),
                                 unroll=2)
        state_ref[j, :, :NS] = hre
        state_ref[j, :, NS:] = him

    ys = [jnp.dot(hb_ref[j], wc_ref[j], preferred_element_type=F32) for j in range(S5_SLABS)]
    u = jnp.concatenate([u_sc[j] for j in range(S5_SLABS)], axis=1)
    y = jnp.concatenate(ys, axis=1) + d_ref[...] * u
    y = jax.nn.gelu(y)
    gate = jnp.dot(y.astype(BF16), wglu_ref[...], preferred_element_type=F32) + bglu_ref[...]
    y = y * jax.nn.sigmoid(gate)
    for j in range(S5_SLABS):
        y_ref[j] = y[:, j * LANES:(j + 1) * LANES]
    for b in range(batch):
        for j in range(S5_SLABS):
            o_ref[b, :, j * LANES:(j + 1) * LANES] = (
                y_ref[j, pl.ds(b, steps, stride=batch), :].astype(BF16))


def _s5(u2, lbre, lbim, wb, wc, d_skip, w_glu, b_glu, batch, steps):
    S, W = u2.shape[0], S5_WIDTH
    NS2 = 2 * S5_SLAB_STATES
    const = lambda *shape: pl.BlockSpec(shape, lambda t: (0,) * len(shape))
    return pl.pallas_call(
        functools.partial(_s5_kernel, steps=steps, batch=batch),
        out_shape=jax.ShapeDtypeStruct((batch, S, W), BF16),
        grid=(S // steps,),
        in_specs=[
            pl.BlockSpec((steps, batch * W), lambda t: (t, 0)),
            const(S5_SLABS, SUBLANES, S5_SLAB_STATES),
            const(S5_SLABS, SUBLANES, S5_SLAB_STATES),
            const(S5_SLABS, LANES, NS2),
            const(S5_SLABS, NS2, LANES),
            const(1, W),
            const(W, W),
            const(1, W),
        ],
        out_specs=pl.BlockSpec((batch, steps, W), lambda t: (0, t, 0)),
        scratch_shapes=[
            pltpu.VMEM((S5_SLABS, batch, NS2), F32),
            pltpu.VMEM((S5_SLABS, steps * batch, NS2), F32),
            pltpu.VMEM((S5_SLABS, steps * batch, LANES), F32),
            pltpu.VMEM((S5_SLABS, steps * batch, LANES), F32),
            pltpu.VMEM((S5_SLABS, steps * batch, NS2), BF16),
        ],
        compiler_params=_params("arbitrary"),
        name="s5",
    )(u2, lbre, lbim, wb, wc, d_skip.astype(F32).reshape(1, W), w_glu.astype(BF16),
      b_glu.astype(F32).reshape(1, W))


def _layer_norm(z, g, b):
    mu = jnp.mean(z, axis=-1, keepdims=True)
    zc = z - mu
    var = jnp.mean(zc * zc, axis=-1, keepdims=True)
    return zc * lax.rsqrt(var + LN_EPS) * g + b


def _merge_kernel(x_ref, a_ref, yb_ref, qm_ref, mem_ref, wkv_ref, wg_ref, bg_ref,
                  wpa_ref, wpb_ref, wpc_ref, wout_ref, g_ref, b_ref, o_ref, kv_ref, *, alpha):
    D = x_ref.shape[2]
    HW = MEM_HEADS * MEM_HEAD_DIM

    @pl.when(pl.program_id(1) == 0)
    def _():
        kv_ref[...] = jnp.dot(mem_ref[0].astype(BF16), wkv_ref[...],
                              preferred_element_type=F32).astype(BF16)

    x = x_ref[0]
    xb = x.astype(BF16)
    qm = qm_ref[0]
    heads = []
    for h in range(MEM_HEADS):
        sl = slice(h * MEM_HEAD_DIM, (h + 1) * MEM_HEAD_DIM)
        s = lax.dot_general(qm[:, sl], kv_ref[:, sl], (((1,), (1,)), ((), ())),
                            preferred_element_type=F32)
        p = jnp.exp(s - jnp.max(s, axis=1, keepdims=True))
        p = p / jnp.sum(p, axis=1, keepdims=True)
        heads.append(jnp.dot(p.astype(BF16), kv_ref[:, HW + h * MEM_HEAD_DIM:HW + (h + 1) * MEM_HEAD_DIM],
                             preferred_element_type=F32))
    c_in = jnp.concatenate(heads, axis=1).astype(BF16)

    y_a = jnp.dot(a_ref[0], wpa_ref[...], preferred_element_type=F32)
    y_b = jnp.dot(yb_ref[0], wpb_ref[...], preferred_element_type=F32)
    y_c = jnp.dot(c_in, wpc_ref[...], preferred_element_type=F32)
    merged = None
    for k, yk in enumerate((y_a, y_b, y_c)):
        gk = jax.nn.sigmoid(jnp.dot(xb, wg_ref[:, k * D:(k + 1) * D], preferred_element_type=F32)
                            + bg_ref[:, k * D:(k + 1) * D])
        merged = gk * yk if merged is None else merged + gk * yk
    mix = jnp.dot(merged.astype(BF16), wout_ref[...], preferred_element_type=F32)
    o_ref[0] = _layer_norm(alpha * x + mix, g_ref[...], b_ref[...])


def _merge(x, att, yb, qm, mem, w_mem_kv, w_gate, b_gate, w_proj_a, w_proj_b, w_proj_c, w_out,
           ln_g, ln_b, alpha, tile):
    B, S, D = x.shape
    n_mem = mem.shape[1]
    HW = MEM_HEADS * MEM_HEAD_DIM
    const = lambda *shape: pl.BlockSpec(shape, lambda b, t: (0,) * len(shape))
    row = lambda width: pl.BlockSpec((1, tile, width), lambda b, t: (b, t, 0))
    return pl.pallas_call(
        functools.partial(_merge_kernel, alpha=alpha),
        out_shape=jax.ShapeDtypeStruct((B, S, D), F32),
        grid=(B, S // tile),
        in_specs=[
            row(D), row(att.shape[2]), row(yb.shape[2]), row(HW),
            pl.BlockSpec((1, n_mem, D), lambda b, t: (b, 0, 0)),
            const(D, 2 * HW), const(D, N_BRANCH * D), const(1, N_BRANCH * D),
            const(att.shape[2], D), const(yb.shape[2], D), const(HW, D), const(D, D),
            const(1, D), const(1, D),
        ],
        out_specs=row(D),
        scratch_shapes=[pltpu.VMEM((n_mem, 2 * HW), BF16)],
        compiler_params=_params("parallel", "arbitrary"),
        name="merge",
    )(x, att, yb, qm, mem, w_mem_kv.astype(BF16), w_gate.astype(BF16),
      b_gate.astype(F32).reshape(1, -1), w_proj_a.astype(BF16), w_proj_b.astype(BF16),
      w_proj_c.astype(BF16), w_out.astype(BF16), ln_g.astype(F32).reshape(1, D),
      ln_b.astype(F32).reshape(1, D))


FFN_CHUNK = 256


def _ffn_kernel(h_ref, wup_ref, cw_ref, cb_ref, wdn_ref, g_ref, b_ref, o_ref,
                act_ref, tail_ref, *, alpha, ffn_dim):
    T = h_ref.shape[1]
    FC = FFN_CHUNK
    HALO = SUBLANES

    @pl.when(pl.program_id(1) == 0)
    def _():
        tail_ref[...] = jnp.zeros(tail_ref.shape, F32)

    h = h_ref[0]
    hb = h.astype(BF16)
    halo_row = lax.broadcasted_iota(I32, (HALO, FC), 0)

    def conv_cols(c0):
        up = jnp.dot(hb, wup_ref[:, c0:c0 + FC], preferred_element_type=F32)
        tail = tail_ref[:, c0:c0 + FC]
        tail_ref[:, c0:c0 + FC] = up[T - HALO:]
        out = cb_ref[:, c0:c0 + FC] + cw_ref[CONV_WIDTH - 1:CONV_WIDTH, c0:c0 + FC] * up
        for d in range(1, CONV_WIDTH):
            rolled = pltpu.roll(up, d, 0)
            top = jnp.where(halo_row < d, pltpu.roll(tail, d, 0), rolled[:HALO])
            delayed = jnp.concatenate([top, rolled[HALO:]], axis=0)
            out = out + cw_ref[CONV_WIDTH - 1 - d:CONV_WIDTH - d, c0:c0 + FC] * delayed
        return out

    for c in range(ffn_dim // FC):
        gate = conv_cols(c * FC)
        up = conv_cols(ffn_dim + c * FC)
        act_ref[:, c * FC:(c + 1) * FC] = (gate * jax.nn.sigmoid(gate) * up).astype(BF16)
    f = jnp.dot(act_ref[...], wdn_ref[...], preferred_element_type=F32)
    o_ref[0] = _layer_norm(alpha * h + f, g_ref[...], b_ref[...])


def _ffn(h, w_up, conv_w, conv_b, w_down, ln_g, ln_b, alpha, tile):
    B, S, D = h.shape
    F2 = w_up.shape[1]
    const = lambda *shape: pl.BlockSpec(shape, lambda b, t: (0,) * len(shape))
    row = pl.BlockSpec((1, tile, D), lambda b, t: (b, t, 0))
    return pl.pallas_call(
        functools.partial(_ffn_kernel, alpha=alpha, ffn_dim=F2 // 2),
        out_shape=jax.ShapeDtypeStruct((B, S, D), F32),
        grid=(B, S // tile),
        in_specs=[row, const(D, F2), const(CONV_WIDTH, F2), const(1, F2), const(F2 // 2, D),
                  const(1, D), const(1, D)],
        out_specs=row,
        scratch_shapes=[
            pltpu.VMEM((tile, F2 // 2), BF16),
            pltpu.VMEM((SUBLANES, F2), F32),
        ],
        compiler_params=_params("parallel", "arbitrary"),
        name="ffn",
    )(h, w_up.astype(BF16), conv_w.astype(F32), conv_b.astype(F32).reshape(1, F2),
      w_down.astype(BF16), ln_g.astype(F32).reshape(1, D), ln_b.astype(F32).reshape(1, D))


PROJ_TILE = 512
S5_STEPS = 64
MERGE_TILE = 512
FFN_TILE = 512


def kernel(x, mem, positions, w_in, w_gate, b_gate, s5_lam_re, s5_lam_im, s5_log_dt, s5_b_re, s5_b_im, s5_c_re, s5_c_im, s5_d, w_glu, b_glu, w_mem_kv, w_proj_a, w_proj_b, w_proj_c, w_out, ln1_g, ln1_b, w_up, conv_w, conv_b, w_down, ln2_g, ln2_b):
    B, S, D = x.shape
    depth = w_in.shape[0]
    alpha = (2.0 * depth) ** 0.25
    h = x
    for l in range(depth):
        qT, k, vT, qiT, ki, wiT, u, qm = _proj(h, positions, w_in[l], min(PROJ_TILE, S))
        att = _dsa(qT, qiT, wiT, ki, k, vT)
        lbre, lbim, wb, wc = _s5_params(s5_lam_re[l], s5_lam_im[l], s5_log_dt[l], s5_b_re[l],
                                        s5_b_im[l], s5_c_re[l], s5_c_im[l])
        yb = _s5(u, lbre, lbim, wb, wc, s5_d[l], w_glu[l], b_glu[l], B, min(S5_STEPS, S))
        h = _merge(h, att, yb, qm, mem, w_mem_kv[l], w_gate[l], b_gate[l], w_proj_a[l],
                   w_proj_b[l], w_proj_c[l], w_out[l], ln1_g[l], ln1_b[l], alpha, min(MERGE_TILE, S))
        h = _ffn(h, w_up[l], conv_w[l], conv_b[l], w_down[l], ln2_g[l], ln2_b[l], alpha,
                 min(FFN_TILE, S))
    return h
```
